```python
import jax, jax.numpy as jnp
from jax import lax
import numpy as np

D_MODEL = 1024
BATCH = 16
SEQ = 2048
DEPTH = 1
DEC_BATCH = 16
DEC_SEQ = 64
PAST_LEN = 4096

CHUNK = 64
HEAD_DIM = 64
SB_HEADS = 8
FOX_HEADS = 8
SB_WIDTH = SB_HEADS * HEAD_DIM
FOX_WIDTH = FOX_HEADS * HEAD_DIM
MIX_WIDTH = SB_WIDTH + FOX_WIDTH
IN_COLS = 3 * SB_WIDTH + 3 * FOX_WIDTH + FOX_HEADS
QBLOCK = 128
N_MEM = 256
MEM_HEADS = 4
MEM_HEAD_DIM = D_MODEL // MEM_HEADS
PEER_HEADS = 8
PEER_NKEYS = 128
PEER_EXPERTS = PEER_NKEYS * PEER_NKEYS
PEER_TOPK = 16
PEER_QDIM = 256
PEER_HALF = PEER_QDIM // 2
TOK_BLOCK = 128
DN_ALPHA = (2.0 * DEPTH) ** 0.25
DN_BETA = (8.0 * DEPTH) ** -0.25
LN_EPS = 1e-5
GN_EPS = 1e-6

kernel_name = 'stickbreak_fox_peer_stream_encoder_step'

F32 = jnp.float32


def _layer_norm(x, g, b):
    xf = x.astype(F32)
    mu = jnp.mean(xf, axis=-1, keepdims=True)
    var = jnp.mean(jnp.square(xf - mu), axis=-1, keepdims=True)
    return ((xf - mu) * lax.rsqrt(var + LN_EPS) * g.astype(F32) + b.astype(F32)).astype(x.dtype)


def _group_rms(o, g):
    return o * lax.rsqrt(jnp.mean(jnp.square(o), axis=-1, keepdims=True) + GN_EPS) * g.astype(F32)


def _query_blocks(q, q_pos):
    B, Tq, H, dh = q.shape
    qb = min(QBLOCK, Tq)
    nb = Tq // qb
    qs = q.astype(F32).reshape(B, nb, qb, H, dh).transpose(1, 0, 2, 3, 4)
    return qs, q_pos.reshape(nb, qb), nb, qb


def _stick_breaking_attn(q, k, v, q_pos, k_pos):
    B, Tq, H, dh = q.shape
    qs, ps, nb, qb = _query_blocks(q, q_pos)
    kf = k.astype(F32)
    vf = v.astype(F32)
    scale = HEAD_DIM ** -0.5

    def one(args):
        qblk, pblk = args
        z = jnp.einsum('bqhd,bkhd->bhqk', qblk, kf) * scale
        mask = (k_pos[None, :] < pblk[:, None])[None, None]
        log_keep = jnp.where(mask, -jax.nn.softplus(z), 0.0)
        later = lax.cumsum(log_keep, axis=3, reverse=True) - log_keep
        w = jnp.where(mask, jnp.exp(jax.nn.log_sigmoid(z) + later), 0.0)
        return jnp.einsum('bhqk,bkhd->bqhd', w, vf)

    o = lax.map(one, (qs, ps))
    return o.transpose(1, 0, 2, 3, 4).reshape(B, Tq, H * dh)


def _forgetting_attn(q, k, v, cq, ck, q_pos, k_pos):
    B, Tq, H, dh = q.shape
    qs, ps, nb, qb = _query_blocks(q, q_pos)
    cqs = cq.reshape(B, nb, qb, H).transpose(1, 0, 3, 2)
    ckT = ck.transpose(0, 2, 1)
    kf = k.astype(F32)
    vf = v.astype(F32)
    scale = HEAD_DIM ** -0.5

    def one(args):
        qblk, pblk, cblk = args
        s = jnp.einsum('bqhd,bkhd->bhqk', qblk, kf) * scale + cblk[..., None] - ckT[:, :, None, :]
        mask = (k_pos[None, :] <= pblk[:, None])[None, None]
        p = jax.nn.softmax(jnp.where(mask, s, -jnp.inf), axis=-1)
        return jnp.einsum('bhqk,bkhd->bqhd', p, vf)

    o = lax.map(one, (qs, ps, cqs))
    return o.transpose(1, 0, 2, 3, 4).reshape(B, Tq, H * dh)


def _mixer(x, sb_k_past, sb_v_past, fx_k_past, fx_v_past, fx_logf_past, w_in, b_f, w_gn, w_out):
    B, T, _ = x.shape
    P = sb_k_past.shape[1]
    proj = x @ w_in
    cuts = [SB_WIDTH, 2 * SB_WIDTH, 3 * SB_WIDTH, 3 * SB_WIDTH + FOX_WIDTH,
            3 * SB_WIDTH + 2 * FOX_WIDTH, 3 * SB_WIDTH + 3 * FOX_WIDTH]
    q_sb, k_sb, v_sb, q_fx, k_fx, v_fx, f_logit = jnp.split(proj, cuts, axis=-1)
    hs = lambda a, h: a.reshape(B, T, h, HEAD_DIM)
    q_sb, k_sb, v_sb = hs(q_sb, SB_HEADS), hs(k_sb, SB_HEADS), hs(v_sb, SB_HEADS)
    q_fx, k_fx, v_fx = hs(q_fx, FOX_HEADS), hs(k_fx, FOX_HEADS), hs(v_fx, FOX_HEADS)
    logf = jax.nn.log_sigmoid((f_logit + b_f).astype(F32))

    k_pos = jnp.arange(P + T, dtype=jnp.int32)
    q_pos = P + jnp.arange(T, dtype=jnp.int32)
    k_sb_all = jnp.concatenate([sb_k_past.astype(x.dtype), k_sb], axis=1)
    v_sb_all = jnp.concatenate([sb_v_past.astype(x.dtype), v_sb], axis=1)
    k_fx_all = jnp.concatenate([fx_k_past.astype(x.dtype), k_fx], axis=1)
    v_fx_all = jnp.concatenate([fx_v_past.astype(x.dtype), v_fx], axis=1)
    c_all = lax.cumsum(jnp.concatenate([fx_logf_past.astype(F32), logf], axis=1), axis=1)

    o_sb = _stick_breaking_attn(q_sb, k_sb_all, v_sb_all, q_pos, k_pos)
    o_fx = _forgetting_attn(q_fx, k_fx_all, v_fx_all, c_all[:, P:], c_all, q_pos, k_pos)
    o = jnp.concatenate([_group_rms(o_sb, w_gn[:SB_WIDTH]), _group_rms(o_fx, w_gn[SB_WIDTH:])], axis=-1)
    y = o.astype(x.dtype) @ w_out
    return y, (k_sb, v_sb, k_fx, v_fx, logf.astype(x.dtype))


def _mem_kv(mem, w_mk, w_mv):
    B, M, _ = mem.shape
    mk = (mem @ w_mk).reshape(B, M, MEM_HEADS, MEM_HEAD_DIM)
    mv = (mem @ w_mv).reshape(B, M, MEM_HEADS, MEM_HEAD_DIM)
    return mk, mv


def _mem_attn(x, mk, mv, w_mq, w_mo):
    B, T, _ = x.shape
    q = (x @ w_mq).reshape(B, T, MEM_HEADS, MEM_HEAD_DIM).astype(F32)
    s = jnp.einsum('bthd,bmhd->bhtm', q, mk.astype(F32)) * (MEM_HEAD_DIM ** -0.5)
    p = jax.nn.softmax(s, axis=-1)
    o = jnp.einsum('bhtm,bmhd->bthd', p, mv.astype(F32)).reshape(B, T, D_MODEL)
    return o.astype(x.dtype) @ w_mo


def _peer(x, w_pq, keys_a, keys_b, u, v):
    B, T, D = x.shape
    n = B * T
    xf = x.reshape(n, D)
    q = (xf @ w_pq).astype(F32).reshape(n, PEER_HEADS, 2, PEER_HALF)
    sa = jnp.einsum('nhd,hkd->nhk', q[:, :, 0], keys_a.astype(F32))
    sb = jnp.einsum('nhd,hkd->nhk', q[:, :, 1], keys_b.astype(F32))
    va, ia = lax.top_k(sa, PEER_TOPK)
    vb, ib = lax.top_k(sb, PEER_TOPK)
    cand = (va[..., :, None] + vb[..., None, :]).reshape(n, PEER_HEADS, PEER_TOPK * PEER_TOPK)
    cidx = (ia[..., :, None] * PEER_NKEYS + ib[..., None, :]).reshape(n, PEER_HEADS, PEER_TOPK * PEER_TOPK)
    top, pos = lax.top_k(cand, PEER_TOPK)
    idx = jnp.take_along_axis(cidx, pos, axis=-1)
    g = jax.nn.softmax(top, axis=-1)

    blk = min(TOK_BLOCK, n)
    pad = (-n) % blk
    nb = (n + pad) // blk
    xb = jnp.pad(xf, ((0, pad), (0, 0))).reshape(nb, blk, D)
    ibk = jnp.pad(idx, ((0, pad), (0, 0), (0, 0))).reshape(nb, blk, PEER_HEADS, PEER_TOPK)
    gbk = jnp.pad(g, ((0, pad), (0, 0), (0, 0))).reshape(nb, blk, PEER_HEADS, PEER_TOPK)

    def one(args):
        xx, ii, gg = args
        ue = jnp.take(u, ii, axis=0).astype(F32)
        h = jax.nn.gelu(jnp.einsum('nd,nhkd->nhk', xx.astype(F32), ue), approximate=False) * gg
        ve = jnp.take(v, ii, axis=0).astype(F32)
        return jnp.einsum('nhk,nhkd->nd', h, ve)

    out = lax.map(one, (xb, ibk, gbk)).reshape(nb * blk, D)[:n]
    return out.reshape(B, T, D).astype(x.dtype)


def _layer(x, sb_k_past, sb_v_past, fx_k_past, fx_v_past, fx_logf_past, mem_k, mem_v,
           w_in, b_f, w_gn, w_out, ln1_g, ln1_b, w_mq, w_mo, ln2_g, ln2_b,
           w_pq, keys_a, keys_b, u, v, ln3_g, ln3_b):
    mix, new_state = _mixer(x, sb_k_past, sb_v_past, fx_k_past, fx_v_past, fx_logf_past, w_in, b_f, w_gn, w_out)
    x = _layer_norm(DN_ALPHA * x + mix, ln1_g, ln1_b)
    x = _layer_norm(DN_ALPHA * x + _mem_attn(x, mem_k, mem_v, w_mq, w_mo), ln2_g, ln2_b)
    x = _layer_norm(DN_ALPHA * x + _peer(x, w_pq, keys_a, keys_b, u, v), ln3_g, ln3_b)
    return x, new_state


def setup_inputs(seed: int = 0) -> dict:
    key = jax.random.key(seed)
    ks = jax.random.split(key, 40)
    nrm = lambda k, shape, s: jax.random.normal(k, shape, F32) * s
    L = DEPTH
    kv_shape = (L, DEC_BATCH, PAST_LEN, SB_HEADS, HEAD_DIM)
    fx_shape = (L, DEC_BATCH, PAST_LEN, FOX_HEADS, HEAD_DIM)
    mem_shape = (L, DEC_BATCH, N_MEM, MEM_HEADS, MEM_HEAD_DIM)
    return {
        'x_prompt': nrm(ks[0], (BATCH, SEQ, D_MODEL), 1.0),
        'x_sample': nrm(ks[1], (DEC_BATCH, DEC_SEQ, D_MODEL), 1.0),
        'mem_prompt': nrm(ks[2], (BATCH, N_MEM, D_MODEL), 1.0),
        'cache_sb_k': nrm(ks[3], kv_shape, 1.0),
        'cache_sb_v': nrm(ks[4], kv_shape, 1.0),
        'cache_fox_k': nrm(ks[5], fx_shape, 1.0),
        'cache_fox_v': nrm(ks[6], fx_shape, 1.0),
        'cache_fox_logf': jax.nn.log_sigmoid(3.0 + nrm(ks[7], (L, DEC_BATCH, PAST_LEN, FOX_HEADS), 1.0)),
        'cache_mem_k': nrm(ks[8], mem_shape, 1.0),
        'cache_mem_v': nrm(ks[9], mem_shape, 1.0),
        'w_in': nrm(ks[10], (L, D_MODEL, IN_COLS), D_MODEL ** -0.5),
        'b_f': 3.0 + nrm(ks[11], (L, FOX_HEADS), 0.5),
        'w_gn': 1.0 + nrm(ks[12], (L, MIX_WIDTH), 0.02),
        'w_out': nrm(ks[13], (L, MIX_WIDTH, D_MODEL), DN_BETA * MIX_WIDTH ** -0.5),
        'ln1_g': 1.0 + nrm(ks[14], (L, D_MODEL), 0.02),
        'ln1_b': nrm(ks[15], (L, D_MODEL), 0.02),
        'w_mq': nrm(ks[16], (L, D_MODEL, D_MODEL), D_MODEL ** -0.5),
        'w_mk': nrm(ks[17], (L, D_MODEL, D_MODEL), D_MODEL ** -0.5),
        'w_mv': nrm(ks[18], (L, D_MODEL, D_MODEL), D_MODEL ** -0.5),
        'w_mo': nrm(ks[19], (L, D_MODEL, D_MODEL), DN_BETA * D_MODEL ** -0.5),
        'ln2_g': 1.0 + nrm(ks[20], (L, D_MODEL), 0.02),
        'ln2_b': nrm(ks[21], (L, D_MODEL), 0.02),
        'w_pq': nrm(ks[22], (L, D_MODEL, PEER_HEADS * PEER_QDIM), D_MODEL ** -0.5),
        'peer_keys_a': nrm(ks[23], (L, PEER_HEADS, PEER_NKEYS, PEER_HALF), PEER_HALF ** -0.5),
        'peer_keys_b': nrm(ks[24], (L, PEER_HEADS, PEER_NKEYS, PEER_HALF), PEER_HALF ** -0.5),
        'peer_u': nrm(ks[25], (L, PEER_EXPERTS, D_MODEL), D_MODEL ** -0.5),
        'peer_v': nrm(ks[26], (L, PEER_EXPERTS, D_MODEL), DN_BETA * PEER_HEADS ** -0.5),
        'ln3_g': 1.0 + nrm(ks[27], (L, D_MODEL), 0.02),
        'ln3_b': nrm(ks[28], (L, D_MODEL), 0.02),
    }


def reference(x_prompt, x_sample, mem_prompt, cache_sb_k, cache_sb_v, cache_fox_k, cache_fox_v,
              cache_fox_logf, cache_mem_k, cache_mem_v, w_in, b_f, w_gn, w_out, ln1_g, ln1_b,
              w_mq, w_mk, w_mv, w_mo, ln2_g, ln2_b, w_pq, peer_keys_a, peer_keys_b, peer_u, peer_v,
              ln3_g, ln3_b):
    hp = x_prompt
    hs = x_sample
    Bp = x_prompt.shape[0]
    p_sbk, p_sbv, p_fxk, p_fxv, p_lf, p_mk, p_mv = [], [], [], [], [], [], []
    s_sbk, s_sbv, s_fxk, s_fxv, s_lf = [], [], [], [], []
    for l in range(DEPTH):
        weights = (w_in[l], b_f[l], w_gn[l], w_out[l], ln1_g[l], ln1_b[l], w_mq[l], w_mo[l], ln2_g[l], ln2_b[l],
                   w_pq[l], peer_keys_a[l], peer_keys_b[l], peer_u[l], peer_v[l], ln3_g[l], ln3_b[l])
        mk_p, mv_p = _mem_kv(mem_prompt, w_mk[l], w_mv[l])
        e_sb = jnp.zeros((Bp, 0, SB_HEADS, HEAD_DIM), hp.dtype)
        e_fx = jnp.zeros((Bp, 0, FOX_HEADS, HEAD_DIM), hp.dtype)
        e_lf = jnp.zeros((Bp, 0, FOX_HEADS), F32)
        hp, st_p = _layer(hp, e_sb, e_sb, e_fx, e_fx, e_lf, mk_p, mv_p, *weights)
        hs, st_s = _layer(hs, cache_sb_k[l], cache_sb_v[l], cache_fox_k[l], cache_fox_v[l], cache_fox_logf[l],
                          cache_mem_k[l], cache_mem_v[l], *weights)
        p_sbk.append(st_p[0]); p_sbv.append(st_p[1]); p_fxk.append(st_p[2]); p_fxv.append(st_p[3]); p_lf.append(st_p[4])
        p_mk.append(mk_p); p_mv.append(mv_p)
        s_sbk.append(st_s[0]); s_sbv.append(st_s[1]); s_fxk.append(st_s[2]); s_fxv.append(st_s[3]); s_lf.append(st_s[4])
    y_prompt = hp
    y_sample = hs
    new_sb_k_p = jnp.stack(p_sbk)
    new_sb_v_p = jnp.stack(p_sbv)
    new_fox_k_p = jnp.stack(p_fxk)
    new_fox_v_p = jnp.stack(p_fxv)
    new_fox_logf_p = jnp.stack(p_lf)
    new_mem_k_p = jnp.stack(p_mk)
    new_mem_v_p = jnp.stack(p_mv)
    new_sb_k_s = jnp.stack(s_sbk)
    new_sb_v_s = jnp.stack(s_sbv)
    new_fox_k_s = jnp.stack(s_fxk)
    new_fox_v_s = jnp.stack(s_fxv)
    new_fox_logf_s = jnp.stack(s_lf)
    return (y_prompt, y_sample, new_sb_k_p, new_sb_v_p, new_fox_k_p, new_fox_v_p, new_fox_logf_p,
            new_mem_k_p, new_mem_v_p, new_sb_k_s, new_sb_v_s, new_fox_k_s, new_fox_v_s, new_fox_logf_s)
```

```python
import functools

import jax
import jax.numpy as jnp
from jax import lax
from jax.experimental import pallas as pl
from jax.experimental.pallas import tpu as pltpu

F32 = jnp.float32
BF16 = jnp.bfloat16
I32 = jnp.int32

D_MODEL = 1024
HEAD_DIM = 64
N_HEADS = 8
GROUP_WIDTH = N_HEADS * HEAD_DIM
MEM_HEADS = 4
MEM_HEAD_DIM = D_MODEL // MEM_HEADS
PEER_HEADS = 8
PEER_NKEYS = 128
PEER_TOPK = 16
PEER_HALF = 128
PEER_PICKS = PEER_HEADS * PEER_TOPK
PEER_EXPERTS = PEER_NKEYS * PEER_NKEYS
DN_ALPHA = 2.0 ** 0.25
LN_EPS = 1e-5
GN_EPS = 1e-6

LANES = 128
SUBLANES = 8
KEY_BLOCK = 128
VMEM_LIMIT = 56 * 1024 * 1024


def _params(*sem):
    return pltpu.CompilerParams(dimension_semantics=sem, vmem_limit_bytes=VMEM_LIMIT)


def _log_sigmoid(x):
    return jnp.minimum(x, 0.0) - jnp.log1p(jnp.exp(-jnp.abs(x)))


def _layer_norm_rows(r, g, b):
    mu = jnp.mean(r, axis=-1, keepdims=True)
    d = r - mu
    var = jnp.mean(d * d, axis=-1, keepdims=True)
    return d * lax.rsqrt(var + LN_EPS) * g + b


def _inproj_kernel(x_ref, w_ref, wf_ref, bf_ref,
                   qsb_ref, ksb_ref, vsb_ref, qfx_ref, kfx_ref, vfx_ref,
                   ksbb_ref, vsbb_ref, kfxb_ref, vfxb_ref, lf_ref):
    xb = x_ref[...].astype(BF16)

    def proj(j):
        return jnp.dot(xb, w_ref[:, j * GROUP_WIDTH:(j + 1) * GROUP_WIDTH], preferred_element_type=F32)

    scale = HEAD_DIM ** -0.5
    qsb_ref[...] = (proj(0) * scale).astype(BF16)
    k = proj(1)
    ksb_ref[...] = k
    ksbb_ref[...] = k.astype(BF16)
    v = proj(2)
    vsb_ref[...] = v
    vsbb_ref[...] = v.astype(BF16)
    qfx_ref[...] = (proj(3) * scale).astype(BF16)
    k = proj(4)
    kfx_ref[...] = k
    kfxb_ref[...] = k.astype(BF16)
    v = proj(5)
    vfx_ref[...] = v
    vfxb_ref[...] = v.astype(BF16)
    f = jnp.dot(xb, wf_ref[...], preferred_element_type=F32) + bf_ref[...]
    lf_ref[...] = _log_sigmoid(f)[:, :N_HEADS]


def _in_projection(x2d, w_main, w_f, b_f, tm):
    m = x2d.shape[0]
    f32o = jax.ShapeDtypeStruct((m, GROUP_WIDTH), F32)
    bf16o = jax.ShapeDtypeStruct((m, GROUP_WIDTH), BF16)
    blk = pl.BlockSpec((tm, GROUP_WIDTH), lambda i: (i, 0))
    return pl.pallas_call(
        _inproj_kernel,
        grid=(m // tm,),
        in_specs=[
            pl.BlockSpec((tm, D_MODEL), lambda i: (i, 0)),
            pl.BlockSpec((D_MODEL, 6 * GROUP_WIDTH), lambda i: (0, 0)),
            pl.BlockSpec((D_MODEL, LANES), lambda i: (0, 0)),
            pl.BlockSpec((1, LANES), lambda i: (0, 0)),
        ],
        out_specs=[blk] * 10 + [pl.BlockSpec((tm, N_HEADS), lambda i: (i, 0))],
        out_shape=[bf16o, f32o, f32o, bf16o, f32o, f32o, bf16o, bf16o, bf16o, bf16o,
                   jax.ShapeDtypeStruct((m, N_HEADS), F32)],
        compiler_params=_params("parallel"),
        name="in_projection",
    )(x2d, w_main, w_f, b_f)


def _cumsum_kernel(lf_ref, tri_ref, c_ref):
    n_chunks = lf_ref.shape[1] // KEY_BLOCK

    def chunk(i, carry):
        r0 = pl.multiple_of(i * KEY_BLOCK, KEY_BLOCK)
        v = lf_ref[0, pl.ds(r0, KEY_BLOCK), :]
        hi = v.astype(BF16)
        r1 = v - hi.astype(F32)
        mid = r1.astype(BF16)
        lo = (r1 - mid.astype(F32)).astype(BF16)
        parts = jnp.concatenate([hi, mid, lo], axis=1)
        s = jnp.dot(tri_ref[...], parts, preferred_element_type=F32)
        c = s[:, :N_HEADS] + s[:, N_HEADS:2 * N_HEADS] + s[:, 2 * N_HEADS:] + carry
        c_ref[0, pl.ds(r0, KEY_BLOCK), :] = c
        return c[KEY_BLOCK - 1:, :]

    lax.fori_loop(0, n_chunks, chunk, jnp.zeros((1, N_HEADS), F32))


def _forget_cumsum(lf):
    b, l, _ = lf.shape
    r = lax.broadcasted_iota(I32, (KEY_BLOCK, KEY_BLOCK), 0)
    c = lax.broadcasted_iota(I32, (KEY_BLOCK, KEY_BLOCK), 1)
    tri = (c <= r).astype(BF16)
    return pl.pallas_call(
        _cumsum_kernel,
        grid=(b,),
        in_specs=[pl.BlockSpec((1, l, N_HEADS), lambda i: (i, 0, 0)),
                  pl.BlockSpec((KEY_BLOCK, KEY_BLOCK), lambda i: (0, 0))],
        out_specs=pl.BlockSpec((1, l, N_HEADS), lambda i: (i, 0, 0)),
        out_shape=jax.ShapeDtypeStruct((b, l, N_HEADS), F32),
        compiler_params=_params("parallel"),
        name="forget_cumsum",
    )(lf, tri)


def _head_masks(width):
    lane = lax.broadcasted_iota(I32, (1, width), 1)
    return lane < HEAD_DIM


def _sb_kernel(q_ref, k_ref, v_ref, tri_ref, o_ref, *, tq, past):
    i = pl.program_id(2)
    q = q_ref[0]
    first = _head_masks(LANES)
    zero_q = jnp.zeros_like(q)
    qh = (jnp.where(first, q, zero_q), jnp.where(first, zero_q, q))
    q_pos0 = past + i * tq
    diag = q_pos0 // KEY_BLOCK
    tri = tri_ref[...]

    def block(j, carry, masked):
        run, acc = carry
        k0 = pl.multiple_of(j * KEY_BLOCK, KEY_BLOCK)
        kb = k_ref[0, pl.ds(k0, KEY_BLOCK), :]
        vb = v_ref[0, pl.ds(k0, KEY_BLOCK), :]
        if masked:
            kpos = k0 + lax.broadcasted_iota(I32, (tq, KEY_BLOCK), 1)
            qpos = q_pos0 + lax.broadcasted_iota(I32, (tq, KEY_BLOCK), 0)
            mask = kpos < qpos
        ws = []
        new_run = []
        for h in range(2):
            z = lax.dot_general(qh[h], kb, (((1,), (1,)), ((), ())), preferred_element_type=F32)
            sp = jnp.log1p(jnp.exp(-jnp.abs(z)))
            log_keep = -(jnp.maximum(z, 0.0) + sp)
            log_beta = jnp.minimum(z, 0.0) - sp
            if masked:
                log_keep = jnp.where(mask, log_keep, 0.0)
            hi = log_keep.astype(BF16)
            lo = (log_keep - hi.astype(F32)).astype(BF16)
            c = jnp.dot(jnp.concatenate([hi, lo], axis=1), tri, preferred_element_type=F32)
            w = jnp.exp(log_beta + c[:, :KEY_BLOCK] + run[h])
            if masked:
                w = jnp.where(mask, w, 0.0)
            new_run.append(run[h] + c[:, KEY_BLOCK:])
            ws.append(w.astype(BF16))
        zero_v = jnp.zeros_like(vb)
        v2 = jnp.concatenate([jnp.where(first, vb, zero_v), jnp.where(first, zero_v, vb)], axis=0)
        acc = acc + jnp.dot(jnp.concatenate(ws, axis=1), v2, preferred_element_type=F32)
        return (tuple(new_run), acc)

    zeros = jnp.zeros((tq, KEY_BLOCK), F32)
    carry = block(diag, ((zeros, zeros), jnp.zeros((tq, LANES), F32)), True)

    def body(it, carry):
        return block(diag - 1 - it, carry, False)

    _, acc = lax.fori_loop(0, diag, body, carry)
    o_ref[0] = acc


def _cumsum_rhs():
    r = lax.broadcasted_iota(I32, (2 * KEY_BLOCK, 2 * KEY_BLOCK), 0) % KEY_BLOCK
    c = lax.broadcasted_iota(I32, (2 * KEY_BLOCK, 2 * KEY_BLOCK), 1)
    return ((c >= KEY_BLOCK) | (r > c)).astype(BF16)


def _stick_breaking_attention(q, k, v, tq, past):
    b, t, _ = q.shape
    lk = k.shape[1]
    pairs = GROUP_WIDTH // LANES
    return pl.pallas_call(
        functools.partial(_sb_kernel, tq=tq, past=past),
        grid=(b, pairs, t // tq),
        in_specs=[
            pl.BlockSpec((1, tq, LANES), lambda bi, hp, i: (bi, i, hp)),
            pl.BlockSpec((1, lk, LANES), lambda bi, hp, i: (bi, 0, hp)),
            pl.BlockSpec((1, lk, LANES), lambda bi, hp, i: (bi, 0, hp)),
            pl.BlockSpec((2 * KEY_BLOCK, 2 * KEY_BLOCK), lambda bi, hp, i: (0, 0)),
        ],
        out_specs=pl.BlockSpec((1, tq, LANES), lambda bi, hp, i: (bi, i, hp)),
        out_shape=jax.ShapeDtypeStruct((b, t, GROUP_WIDTH), F32),
        compiler_params=_params("parallel", "parallel", "arbitrary"),
        name="stick_breaking_attention",
    )(q, k, v, _cumsum_rhs())


def _fox_kernel(q_ref, k_ref, v_ref, cq_ref, ck_ref, o_ref, *, tq, past):
    i = pl.program_id(2)
    hp = pl.program_id(1)
    q = q_ref[0]
    first = _head_masks(LANES)
    zero_q = jnp.zeros_like(q)
    qh = (jnp.where(first, q, zero_q), jnp.where(first, zero_q, q))
    q_pos0 = past + i * tq
    diag = q_pos0 // KEY_BLOCK
    cq_all = cq_ref[0]
    head_lane = lax.broadcasted_iota(I32, (1, N_HEADS), 1)
    cq = [jnp.sum(jnp.where(head_lane == 2 * hp + h, cq_all, 0.0), axis=1, keepdims=True) for h in range(2)]

    def block(j, carry, masked):
        ms, ls, acc = carry
        k0 = pl.multiple_of(j * KEY_BLOCK, KEY_BLOCK)
        kb = k_ref[0, pl.ds(k0, KEY_BLOCK), :]
        vb = v_ref[0, pl.ds(k0, KEY_BLOCK), :]
        ck = ck_ref[0, 0, j]
        if masked:
            kpos = k0 + lax.broadcasted_iota(I32, (tq, KEY_BLOCK), 1)
            qpos = q_pos0 + lax.broadcasted_iota(I32, (tq, KEY_BLOCK), 0)
            mask = kpos <= qpos
        ps, new_m, new_l, scales = [], [], [], []
        for h in range(2):
            s = lax.dot_general(qh[h], kb, (((1,), (1,)), ((), ())), preferred_element_type=F32)
            s = s + cq[h] - ck[h:h + 1, :]
            if masked:
                s = jnp.where(mask, s, -jnp.inf)
            m = jnp.maximum(ms[h], jnp.max(s, axis=1, keepdims=True))
            p = jnp.exp(s - m)
            a = jnp.exp(ms[h] - m)
            new_m.append(m)
            new_l.append(a * ls[h] + jnp.sum(p, axis=1, keepdims=True))
            scales.append(a)
            ps.append(p.astype(BF16))
        zero_v = jnp.zeros_like(vb)
        v2 = jnp.concatenate([jnp.where(first, vb, zero_v), jnp.where(first, zero_v, vb)], axis=0)
        pv = jnp.dot(jnp.concatenate(ps, axis=1), v2, preferred_element_type=F32)
        acc = acc * jnp.where(first, scales[0], scales[1]) + pv
        return (tuple(new_m), tuple(new_l), acc)

    neg = jnp.full((tq, 1), -jnp.inf, F32)
    zero = jnp.zeros((tq, 1), F32)
    carry = block(diag, ((neg, neg), (zero, zero), jnp.zeros((tq, LANES), F32)), True)

    def body(it, carry):
        return block(diag - 1 - it, carry, False)

    _, ls, acc = lax.fori_loop(0, diag, body, carry)
    o_ref[0] = acc / jnp.where(first, ls[0], ls[1])


def _forgetting_attention(q, k, v, cq, ck, tq, past):
    b, t, _ = q.shape
    lk = k.shape[1]
    pairs = GROUP_WIDTH // LANES
    return pl.pallas_call(
        functools.partial(_fox_kernel, tq=tq, past=past),
        grid=(b, pairs, t // tq),
        in_specs=[
            pl.BlockSpec((1, tq, LANES), lambda bi, hp, i: (bi, i, hp)),
            pl.BlockSpec((1, lk, LANES), lambda bi, hp, i: (bi, 0, hp)),
            pl.BlockSpec((1, lk, LANES), lambda bi, hp, i: (bi, 0, hp)),
            pl.BlockSpec((1, tq, N_HEADS), lambda bi, hp, i: (bi, i, 0)),
            pl.BlockSpec((1, 1, lk // KEY_BLOCK, 2, KEY_BLOCK), lambda bi, hp, i: (bi, hp, 0, 0, 0)),
        ],
        out_specs=pl.BlockSpec((1, tq, LANES), lambda bi, hp, i: (bi, i, hp)),
        out_shape=jax.ShapeDtypeStruct((b, t, GROUP_WIDTH), F32),
        compiler_params=_params("parallel", "parallel", "arbitrary"),
        name="forgetting_attention",
    )(q, k, v, cq, ck)


def _mixout_kernel(osb_ref, ofx_ref, x_ref, gn_ref, w_ref, g_ref, b_ref, y_ref):
    def rms(o, g):
        return o * lax.rsqrt(jnp.mean(o * o, axis=-1, keepdims=True) + GN_EPS) * g

    gn = gn_ref[...]
    o = jnp.concatenate([rms(osb_ref[...], gn[:, :GROUP_WIDTH]), rms(ofx_ref[...], gn[:, GROUP_WIDTH:])], axis=1)
    mix = jnp.dot(o.astype(BF16), w_ref[...], preferred_element_type=F32)
    y_ref[...] = _layer_norm_rows(DN_ALPHA * x_ref[...] + mix, g_ref[...], b_ref[...])


def _mix_out(osb, ofx, x2d, w_gn, w_out, g, b, tm):
    m = x2d.shape[0]
    row = pl.BlockSpec((1, D_MODEL), lambda i: (0, 0))
    return pl.pallas_call(
        _mixout_kernel,
        grid=(m // tm,),
        in_specs=[
            pl.BlockSpec((tm, GROUP_WIDTH), lambda i: (i, 0)),
            pl.BlockSpec((tm, GROUP_WIDTH), lambda i: (i, 0)),
            pl.BlockSpec((tm, D_MODEL), lambda i: (i, 0)),
            row,
            pl.BlockSpec((D_MODEL, D_MODEL), lambda i: (0, 0)),
            row, row,
        ],
        out_specs=pl.BlockSpec((tm, D_MODEL), lambda i: (i, 0)),
        out_shape=jax.ShapeDtypeStruct((m, D_MODEL), F32),
        compiler_params=_params("parallel"),
        name="mix_out_ln1",
    )(osb, ofx, x2d, w_gn, w_out, g, b)


def _memkv_kernel(m_ref, wk_ref, wv_ref, k_ref, v_ref):
    mb = m_ref[...].astype(BF16)
    k_ref[...] = jnp.dot(mb, wk_ref[...], preferred_element_type=F32)
    v_ref[...] = jnp.dot(mb, wv_ref[...], preferred_element_type=F32)


def _mem_kv(mem2d, w_mk, w_mv, tm):
    m = mem2d.shape[0]
    wspec = pl.BlockSpec((D_MODEL, D_MODEL), lambda i: (0, 0))
    blk = pl.BlockSpec((tm, D_MODEL), lambda i: (i, 0))
    out = jax.ShapeDtypeStruct((m, D_MODEL), F32)
    return pl.pallas_call(
        _memkv_kernel,
        grid=(m // tm,),
        in_specs=[blk, wspec, wspec],
        out_specs=[blk, blk],
        out_shape=[out, out],
        compiler_params=_params("parallel"),
        name="mem_kv",
    )(mem2d, w_mk, w_mv)


def _memattn_kernel(x_ref, mk_ref, mv_ref, wq_ref, wo_ref, g_ref, b_ref, y_ref):
    x = x_ref[0]
    q = jnp.dot(x.astype(BF16), wq_ref[...], preferred_element_type=F32)
    qb = (q * (MEM_HEAD_DIM ** -0.5)).astype(BF16)
    mk = mk_ref[0].astype(BF16)
    mv = mv_ref[0].astype(BF16)
    outs = []
    for h in range(MEM_HEADS):
        sl = slice(h * MEM_HEAD_DIM, (h + 1) * MEM_HEAD_DIM)
        s = lax.dot_general(qb[:, sl], mk[:, sl], (((1,), (1,)), ((), ())), preferred_element_type=F32)
        p = jnp.exp(s - jnp.max(s, axis=1, keepdims=True))
        o = jnp.dot(p.astype(BF16), mv[:, sl], preferred_element_type=F32)
        outs.append(o / jnp.sum(p, axis=1, keepdims=True))
    o = jnp.concatenate(outs, axis=1).astype(BF16)
    att = jnp.dot(o, wo_ref[...], preferred_element_type=F32)
    y_ref[0] = _layer_norm_rows(DN_ALPHA * x + att, g_ref[...], b_ref[...])


def _mem_attention(x3d, mk, mv, w_mq, w_mo, g, b, tm):
    bsz, t, _ = x3d.shape
    n_mem = mk.shape[1]
    row = pl.BlockSpec((1, D_MODEL), lambda bi, i: (0, 0))
    wspec = pl.BlockSpec((D_MODEL, D_MODEL), lambda bi, i: (0, 0))
    return pl.pallas_call(
        _memattn_kernel,
        grid=(bsz, t // tm),
        in_specs=[
            pl.BlockSpec((1, tm, D_MODEL), lambda bi, i: (bi, i, 0)),
            pl.BlockSpec((1, n_mem, D_MODEL), lambda bi, i: (bi, 0, 0)),
            pl.BlockSpec((1, n_mem, D_MODEL), lambda bi, i: (bi, 0, 0)),
            wspec, wspec, row, row,
        ],
        out_specs=pl.BlockSpec((1, tm, D_MODEL), lambda bi, i: (bi, i, 0)),
        out_shape=jax.ShapeDtypeStruct((bsz, t, D_MODEL), F32),
        compiler_params=_params("parallel", "parallel"),
        name="mem_attention_ln2",
    )(x3d, mk, mv, w_mq, w_mo, g, b)


def _topk_rows(s, key, k, big):
    vals, keys = [], []
    for r in range(k):
        m = jnp.max(s, axis=0, keepdims=True)
        km = jnp.min(jnp.where(s == m, key, big), axis=0, keepdims=True)
        vals.append(m)
        keys.append(km)
        if r + 1 < k:
            s = jnp.where(key == km, -jnp.inf, s)
    return jnp.concatenate(vals, axis=0), jnp.concatenate(keys, axis=0)


def _route_kernel(x_ref, w_ref, ka_ref, kb_ref, idx_ref, g_ref, q_scr):
    tm = x_ref.shape[0]
    xb = x_ref[...].astype(BF16)
    for c in range(2 * PEER_HEADS):
        q_scr[c] = jnp.dot(xb, w_ref[:, c * PEER_HALF:(c + 1) * PEER_HALF],
                           preferred_element_type=F32).astype(BF16)
    key_io = lax.broadcasted_iota(I32, (PEER_NKEYS, tm), 0)
    n_cand = PEER_TOPK * PEER_TOPK
    pos_io = lax.broadcasted_iota(I32, (n_cand, tm), 0)
    nt = (((1,), (1,)), ((), ()))

    def head(h, carry):
        sa = lax.dot_general(ka_ref[h], q_scr[2 * h], nt, preferred_element_type=F32)
        sb = lax.dot_general(kb_ref[h], q_scr[2 * h + 1], nt, preferred_element_type=F32)
        va, ia = _topk_rows(sa, key_io, PEER_TOPK, PEER_NKEYS)
        vb, ib = _topk_rows(sb, key_io, PEER_TOPK, PEER_NKEYS)
        cand = jnp.concatenate([va[r:r + 1] + vb for r in range(PEER_TOPK)], axis=0)
        cidx = jnp.concatenate([ia[r:r + 1] * PEER_NKEYS + ib for r in range(PEER_TOPK)], axis=0)
        ckey = pos_io * PEER_EXPERTS + cidx
        top, tkey = _topk_rows(cand, ckey, PEER_TOPK, n_cand * PEER_EXPERTS)
        e = jnp.exp(top - top[0:1])
        r0 = pl.multiple_of(h * PEER_TOPK, PEER_TOPK)
        idx_ref[0, pl.ds(r0, PEER_TOPK), :] = tkey & (PEER_EXPERTS - 1)
        g_ref[0, pl.ds(r0, PEER_TOPK), :] = e / jnp.sum(e, axis=0, keepdims=True)
        return carry

    lax.fori_loop(0, PEER_HEADS, head, 0)


def _peer_route(x2d, w_pq, keys_a, keys_b, tm):
    m = x2d.shape[0]
    nt = m // tm
    kspec = pl.BlockSpec((PEER_HEADS, PEER_NKEYS, PEER_HALF), lambda i: (0, 0, 0))
    ospec = pl.BlockSpec((1, PEER_PICKS, tm), lambda i: (i, 0, 0))
    return pl.pallas_call(
        _route_kernel,
        grid=(nt,),
        in_specs=[pl.BlockSpec((tm, D_MODEL), lambda i: (i, 0)),
                  pl.BlockSpec((D_MODEL, 2 * PEER_HEADS * PEER_HALF), lambda i: (0, 0)),
                  kspec, kspec],
        out_specs=[ospec, ospec],
        out_shape=[jax.ShapeDtypeStruct((nt, PEER_PICKS, tm), I32),
                   jax.ShapeDtypeStruct((nt, PEER_PICKS, tm), F32)],
        scratch_shapes=[pltpu.VMEM((2 * PEER_HEADS, tm, PEER_HALF), BF16)],
        compiler_params=_params("parallel"),
        name="peer_route",
    )(x2d, w_pq, keys_a, keys_b)


HALF_EXPERTS = PEER_EXPERTS // 2
HALF_SHIFT = HALF_EXPERTS.bit_length() - 1
HIGH_MASK = -65536


def _pack_table(t):
    bits = lax.bitcast_convert_type(t.astype(BF16), jnp.uint16).astype(jnp.uint32)
    word = (bits[:HALF_EXPERTS] << 16) | bits[HALF_EXPERTS:]
    return lax.bitcast_convert_type(word, I32).reshape(HALF_EXPERTS, SUBLANES, LANES)


def _table_spec():
    return pl.BlockSpec((HALF_EXPERTS, SUBLANES, LANES), lambda i: (0, 0, 0), pipeline_mode=pl.Buffered(1))


def _table_row(tab_ref, e):
    row = tab_ref[e & (HALF_EXPERTS - 1)]
    shift = (e >> HALF_SHIFT) * 16
    return pltpu.bitcast((row << shift) & HIGH_MASK, F32)


def _peer_in_kernel(idx_ref, x_ref, g_ref, tab_ref, w_ref):
    tm = x_ref.shape[0]
    sub_io = lax.broadcasted_iota(I32, (SUBLANES, LANES), 0)
    tok_io = lax.broadcasted_iota(I32, (PEER_PICKS, tm), 1)

    def token(t, h_t):
        x = x_ref[t]
        groups = []
        for g8 in range(PEER_PICKS // SUBLANES):
            acc = jnp.zeros((SUBLANES, LANES), F32)
            for s in range(SUBLANES):
                e = idx_ref[0, g8 * SUBLANES + s, t]
                pr = x * _table_row(tab_ref, e)
                pr = pr + pltpu.roll(pr, 4, 0)
                pr = pr + pltpu.roll(pr, 2, 0)
                pr = pr + pltpu.roll(pr, 1, 0)
                acc = jnp.where(sub_io == s, pr, acc)
            groups.append(acc)
        col = jnp.sum(jnp.concatenate(groups, axis=0), axis=1, keepdims=True)
        return jnp.where(tok_io == t, col, h_t)

    h_t = lax.fori_loop(0, tm, token, jnp.zeros((PEER_PICKS, tm), F32))
    gelu = 0.5 * h_t * (1.0 + lax.erf(h_t * (2.0 ** -0.5)))
    w_ref[0] = gelu * g_ref[0]


def _peer_in(idx, x3, gate, table, tm):
    nt = idx.shape[0]
    tspec = pl.BlockSpec((1, PEER_PICKS, tm), lambda i: (i, 0, 0))
    return pl.pallas_call(
        _peer_in_kernel,
        grid=(nt,),
        in_specs=[pl.BlockSpec((1, PEER_PICKS, tm), lambda i: (i, 0, 0), memory_space=pltpu.SMEM),
                  pl.BlockSpec((tm, SUBLANES, LANES), lambda i: (i, 0, 0)),
                  tspec,
                  _table_spec()],
        out_specs=tspec,
        out_shape=jax.ShapeDtypeStruct((nt, PEER_PICKS, tm), F32),
        compiler_params=_params("arbitrary"),
        name="peer_expert_in",
    )(idx, x3, gate, table)


def _peer_out_kernel(idx_ref, w_ref, x_ref, g_ref, b_ref, tab_ref, y_ref):
    tm = x_ref.shape[0]

    def token(t, carry):
        acc = DN_ALPHA * x_ref[t]
        for p in range(PEER_PICKS):
            acc = acc + w_ref[0, p, t] * _table_row(tab_ref, idx_ref[0, p, t])
        y_ref[t] = acc
        return carry

    lax.fori_loop(0, tm, token, 0)
    r = y_ref[...]
    n = float(D_MODEL)
    mu = jnp.sum(jnp.sum(r, axis=2, keepdims=True), axis=1, keepdims=True) / n
    d = r - mu
    var = jnp.sum(jnp.sum(d * d, axis=2, keepdims=True), axis=1, keepdims=True) / n
    y_ref[...] = d * lax.rsqrt(var + LN_EPS) * g_ref[...] + b_ref[...]


def _peer_out(idx, w, x3, g3, b3, table, tm):
    nt = idx.shape[0]
    m = x3.shape[0]
    sspec = pl.BlockSpec((1, PEER_PICKS, tm), lambda i: (i, 0, 0), memory_space=pltpu.SMEM)
    vec = pl.BlockSpec((1, SUBLANES, LANES), lambda i: (0, 0, 0))
    xspec = pl.BlockSpec((tm, SUBLANES, LANES), lambda i: (i, 0, 0))
    return pl.pallas_call(
        _peer_out_kernel,
        grid=(nt,),
        in_specs=[sspec, sspec, xspec, vec, vec, _table_spec()],
        out_specs=xspec,
        out_shape=jax.ShapeDtypeStruct((m, SUBLANES, LANES), F32),
        compiler_params=_params("arbitrary"),
        name="peer_expert_out",
    )(idx, w, x3, g3, b3, table)


def _pick_tile(n, pref):
    t = pref
    while n % t:
        t //= 2
    return t


def _layer(x, past, mem_k, mem_v, wts):
    b, t, _ = x.shape
    m = b * t
    x2d = x.reshape(m, D_MODEL)
    tm = _pick_tile(m, 256)

    (qsb, ksb, vsb, qfx, kfx, vfx, ksbb, vsbb, kfxb, vfxb, lf) = _in_projection(
        x2d, wts["w_in_main"], wts["w_in_f"], wts["b_f"], tm)
    state = tuple(a.reshape(b, t, N_HEADS, HEAD_DIM) for a in (ksb, vsb, kfx, vfx)) + (lf.reshape(b, t, N_HEADS),)

    r3 = lambda a: a.reshape(b, t, GROUP_WIDTH)
    if past is None:
        p = 0
        k_sb, v_sb, k_fx, v_fx = r3(ksbb), r3(vsbb), r3(kfxb), r3(vfxb)
        lf_all = lf.reshape(b, t, N_HEADS)
    else:
        p = past[0].shape[1]
        pad = (-(p + t)) % KEY_BLOCK

        def cat(c, new):
            parts = [c.reshape(b, p, GROUP_WIDTH).astype(BF16), r3(new)]
            if pad:
                parts.append(jnp.zeros((b, pad, GROUP_WIDTH), BF16))
            return jnp.concatenate(parts, axis=1)

        k_sb, v_sb, k_fx, v_fx = cat(past[0], ksbb), cat(past[1], vsbb), cat(past[2], kfxb), cat(past[3], vfxb)
        parts = [past[4].astype(F32), lf.reshape(b, t, N_HEADS)]
        if pad:
            parts.append(jnp.zeros((b, pad, N_HEADS), F32))
        lf_all = jnp.concatenate(parts, axis=1)

    c_all = _forget_cumsum(lf_all)
    lk = c_all.shape[1]
    cq = c_all[:, p:p + t]
    ck = c_all.transpose(0, 2, 1).reshape(b, N_HEADS // 2, 2, lk // KEY_BLOCK, KEY_BLOCK).transpose(0, 1, 3, 2, 4)

    tq = min(KEY_BLOCK, t)
    o_sb = _stick_breaking_attention(r3(qsb), k_sb, v_sb, tq, p)
    o_fx = _forgetting_attention(r3(qfx), k_fx, v_fx, cq, ck, tq, p)

    x1 = _mix_out(o_sb.reshape(m, GROUP_WIDTH), o_fx.reshape(m, GROUP_WIDTH), x2d,
                  wts["w_gn"], wts["w_out"], wts["ln1_g"], wts["ln1_b"], tm)
    x2 = _mem_attention(x1.reshape(b, t, D_MODEL), mem_k, mem_v, wts["w_mq"], wts["w_mo"],
                        wts["ln2_g"], wts["ln2_b"], _pick_tile(t, 256))
    x2d2 = x2.reshape(m, D_MODEL)

    tr = LANES
    idx, gate = _peer_route(x2d2, wts["w_pq"], wts["keys_a"], wts["keys_b"], tr)
    x3 = x2d2.reshape(m, SUBLANES, LANES)
    w = _peer_in(idx, x3, gate, wts["table_u"], tr)
    y = _peer_out(idx, w, x3, wts["ln3_g3"], wts["ln3_b3"], wts["table_v"], tr)
    return y.reshape(b, t, D_MODEL), state


def kernel(x_prompt, x_sample, mem_prompt, cache_sb_k, cache_sb_v, cache_fox_k, cache_fox_v, cache_fox_logf,
           cache_mem_k, cache_mem_v, w_in, b_f, w_gn, w_out, ln1_g, ln1_b, w_mq, w_mk, w_mv, w_mo, ln2_g, ln2_b,
           w_pq, peer_keys_a, peer_keys_b, peer_u, peer_v, ln3_g, ln3_b):
    depth = w_in.shape[0]
    hp, hs = x_prompt, x_sample
    bp = x_prompt.shape[0]
    n_mem = mem_prompt.shape[1]
    mix_cols = 6 * GROUP_WIDTH
    outs_p = [[] for _ in range(7)]
    outs_s = [[] for _ in range(5)]
    row = lambda a: a.reshape(1, D_MODEL)
    for l in range(depth):
        wts = {
            "w_in_main": w_in[l][:, :mix_cols].astype(BF16),
            "w_in_f": jnp.pad(w_in[l][:, mix_cols:], ((0, 0), (0, LANES - N_HEADS))).astype(BF16),
            "b_f": jnp.pad(b_f[l], (0, LANES - N_HEADS)).reshape(1, LANES),
            "w_gn": row(w_gn[l]), "w_out": w_out[l].astype(BF16),
            "ln1_g": row(ln1_g[l]), "ln1_b": row(ln1_b[l]),
            "w_mq": w_mq[l].astype(BF16), "w_mo": w_mo[l].astype(BF16),
            "ln2_g": row(ln2_g[l]), "ln2_b": row(ln2_b[l]),
            "w_pq": w_pq[l].astype(BF16),
            "keys_a": peer_keys_a[l].astype(BF16), "keys_b": peer_keys_b[l].astype(BF16),
            "table_u": _pack_table(peer_u[l]), "table_v": _pack_table(peer_v[l]),
            "ln3_g3": ln3_g[l].reshape(1, SUBLANES, LANES), "ln3_b3": ln3_b[l].reshape(1, SUBLANES, LANES),
        }
        mem2d = mem_prompt.reshape(bp * n_mem, D_MODEL)
        mk_p, mv_p = _mem_kv(mem2d, w_mk[l].astype(BF16), w_mv[l].astype(BF16), _pick_tile(bp * n_mem, 512))
        mk_p = mk_p.reshape(bp, n_mem, D_MODEL)
        mv_p = mv_p.reshape(bp, n_mem, D_MODEL)
        hp, st_p = _layer(hp, None, mk_p, mv_p, wts)
        bs = x_sample.shape[0]
        past = (cache_sb_k[l], cache_sb_v[l], cache_fox_k[l], cache_fox_v[l], cache_fox_logf[l])
        hs, st_s = _layer(hs, past, cache_mem_k[l].reshape(bs, -1, D_MODEL), cache_mem_v[l].reshape(bs, -1, D_MODEL), wts)
        for i in range(5):
            outs_p[i].append(st_p[i])
            outs_s[i].append(st_s[i])
        outs_p[5].append(mk_p.reshape(bp, n_mem, MEM_HEADS, MEM_HEAD_DIM))
        outs_p[6].append(mv_p.reshape(bp, n_mem, MEM_HEADS, MEM_HEAD_DIM))
    stack = lambda xs: jnp.stack(xs)
    return (hp, hs) + tuple(stack(o) for o in outs_p) + tuple(stack(o) for o in outs_s)
```

```python
import functools

import jax
import jax.numpy as jnp
from jax import lax
from jax.experimental import pallas as pl
from jax.experimental.pallas import tpu as pltpu

F32 = jnp.float32
BF16 = jnp.bfloat16
I32 = jnp.int32

D_MODEL = 1024
HEAD_DIM = 64
N_HEADS = 8
GROUP_WIDTH = N_HEADS * HEAD_DIM
MEM_HEADS = 4
MEM_HEAD_DIM = D_MODEL // MEM_HEADS
PEER_HEADS = 8
PEER_NKEYS = 128
PEER_TOPK = 16
PEER_HALF = 128
PEER_PICKS = PEER_HEADS * PEER_TOPK
PEER_EXPERTS = PEER_NKEYS * PEER_NKEYS
DN_ALPHA = 2.0 ** 0.25
LN_EPS = 1e-5
GN_EPS = 1e-6

LANES = 128
SUBLANES = 8
KEY_BLOCK = 128
VMEM_LIMIT = 56 * 1024 * 1024


def _params(*sem):
    return pltpu.CompilerParams(dimension_semantics=sem, vmem_limit_bytes=VMEM_LIMIT)


def _log_sigmoid(x):
    return jnp.minimum(x, 0.0) - jnp.log1p(jnp.exp(-jnp.abs(x)))


def _layer_norm_rows(r, g, b):
    mu = jnp.mean(r, axis=-1, keepdims=True)
    d = r - mu
    var = jnp.mean(d * d, axis=-1, keepdims=True)
    return d * lax.rsqrt(var + LN_EPS) * g + b


def _inproj_kernel(x_ref, w_ref, wf_ref, bf_ref,
                   qsb_ref, ksb_ref, vsb_ref, qfx_ref, kfx_ref, vfx_ref,
                   ksbb_ref, vsbb_ref, kfxb_ref, vfxb_ref, lf_ref):
    xb = x_ref[...].astype(BF16)

    def proj(j):
        return jnp.dot(xb, w_ref[:, j * GROUP_WIDTH:(j + 1) * GROUP_WIDTH], preferred_element_type=F32)

    scale = HEAD_DIM ** -0.5
    qsb_ref[...] = (proj(0) * scale).astype(BF16)
    k = proj(1)
    ksb_ref[...] = k
    ksbb_ref[...] = k.astype(BF16)
    v = proj(2)
    vsb_ref[...] = v
    vsbb_ref[...] = v.astype(BF16)
    qfx_ref[...] = (proj(3) * scale).astype(BF16)
    k = proj(4)
    kfx_ref[...] = k
    kfxb_ref[...] = k.astype(BF16)
    v = proj(5)
    vfx_ref[...] = v
    vfxb_ref[...] = v.astype(BF16)
    f = jnp.dot(xb, wf_ref[...], preferred_element_type=F32) + bf_ref[...]
    lf_ref[...] = _log_sigmoid(f)[:, :N_HEADS]


def _in_projection(x2d, w_main, w_f, b_f, tm):
    m = x2d.shape[0]
    f32o = jax.ShapeDtypeStruct((m, GROUP_WIDTH), F32)
    bf16o = jax.ShapeDtypeStruct((m, GROUP_WIDTH), BF16)
    blk = pl.BlockSpec((tm, GROUP_WIDTH), lambda i: (i, 0))
    return pl.pallas_call(
        _inproj_kernel,
        grid=(m // tm,),
        in_specs=[
            pl.BlockSpec((tm, D_MODEL), lambda i: (i, 0)),
            pl.BlockSpec((D_MODEL, 6 * GROUP_WIDTH), lambda i: (0, 0)),
            pl.BlockSpec((D_MODEL, LANES), lambda i: (0, 0)),
            pl.BlockSpec((1, LANES), lambda i: (0, 0)),
        ],
        out_specs=[blk] * 10 + [pl.BlockSpec((tm, N_HEADS), lambda i: (i, 0))],
        out_shape=[bf16o, f32o, f32o, bf16o, f32o, f32o, bf16o, bf16o, bf16o, bf16o,
                   jax.ShapeDtypeStruct((m, N_HEADS), F32)],
        compiler_params=_params("parallel"),
        name="in_projection",
    )(x2d, w_main, w_f, b_f)


def _cumsum_kernel(lf_ref, tri_ref, c_ref):
    n_chunks = lf_ref.shape[1] // KEY_BLOCK

    def chunk(i, carry):
        r0 = pl.multiple_of(i * KEY_BLOCK, KEY_BLOCK)
        v = lf_ref[0, pl.ds(r0, KEY_BLOCK), :]
        hi = v.astype(BF16)
        r1 = v - hi.astype(F32)
        mid = r1.astype(BF16)
        lo = (r1 - mid.astype(F32)).astype(BF16)
        parts = jnp.concatenate([hi, mid, lo], axis=1)
        s = jnp.dot(tri_ref[...], parts, preferred_element_type=F32)
        c = s[:, :N_HEADS] + s[:, N_HEADS:2 * N_HEADS] + s[:, 2 * N_HEADS:] + carry
        c_ref[0, pl.ds(r0, KEY_BLOCK), :] = c
        return c[KEY_BLOCK - 1:, :]

    lax.fori_loop(0, n_chunks, chunk, jnp.zeros((1, N_HEADS), F32))


def _forget_cumsum(lf):
    b, l, _ = lf.shape
    r = lax.broadcasted_iota(I32, (KEY_BLOCK, KEY_BLOCK), 0)
    c = lax.broadcasted_iota(I32, (KEY_BLOCK, KEY_BLOCK), 1)
    tri = (c <= r).astype(BF16)
    return pl.pallas_call(
        _cumsum_kernel,
        grid=(b,),
        in_specs=[pl.BlockSpec((1, l, N_HEADS), lambda i: (i, 0, 0)),
                  pl.BlockSpec((KEY_BLOCK, KEY_BLOCK), lambda i: (0, 0))],
        out_specs=pl.BlockSpec((1, l, N_HEADS), lambda i: (i, 0, 0)),
        out_shape=jax.ShapeDtypeStruct((b, l, N_HEADS), F32),
        compiler_params=_params("parallel"),
        name="forget_cumsum",
    )(lf, tri)


def _head_masks(width):
    lane = lax.broadcasted_iota(I32, (1, width), 1)
    return lane < HEAD_DIM


def _sb_kernel(q_ref, k_ref, v_ref, tri_ref, o_ref, *, tq, past):
    i = pl.program_id(2)
    q = q_ref[0]
    first = _head_masks(LANES)
    zero_q = jnp.zeros_like(q)
    qh = (jnp.where(first, q, zero_q), jnp.where(first, zero_q, q))
    q_pos0 = past + i * tq
    diag = q_pos0 // KEY_BLOCK
    tri = tri_ref[...]

    def block(j, carry, masked):
        run, acc = carry
        k0 = pl.multiple_of(j * KEY_BLOCK, KEY_BLOCK)
        kb = k_ref[0, pl.ds(k0, KEY_BLOCK), :]
        vb = v_ref[0, pl.ds(k0, KEY_BLOCK), :]
        if masked:
            kpos = k0 + lax.broadcasted_iota(I32, (tq, KEY_BLOCK), 1)
            qpos = q_pos0 + lax.broadcasted_iota(I32, (tq, KEY_BLOCK), 0)
            mask = kpos < qpos
        ws = []
        new_run = []
        for h in range(2):
            z = lax.dot_general(qh[h], kb, (((1,), (1,)), ((), ())), preferred_element_type=F32)
            sp = jnp.log1p(jnp.exp(-jnp.abs(z)))
            log_keep = -(jnp.maximum(z, 0.0) + sp)
            log_beta = jnp.minimum(z, 0.0) - sp
            if masked:
                log_keep = jnp.where(mask, log_keep, 0.0)
            hi = log_keep.astype(BF16)
            lo = (log_keep - hi.astype(F32)).astype(BF16)
            c = jnp.dot(jnp.concatenate([hi, lo], axis=1), tri, preferred_element_type=F32)
            w = jnp.exp(log_beta + c[:, :KEY_BLOCK] + run[h])
            if masked:
                w = jnp.where(mask, w, 0.0)
            new_run.append(run[h] + c[:, KEY_BLOCK:])
            ws.append(w.astype(BF16))
        zero_v = jnp.zeros_like(vb)
        v2 = jnp.concatenate([jnp.where(first, vb, zero_v), jnp.where(first, zero_v, vb)], axis=0)
        acc = acc + jnp.dot(jnp.concatenate(ws, axis=1), v2, preferred_element_type=F32)
        return (tuple(new_run), acc)

    zeros = jnp.zeros((tq, KEY_BLOCK), F32)
    carry = block(diag, ((zeros, zeros), jnp.zeros((tq, LANES), F32)), True)

    def body(it, carry):
        return block(diag - 1 - it, carry, False)

    _, acc = lax.fori_loop(0, diag, body, carry)
    o_ref[0] = acc


def _cumsum_rhs():
    r = lax.broadcasted_iota(I32, (2 * KEY_BLOCK, 2 * KEY_BLOCK), 0) % KEY_BLOCK
    c = lax.broadcasted_iota(I32, (2 * KEY_BLOCK, 2 * KEY_BLOCK), 1)
    return ((c >= KEY_BLOCK) | (r > c)).astype(BF16)


def _stick_breaking_attention(q, k, v, tq, past):
    b, t, _ = q.shape
    lk = k.shape[1]
    pairs = GROUP_WIDTH // LANES
    return pl.pallas_call(
        functools.partial(_sb_kernel, tq=tq, past=past),
        grid=(b, pairs, t // tq),
        in_specs=[
            pl.BlockSpec((1, tq, LANES), lambda bi, hp, i: (bi, i, hp)),
            pl.BlockSpec((1, lk, LANES), lambda bi, hp, i: (bi, 0, hp)),
            pl.BlockSpec((1, lk, LANES), lambda bi, hp, i: (bi, 0, hp)),
            pl.BlockSpec((2 * KEY_BLOCK, 2 * KEY_BLOCK), lambda bi, hp, i: (0, 0)),
        ],
        out_specs=pl.BlockSpec((1, tq, LANES), lambda bi, hp, i: (bi, i, hp)),
        out_shape=jax.ShapeDtypeStruct((b, t, GROUP_WIDTH), F32),
        compiler_params=_params("parallel", "parallel", "arbitrary"),
        name="stick_breaking_attention",
    )(q, k, v, _cumsum_rhs())


def _fox_kernel(q_ref, k_ref, v_ref, cq_ref, ck_ref, o_ref, *, tq, past):
    i = pl.program_id(2)
    hp = pl.program_id(1)
    q = q_ref[0]
    first = _head_masks(LANES)
    zero_q = jnp.zeros_like(q)
    qh = (jnp.where(first, q, zero_q), jnp.where(first, zero_q, q))
    q_pos0 = past + i * tq
    diag = q_pos0 // KEY_BLOCK
    cq_all = cq_ref[0]
    head_lane = lax.broadcasted_iota(I32, (1, N_HEADS), 1)
    cq = [jnp.sum(jnp.where(head_lane == 2 * hp + h, cq_all, 0.0), axis=1, keepdims=True) for h in range(2)]

    def block(j, carry, masked):
        ms, ls, acc = carry
        k0 = pl.multiple_of(j * KEY_BLOCK, KEY_BLOCK)
        kb = k_ref[0, pl.ds(k0, KEY_BLOCK), :]
        vb = v_ref[0, pl.ds(k0, KEY_BLOCK), :]
        ck = ck_ref[0, 0, j]
        if masked:
            kpos = k0 + lax.broadcasted_iota(I32, (tq, KEY_BLOCK), 1)
            qpos = q_pos0 + lax.broadcasted_iota(I32, (tq, KEY_BLOCK), 0)
            mask = kpos <= qpos
        ps, new_m, new_l, scales = [], [], [], []
        for h in range(2):
            s = lax.dot_general(qh[h], kb, (((1,), (1,)), ((), ())), preferred_element_type=F32)
            s = s + cq[h] - ck[h:h + 1, :]
            if masked:
                s = jnp.where(mask, s, -jnp.inf)
            m = jnp.maximum(ms[h], jnp.max(s, axis=1, keepdims=True))
            p = jnp.exp(s - m)
            a = jnp.exp(ms[h] - m)
            new_m.append(m)
            new_l.append(a * ls[h] + jnp.sum(p, axis=1, keepdims=True))
            scales.append(a)
            ps.append(p.astype(BF16))
        zero_v = jnp.zeros_like(vb)
        v2 = jnp.concatenate([jnp.where(first, vb, zero_v), jnp.where(first, zero_v, vb)], axis=0)
        pv = jnp.dot(jnp.concatenate(ps, axis=1), v2, preferred_element_type=F32)
        acc = acc * jnp.where(first, scales[0], scales[1]) + pv
        return (tuple(new_m), tuple(new_l), acc)

    neg = jnp.full((tq, 1), -jnp.inf, F32)
    zero = jnp.zeros((tq, 1), F32)
    carry = block(diag, ((neg, neg), (zero, zero), jnp.zeros((tq, LANES), F32)), True)

    def body(it, carry):
        return block(diag - 1 - it, carry, False)

    _, ls, acc = lax.fori_loop(0, diag, body, carry)
    o_ref[0] = acc / jnp.where(first, ls[0], ls[1])


def _forgetting_attention(q, k, v, cq, ck, tq, past):
    b, t, _ = q.shape
    lk = k.shape[1]
    pairs = GROUP_WIDTH // LANES
    return pl.pallas_call(
        functools.partial(_fox_kernel, tq=tq, past=past),
        grid=(b, pairs, t // tq),
        in_specs=[
            pl.BlockSpec((1, tq, LANES), lambda bi, hp, i: (bi, i, hp)),
            pl.BlockSpec((1, lk, LANES), lambda bi, hp, i: (bi, 0, hp)),
            pl.BlockSpec((1, lk, LANES), lambda bi, hp, i: (bi, 0, hp)),
            pl.BlockSpec((1, tq, N_HEADS), lambda bi, hp, i: (bi, i, 0)),
            pl.BlockSpec((1, 1, lk // KEY_BLOCK, 2, KEY_BLOCK), lambda bi, hp, i: (bi, hp, 0, 0, 0)),
        ],
        out_specs=pl.BlockSpec((1, tq, LANES), lambda bi, hp, i: (bi, i, hp)),
        out_shape=jax.ShapeDtypeStruct((b, t, GROUP_WIDTH), F32),
        compiler_params=_params("parallel", "parallel", "arbitrary"),
        name="forgetting_attention",
    )(q, k, v, cq, ck)


def _mixout_kernel(osb_ref, ofx_ref, x_ref, gn_ref, w_ref, g_ref, b_ref, y_ref):
    def rms(o, g):
        return o * lax.rsqrt(jnp.mean(o * o, axis=-1, keepdims=True) + GN_EPS) * g

    gn = gn_ref[...]
    o = jnp.concatenate([rms(osb_ref[...], gn[:, :GROUP_WIDTH]), rms(ofx_ref[...], gn[:, GROUP_WIDTH:])], axis=1)
    mix = jnp.dot(o.astype(BF16), w_ref[...], preferred_element_type=F32)
    y_ref[...] = _layer_norm_rows(DN_ALPHA * x_ref[...] + mix, g_ref[...], b_ref[...])


def _mix_out(osb, ofx, x2d, w_gn, w_out, g, b, tm):
    m = x2d.shape[0]
    row = pl.BlockSpec((1, D_MODEL), lambda i: (0, 0))
    return pl.pallas_call(
        _mixout_kernel,
        grid=(m // tm,),
        in_specs=[
            pl.BlockSpec((tm, GROUP_WIDTH), lambda i: (i, 0)),
            pl.BlockSpec((tm, GROUP_WIDTH), lambda i: (i, 0)),
            pl.BlockSpec((tm, D_MODEL), lambda i: (i, 0)),
            row,
            pl.BlockSpec((D_MODEL, D_MODEL), lambda i: (0, 0)),
            row, row,
        ],
        out_specs=pl.BlockSpec((tm, D_MODEL), lambda i: (i, 0)),
        out_shape=jax.ShapeDtypeStruct((m, D_MODEL), F32),
        compiler_params=_params("parallel"),
        name="mix_out_ln1",
    )(osb, ofx, x2d, w_gn, w_out, g, b)


def _memkv_kernel(m_ref, wk_ref, wv_ref, k_ref, v_ref):
    mb = m_ref[...].astype(BF16)
    k_ref[...] = jnp.dot(mb, wk_ref[...], preferred_element_type=F32)
    v_ref[...] = jnp.dot(mb, wv_ref[...], preferred_element_type=F32)


def _mem_kv(mem2d, w_mk, w_mv, tm):
    m = mem2d.shape[0]
    wspec = pl.BlockSpec((D_MODEL, D_MODEL), lambda i: (0, 0))
    blk = pl.BlockSpec((tm, D_MODEL), lambda i: (i, 0))
    out = jax.ShapeDtypeStruct((m, D_MODEL), F32)
    return pl.pallas_call(
        _memkv_kernel,
        grid=(m // tm,),
        in_specs=[blk, wspec, wspec],
        out_specs=[blk, blk],
        out_shape=[out, out],
        compiler_params=_params("parallel"),
        name="mem_kv",
    )(mem2d, w_mk, w_mv)


def _memattn_kernel(x_ref, mk_ref, mv_ref, wq_ref, wo_ref, g_ref, b_ref, y_ref):
    x = x_ref[0]
    q = jnp.dot(x.astype(BF16), wq_ref[...], preferred_element_type=F32)
    qb = (q * (MEM_HEAD_DIM ** -0.5)).astype(BF16)
    mk = mk_ref[0].astype(BF16)
    mv = mv_ref[0].astype(BF16)
    outs = []
    for h in range(MEM_HEADS):
        sl = slice(h * MEM_HEAD_DIM, (h + 1) * MEM_HEAD_DIM)
        s = lax.dot_general(qb[:, sl], mk[:, sl], (((1,), (1,)), ((), ())), preferred_element_type=F32)
        p = jnp.exp(s - jnp.max(s, axis=1, keepdims=True))
        o = jnp.dot(p.astype(BF16), mv[:, sl], preferred_element_type=F32)
        outs.append(o / jnp.sum(p, axis=1, keepdims=True))
    o = jnp.concatenate(outs, axis=1).astype(BF16)
    att = jnp.dot(o, wo_ref[...], preferred_element_type=F32)
    y_ref[0] = _layer_norm_rows(DN_ALPHA * x + att, g_ref[...], b_ref[...])


def _mem_attention(x3d, mk, mv, w_mq, w_mo, g, b, tm):
    bsz, t, _ = x3d.shape
    n_mem = mk.shape[1]
    row = pl.BlockSpec((1, D_MODEL), lambda bi, i: (0, 0))
    wspec = pl.BlockSpec((D_MODEL, D_MODEL), lambda bi, i: (0, 0))
    return pl.pallas_call(
        _memattn_kernel,
        grid=(bsz, t // tm),
        in_specs=[
            pl.BlockSpec((1, tm, D_MODEL), lambda bi, i: (bi, i, 0)),
            pl.BlockSpec((1, n_mem, D_MODEL), lambda bi, i: (bi, 0, 0)),
            pl.BlockSpec((1, n_mem, D_MODEL), lambda bi, i: (bi, 0, 0)),
            wspec, wspec, row, row,
        ],
        out_specs=pl.BlockSpec((1, tm, D_MODEL), lambda bi, i: (bi, i, 0)),
        out_shape=jax.ShapeDtypeStruct((bsz, t, D_MODEL), F32),
        compiler_params=_params("parallel", "parallel"),
        name="mem_attention_ln2",
    )(x3d, mk, mv, w_mq, w_mo, g, b)


def _topk_rows(s, key, k, big):
    vals, keys = [], []
    for r in range(k):
        m = jnp.max(s, axis=0, keepdims=True)
        km = jnp.min(jnp.where(s == m, key, big), axis=0, keepdims=True)
        vals.append(m)
        keys.append(km)
        if r + 1 < k:
            s = jnp.where(key == km, -jnp.inf, s)
    return jnp.concatenate(vals, axis=0), jnp.concatenate(keys, axis=0)


HALF_EXPERTS = PEER_EXPERTS // 2
HALF_SHIFT = HALF_EXPERTS.bit_length() - 1
HIGH_MASK = -65536
CODE_SHIFT_BIT = 16


def _row_code(e):
    return ((e & (HALF_EXPERTS - 1)) * SUBLANES) | ((e >> HALF_SHIFT) << (CODE_SHIFT_BIT + 4))


def _route_kernel(x_ref, w_ref, ka_ref, kb_ref, idx_ref, g_ref, q_scr):
    tm = x_ref.shape[0]
    xb = x_ref[...].astype(BF16)
    for c in range(2 * PEER_HEADS):
        q_scr[c] = jnp.dot(xb, w_ref[:, c * PEER_HALF:(c + 1) * PEER_HALF],
                           preferred_element_type=F32).astype(BF16)
    key_io = lax.broadcasted_iota(I32, (PEER_NKEYS, tm), 0)
    n_cand = PEER_TOPK * PEER_TOPK
    nt = (((1,), (1,)), ((), ()))
    half = PEER_TOPK // 2
    cols = [PEER_TOPK, half] + [SUBLANES] * (half - 2)
    sub = lambda n: lax.broadcasted_iota(I32, (n, tm), 0)
    pos = jnp.concatenate([i * PEER_TOPK + sub(n) for i, n in enumerate(cols)] + [(half + sub(half)) * PEER_TOPK],
                          axis=0) * PEER_EXPERTS

    def head(h):
        sa = lax.dot_general(ka_ref[h], q_scr[2 * h], nt, preferred_element_type=F32)
        sb = lax.dot_general(kb_ref[h], q_scr[2 * h + 1], nt, preferred_element_type=F32)
        va, ia = _topk_rows(sa, key_io, PEER_TOPK, PEER_NKEYS)
        vb, ib = _topk_rows(sb, key_io, PEER_TOPK, PEER_NKEYS)
        cand = jnp.concatenate([va[i:i + 1] + vb[:n] for i, n in enumerate(cols)] + [va[half:] + vb[0:1]], axis=0)
        cidx = jnp.concatenate([ia[i:i + 1] * PEER_NKEYS + ib[:n] for i, n in enumerate(cols)]
                               + [ia[half:] * PEER_NKEYS + ib[0:1]], axis=0)
        top, tkey = _topk_rows(cand, pos + cidx, PEER_TOPK, n_cand * PEER_EXPERTS)
        e = jnp.exp(top - top[0:1])
        r0 = pl.multiple_of(h * PEER_TOPK, PEER_TOPK)
        idx_ref[0, pl.ds(r0, PEER_TOPK), :] = _row_code(tkey & (PEER_EXPERTS - 1))
        g_ref[0, pl.ds(r0, PEER_TOPK), :] = e / jnp.sum(e, axis=0, keepdims=True)

    def head_pair(i, carry):
        head(2 * i)
        head(2 * i + 1)
        return carry

    lax.fori_loop(0, PEER_HEADS // 2, head_pair, 0)


def _peer_route(x2d, w_pq, keys_a, keys_b, tm):
    m = x2d.shape[0]
    nt = m // tm
    kspec = pl.BlockSpec((PEER_HEADS, PEER_NKEYS, PEER_HALF), lambda i: (0, 0, 0))
    ospec = pl.BlockSpec((1, PEER_PICKS, tm), lambda i: (i, 0, 0))
    return pl.pallas_call(
        _route_kernel,
        grid=(nt,),
        in_specs=[pl.BlockSpec((tm, D_MODEL), lambda i: (i, 0)),
                  pl.BlockSpec((D_MODEL, 2 * PEER_HEADS * PEER_HALF), lambda i: (0, 0)),
                  kspec, kspec],
        out_specs=[ospec, ospec],
        out_shape=[jax.ShapeDtypeStruct((nt, PEER_PICKS, tm), I32),
                   jax.ShapeDtypeStruct((nt, PEER_PICKS, tm), F32)],
        scratch_shapes=[pltpu.VMEM((2 * PEER_HEADS, tm, PEER_HALF), BF16)],
        compiler_params=_params("parallel"),
        name="peer_route",
    )(x2d, w_pq, keys_a, keys_b)


def _pack_table(t):
    bits = lax.bitcast_convert_type(t.astype(BF16), jnp.uint16).astype(jnp.uint32)
    word = (bits[:HALF_EXPERTS] << 16) | bits[HALF_EXPERTS:]
    return lax.bitcast_convert_type(word, I32).reshape(HALF_EXPERTS * SUBLANES, LANES)


def _table_spec():
    return pl.BlockSpec((HALF_EXPERTS * SUBLANES, LANES), lambda i: (0, 0), pipeline_mode=pl.Buffered(1))


def _table_row(tab_ref, code):
    off = pl.multiple_of(code & ((1 << CODE_SHIFT_BIT) - 1), SUBLANES)
    row = tab_ref[pl.ds(off, SUBLANES), :]
    return pltpu.bitcast((row << (code >> CODE_SHIFT_BIT)) & HIGH_MASK, F32)


BIT_REVERSED = (0, 4, 2, 6, 1, 5, 3, 7)


def _sublane_sums(prods, sub_io):
    def merge(a, b, h):
        low = (sub_io & h) == 0
        return jnp.where(low, a, pltpu.roll(b, h, 0)) + jnp.where(low, pltpu.roll(a, SUBLANES - h, 0), b)

    p = [prods[BIT_REVERSED[k]] for k in range(SUBLANES)]
    t = [merge(p[2 * k], p[2 * k + 1], 4) for k in range(4)]
    u = [merge(t[2 * k], t[2 * k + 1], 2) for k in range(2)]
    return merge(u[0], u[1], 1)


REDUCE_UNROLL = 8
OFFSET_MASK = (1 << CODE_SHIFT_BIT) - SUBLANES


def _peer_in_kernel(idx_ref, x_ref, code_ref, g_ref, tab_ref, w_ref, part_ref):
    tm = x_ref.shape[0]
    sub_io = lax.broadcasted_iota(I32, (SUBLANES, LANES), 0)
    tok_io = lax.broadcasted_iota(I32, (PEER_PICKS, tm), 1)

    def token(t, carry):
        x = x_ref[t]
        for g8 in range(PEER_PICKS // SUBLANES):
            prods = [x * _table_row(tab_ref, idx_ref[0, t, g8 * SUBLANES + s]) for s in range(SUBLANES)]
            part_ref[t, g8 * SUBLANES:(g8 + 1) * SUBLANES, :] = _sublane_sums(prods, sub_io)
        return carry

    lax.fori_loop(0, tm, token, 0)

    def reduce(i, h_t):
        for k in range(REDUCE_UNROLL):
            t = i * REDUCE_UNROLL + k
            col = jnp.sum(part_ref[t], axis=1, keepdims=True)
            h_t = jnp.where(tok_io == t, col, h_t)
        return h_t

    h_t = lax.fori_loop(0, tm // REDUCE_UNROLL, reduce, jnp.zeros((PEER_PICKS, tm), F32))
    gelu = 0.5 * h_t * (1.0 + lax.erf(h_t * (2.0 ** -0.5)))
    w_bits = pltpu.bitcast((gelu * g_ref[0]).astype(BF16).astype(F32), I32)
    code = code_ref[0]
    w_ref[0] = w_bits | (code & OFFSET_MASK) | (code >> (CODE_SHIFT_BIT + 4))


def _peer_in(idx_t, x3, code, gate, table, tm):
    nt = idx_t.shape[0]
    tspec = pl.BlockSpec((1, PEER_PICKS, tm), lambda i: (i, 0, 0))
    return pl.pallas_call(
        _peer_in_kernel,
        grid=(nt,),
        in_specs=[pl.BlockSpec((1, tm, PEER_PICKS), lambda i: (i, 0, 0), memory_space=pltpu.SMEM),
                  pl.BlockSpec((tm, SUBLANES, LANES), lambda i: (i, 0, 0)),
                  tspec, tspec,
                  _table_spec()],
        out_specs=tspec,
        out_shape=jax.ShapeDtypeStruct((nt, PEER_PICKS, tm), I32),
        scratch_shapes=[pltpu.VMEM((tm, PEER_PICKS, LANES), F32)],
        compiler_params=_params("arbitrary"),
        name="peer_expert_in",
    )(idx_t, x3, code, gate, table)


def _peer_out_kernel(word_ref, x_ref, g_ref, b_ref, tab_ref, y_ref):
    tm = x_ref.shape[0]

    def token(t, carry):
        acc = DN_ALPHA * x_ref[t]
        for p in range(PEER_PICKS):
            word = word_ref[0, t, p]
            off = pl.multiple_of(word & OFFSET_MASK, SUBLANES)
            row = tab_ref[pl.ds(off, SUBLANES), :]
            wv = jnp.full((SUBLANES, LANES), word, I32)
            val = pltpu.bitcast((row << ((wv & 1) << 4)) & HIGH_MASK, F32)
            acc = acc + pltpu.bitcast(wv & HIGH_MASK, F32) * val
        y_ref[t] = acc
        return carry

    lax.fori_loop(0, tm, token, 0)
    r = y_ref[...]
    n = float(D_MODEL)
    mu = jnp.sum(jnp.sum(r, axis=2, keepdims=True), axis=1, keepdims=True) / n
    d = r - mu
    var = jnp.sum(jnp.sum(d * d, axis=2, keepdims=True), axis=1, keepdims=True) / n
    y_ref[...] = d * lax.rsqrt(var + LN_EPS) * g_ref[...] + b_ref[...]


def _peer_out(words_t, x3, g3, b3, table, tm):
    nt = words_t.shape[0]
    m = x3.shape[0]
    sspec = pl.BlockSpec((1, tm, PEER_PICKS), lambda i: (i, 0, 0), memory_space=pltpu.SMEM)
    vec = pl.BlockSpec((1, SUBLANES, LANES), lambda i: (0, 0, 0))
    xspec = pl.BlockSpec((tm, SUBLANES, LANES), lambda i: (i, 0, 0))
    return pl.pallas_call(
        _peer_out_kernel,
        grid=(nt,),
        in_specs=[sspec, xspec, vec, vec, _table_spec()],
        out_specs=xspec,
        out_shape=jax.ShapeDtypeStruct((m, SUBLANES, LANES), F32),
        compiler_params=_params("arbitrary"),
        name="peer_expert_out",
    )(words_t, x3, g3, b3, table)


def _pick_tile(n, pref):
    t = pref
    while n % t:
        t //= 2
    return t


def _layer(x, past, mem_k, mem_v, wts):
    b, t, _ = x.shape
    m = b * t
    x2d = x.reshape(m, D_MODEL)
    tm = _pick_tile(m, 256)

    (qsb, ksb, vsb, qfx, kfx, vfx, ksbb, vsbb, kfxb, vfxb, lf) = _in_projection(
        x2d, wts["w_in_main"], wts["w_in_f"], wts["b_f"], tm)
    state = tuple(a.reshape(b, t, N_HEADS, HEAD_DIM) for a in (ksb, vsb, kfx, vfx)) + (lf.reshape(b, t, N_HEADS),)

    r3 = lambda a: a.reshape(b, t, GROUP_WIDTH)
    if past is None:
        p = 0
        k_sb, v_sb, k_fx, v_fx = r3(ksbb), r3(vsbb), r3(kfxb), r3(vfxb)
        lf_all = lf.reshape(b, t, N_HEADS)
    else:
        p = past[0].shape[1]
        pad = (-(p + t)) % KEY_BLOCK

        def cat(c, new):
            parts = [c.reshape(b, p, GROUP_WIDTH).astype(BF16), r3(new)]
            if pad:
                parts.append(jnp.zeros((b, pad, GROUP_WIDTH), BF16))
            return jnp.concatenate(parts, axis=1)

        k_sb, v_sb, k_fx, v_fx = cat(past[0], ksbb), cat(past[1], vsbb), cat(past[2], kfxb), cat(past[3], vfxb)
        parts = [past[4].astype(F32), lf.reshape(b, t, N_HEADS)]
        if pad:
            parts.append(jnp.zeros((b, pad, N_HEADS), F32))
        lf_all = jnp.concatenate(parts, axis=1)

    c_all = _forget_cumsum(lf_all)
    lk = c_all.shape[1]
    cq = c_all[:, p:p + t]
    ck = c_all.transpose(0, 2, 1).reshape(b, N_HEADS // 2, 2, lk // KEY_BLOCK, KEY_BLOCK).transpose(0, 1, 3, 2, 4)

    tq = min(KEY_BLOCK, t)
    o_sb = _stick_breaking_attention(r3(qsb), k_sb, v_sb, tq, p)
    o_fx = _forgetting_attention(r3(qfx), k_fx, v_fx, cq, ck, tq, p)

    x1 = _mix_out(o_sb.reshape(m, GROUP_WIDTH), o_fx.reshape(m, GROUP_WIDTH), x2d,
                  wts["w_gn"], wts["w_out"], wts["ln1_g"], wts["ln1_b"], tm)
    x2 = _mem_attention(x1.reshape(b, t, D_MODEL), mem_k, mem_v, wts["w_mq"], wts["w_mo"],
                        wts["ln2_g"], wts["ln2_b"], _pick_tile(t, 256))
    x2d2 = x2.reshape(m, D_MODEL)

    tr = LANES
    idx, gate = _peer_route(x2d2, wts["w_pq"], wts["keys_a"], wts["keys_b"], tr)
    x3 = x2d2.reshape(m, SUBLANES, LANES)
    words = _peer_in(idx.transpose(0, 2, 1), x3, idx, gate, wts["table_u"], tr)
    y = _peer_out(words.transpose(0, 2, 1), x3, wts["ln3_g3"], wts["ln3_b3"], wts["table_v"], tr)
    return y.reshape(b, t, D_MODEL), state


def kernel(x_prompt, x_sample, mem_prompt, cache_sb_k, cache_sb_v, cache_fox_k, cache_fox_v, cache_fox_logf,
           cache_mem_k, cache_mem_v, w_in, b_f, w_gn, w_out, ln1_g, ln1_b, w_mq, w_mk, w_mv, w_mo, ln2_g, ln2_b,
           w_pq, peer_keys_a, peer_keys_b, peer_u, peer_v, ln3_g, ln3_b):
    depth = w_in.shape[0]
    hp, hs = x_prompt, x_sample
    bp = x_prompt.shape[0]
    n_mem = mem_prompt.shape[1]
    mix_cols = 6 * GROUP_WIDTH
    outs_p = [[] for _ in range(7)]
    outs_s = [[] for _ in range(5)]
    row = lambda a: a.reshape(1, D_MODEL)
    for l in range(depth):
        wts = {
            "w_in_main": w_in[l][:, :mix_cols].astype(BF16),
            "w_in_f": jnp.pad(w_in[l][:, mix_cols:], ((0, 0), (0, LANES - N_HEADS))).astype(BF16),
            "b_f": jnp.pad(b_f[l], (0, LANES - N_HEADS)).reshape(1, LANES),
            "w_gn": row(w_gn[l]), "w_out": w_out[l].astype(BF16),
            "ln1_g": row(ln1_g[l]), "ln1_b": row(ln1_b[l]),
            "w_mq": w_mq[l].astype(BF16), "w_mo": w_mo[l].astype(BF16),
            "ln2_g": row(ln2_g[l]), "ln2_b": row(ln2_b[l]),
            "w_pq": w_pq[l].astype(BF16),
            "keys_a": peer_keys_a[l].astype(BF16), "keys_b": peer_keys_b[l].astype(BF16),
            "table_u": _pack_table(peer_u[l]), "table_v": _pack_table(peer_v[l]),
            "ln3_g3": ln3_g[l].reshape(1, SUBLANES, LANES), "ln3_b3": ln3_b[l].reshape(1, SUBLANES, LANES),
        }
        mem2d = mem_prompt.reshape(bp * n_mem, D_MODEL)
        mk_p, mv_p = _mem_kv(mem2d, w_mk[l].astype(BF16), w_mv[l].astype(BF16), _pick_tile(bp * n_mem, 512))
        mk_p = mk_p.reshape(bp, n_mem, D_MODEL)
        mv_p = mv_p.reshape(bp, n_mem, D_MODEL)
        hp, st_p = _layer(hp, None, mk_p, mv_p, wts)
        bs = x_sample.shape[0]
        past = (cache_sb_k[l], cache_sb_v[l], cache_fox_k[l], cache_fox_v[l], cache_fox_logf[l])
        hs, st_s = _layer(hs, past, cache_mem_k[l].reshape(bs, -1, D_MODEL), cache_mem_v[l].reshape(bs, -1, D_MODEL), wts)
        for i in range(5):
            outs_p[i].append(st_p[i])
            outs_s[i].append(st_s[i])
        outs_p[5].append(mk_p.reshape(bp, n_mem, MEM_HEADS, MEM_HEAD_DIM))
        outs_p[6].append(mv_p.reshape(bp, n_mem, MEM_HEADS, MEM_HEAD_DIM))
    stack = lambda xs: jnp.stack(xs)
    return (hp, hs) + tuple(stack(o) for o in outs_p) + tuple(stack(o) for o in outs_s)
```

```python
import functools

import jax
import jax.numpy as jnp
from jax import lax
from jax.experimental import pallas as pl
from jax.experimental.pallas import tpu as pltpu

F32 = jnp.float32
BF16 = jnp.bfloat16
I32 = jnp.int32

D_MODEL = 1024
HEAD_DIM = 64
N_HEADS = 8
GROUP_WIDTH = N_HEADS * HEAD_DIM
MEM_HEADS = 4
MEM_HEAD_DIM = D_MODEL // MEM_HEADS
PEER_HEADS = 8
PEER_NKEYS = 128
PEER_TOPK = 16
PEER_HALF = 128
PEER_PICKS = PEER_HEADS * PEER_TOPK
PEER_EXPERTS = PEER_NKEYS * PEER_NKEYS
DN_ALPHA = 2.0 ** 0.25
LN_EPS = 1e-5
GN_EPS = 1e-6

LANES = 128
SUBLANES = 8
KEY_BLOCK = 128
FOX_KEY_BLOCK = 256
QUERY_BLOCK = 256
VMEM_LIMIT = 56 * 1024 * 1024


def _params(*sem):
    return pltpu.CompilerParams(dimension_semantics=sem, vmem_limit_bytes=VMEM_LIMIT)


def _log_sigmoid(x):
    return jnp.minimum(x, 0.0) - jnp.log1p(jnp.exp(-jnp.abs(x)))


def _layer_norm_rows(r, g, b):
    mu = jnp.mean(r, axis=-1, keepdims=True)
    d = r - mu
    var = jnp.mean(d * d, axis=-1, keepdims=True)
    return d * lax.rsqrt(var + LN_EPS) * g + b


def _inproj_kernel(x_ref, w_ref, wf_ref, bf_ref,
                   qsb_ref, ksb_ref, vsb_ref, qfx_ref, kfx_ref, vfx_ref,
                   ksbb_ref, vsbb_ref, kfxb_ref, vfxb_ref, lf_ref):
    xb = x_ref[...].astype(BF16)

    def proj(j):
        return jnp.dot(xb, w_ref[:, j * GROUP_WIDTH:(j + 1) * GROUP_WIDTH], preferred_element_type=F32)

    scale = HEAD_DIM ** -0.5
    qsb_ref[...] = (proj(0) * scale).astype(BF16)
    k = proj(1)
    ksb_ref[...] = k
    ksbb_ref[...] = k.astype(BF16)
    v = proj(2)
    vsb_ref[...] = v
    vsbb_ref[...] = v.astype(BF16)
    qfx_ref[...] = (proj(3) * scale).astype(BF16)
    k = proj(4)
    kfx_ref[...] = k
    kfxb_ref[...] = k.astype(BF16)
    v = proj(5)
    vfx_ref[...] = v
    vfxb_ref[...] = v.astype(BF16)
    f = jnp.dot(xb, wf_ref[...], preferred_element_type=F32) + bf_ref[...]
    lf_ref[...] = _log_sigmoid(f)[:, :N_HEADS]


def _in_projection(x2d, w_main, w_f, b_f, tm):
    m = x2d.shape[0]
    f32o = jax.ShapeDtypeStruct((m, GROUP_WIDTH), F32)
    bf16o = jax.ShapeDtypeStruct((m, GROUP_WIDTH), BF16)
    blk = pl.BlockSpec((tm, GROUP_WIDTH), lambda i: (i, 0))
    return pl.pallas_call(
        _inproj_kernel,
        grid=(m // tm,),
        in_specs=[
            pl.BlockSpec((tm, D_MODEL), lambda i: (i, 0)),
            pl.BlockSpec((D_MODEL, 6 * GROUP_WIDTH), lambda i: (0, 0)),
            pl.BlockSpec((D_MODEL, LANES), lambda i: (0, 0)),
            pl.BlockSpec((1, LANES), lambda i: (0, 0)),
        ],
        out_specs=[blk] * 10 + [pl.BlockSpec((tm, N_HEADS), lambda i: (i, 0))],
        out_shape=[bf16o, f32o, f32o, bf16o, f32o, f32o, bf16o, bf16o, bf16o, bf16o,
                   jax.ShapeDtypeStruct((m, N_HEADS), F32)],
        compiler_params=_params("parallel"),
        name="in_projection",
    )(x2d, w_main, w_f, b_f)


def _cumsum_kernel(lf_ref, tri_ref, c_ref):
    n_chunks = lf_ref.shape[1] // KEY_BLOCK

    def chunk(i, carry):
        r0 = pl.multiple_of(i * KEY_BLOCK, KEY_BLOCK)
        v = lf_ref[0, pl.ds(r0, KEY_BLOCK), :]
        hi = v.astype(BF16)
        r1 = v - hi.astype(F32)
        mid = r1.astype(BF16)
        lo = (r1 - mid.astype(F32)).astype(BF16)
        parts = jnp.concatenate([hi, mid, lo], axis=1)
        s = jnp.dot(tri_ref[...], parts, preferred_element_type=F32)
        c = s[:, :N_HEADS] + s[:, N_HEADS:2 * N_HEADS] + s[:, 2 * N_HEADS:] + carry
        c_ref[0, pl.ds(r0, KEY_BLOCK), :] = c
        return c[KEY_BLOCK - 1:, :]

    lax.fori_loop(0, n_chunks, chunk, jnp.zeros((1, N_HEADS), F32))


def _forget_cumsum(lf):
    b, l, _ = lf.shape
    r = lax.broadcasted_iota(I32, (KEY_BLOCK, KEY_BLOCK), 0)
    c = lax.broadcasted_iota(I32, (KEY_BLOCK, KEY_BLOCK), 1)
    tri = (c <= r).astype(BF16)
    return pl.pallas_call(
        _cumsum_kernel,
        grid=(b,),
        in_specs=[pl.BlockSpec((1, l, N_HEADS), lambda i: (i, 0, 0)),
                  pl.BlockSpec((KEY_BLOCK, KEY_BLOCK), lambda i: (0, 0))],
        out_specs=pl.BlockSpec((1, l, N_HEADS), lambda i: (i, 0, 0)),
        out_shape=jax.ShapeDtypeStruct((b, l, N_HEADS), F32),
        compiler_params=_params("parallel"),
        name="forget_cumsum",
    )(lf, tri)


def _head_masks(width):
    lane = lax.broadcasted_iota(I32, (1, width), 1)
    return lane < HEAD_DIM


def _sb_kernel(q_ref, k_ref, v_ref, tri_ref, o_ref, *, tq, past):
    i = pl.program_id(2)
    q = q_ref[0]
    first = _head_masks(LANES)
    zero_q = jnp.zeros_like(q)
    qh = (jnp.where(first, q, zero_q), jnp.where(first, zero_q, q))
    q_pos0 = past + i * tq
    diag = q_pos0 // KEY_BLOCK
    n_diag = max(1, tq // KEY_BLOCK)
    tri = tri_ref[...]

    def block(j, run, acc, masked):
        k0 = pl.multiple_of(j * KEY_BLOCK, KEY_BLOCK)
        kb = k_ref[0, pl.ds(k0, KEY_BLOCK), :]
        vb = v_ref[0, pl.ds(k0, KEY_BLOCK), :]
        if masked:
            kpos = k0 + lax.broadcasted_iota(I32, (tq, KEY_BLOCK), 1)
            qpos = q_pos0 + lax.broadcasted_iota(I32, (tq, KEY_BLOCK), 0)
            mask = kpos < qpos
        ws = []
        new_run = []
        for h in range(2):
            z = lax.dot_general(qh[h], kb, (((1,), (1,)), ((), ())), preferred_element_type=F32)
            sp = jnp.maximum(z, 0.0) + jnp.log(1.0 + jnp.exp(-jnp.abs(z)))
            log_beta = z - sp
            if masked:
                sp = jnp.where(mask, sp, 0.0)
            hi = sp.astype(BF16)
            lo = (sp - hi.astype(F32)).astype(BF16)
            c = jnp.dot(jnp.concatenate([hi, lo], axis=1), tri, preferred_element_type=F32)
            w = jnp.exp(log_beta + c[:, :KEY_BLOCK] + run[h])
            if masked:
                w = jnp.where(mask, w, 0.0)
            new_run.append(run[h] + c[:, KEY_BLOCK:])
            ws.append(w.astype(BF16))
        zero_v = jnp.zeros_like(vb)
        v2 = jnp.concatenate([jnp.where(first, vb, zero_v), jnp.where(first, zero_v, vb)], axis=0)
        acc = acc + jnp.dot(jnp.concatenate(ws, axis=1), v2, preferred_element_type=F32)
        return tuple(new_run), acc

    def alive(run):
        return (jnp.max(jnp.maximum(run[0], run[1])) > EXP_UNDERFLOW).astype(I32)

    zeros = jnp.zeros((tq, KEY_BLOCK), F32)
    run, acc = (zeros, zeros), jnp.zeros((tq, LANES), F32)
    for d in reversed(range(n_diag)):
        run, acc = block(diag + d, run, acc, True)

    def cond(state):
        it, live, _, _ = state
        return (it < diag) & (live > 0)

    def body(state):
        it, _, run, acc = state
        run, acc = block(diag - 1 - it, run, acc, False)
        return it + 1, alive(run), run, acc

    _, _, _, acc = lax.while_loop(cond, body, (jnp.int32(0), alive(run), run, acc))
    o_ref[0] = acc


EXP_UNDERFLOW = -105.0


def _cumsum_rhs():
    r = lax.broadcasted_iota(I32, (2 * KEY_BLOCK, 2 * KEY_BLOCK), 0) % KEY_BLOCK
    c = lax.broadcasted_iota(I32, (2 * KEY_BLOCK, 2 * KEY_BLOCK), 1)
    return -((c >= KEY_BLOCK) | (r > c)).astype(BF16)


def _stick_breaking_attention(q, k, v, tq, past):
    b, t, _ = q.shape
    lk = k.shape[1]
    pairs = GROUP_WIDTH // LANES
    return pl.pallas_call(
        functools.partial(_sb_kernel, tq=tq, past=past),
        grid=(b, pairs, t // tq),
        in_specs=[
            pl.BlockSpec((1, tq, LANES), lambda bi, hp, i: (bi, i, hp)),
            pl.BlockSpec((1, lk, LANES), lambda bi, hp, i: (bi, 0, hp)),
            pl.BlockSpec((1, lk, LANES), lambda bi, hp, i: (bi, 0, hp)),
            pl.BlockSpec((2 * KEY_BLOCK, 2 * KEY_BLOCK), lambda bi, hp, i: (0, 0)),
        ],
        out_specs=pl.BlockSpec((1, tq, LANES), lambda bi, hp, i: (bi, i, hp)),
        out_shape=jax.ShapeDtypeStruct((b, t, GROUP_WIDTH), F32),
        compiler_params=_params("parallel", "parallel", "arbitrary"),
        name="stick_breaking_attention",
    )(q, k, v, _cumsum_rhs())


def _fox_kernel(q_ref, k_ref, v_ref, cq_ref, ck_ref, o_ref, *, tq, past):
    i = pl.program_id(2)
    hp = pl.program_id(1)
    q = q_ref[0]
    first = _head_masks(LANES)
    zero_q = jnp.zeros_like(q)
    qh = (jnp.where(first, q, zero_q), jnp.where(first, zero_q, q))
    q_pos0 = past + i * tq
    diag = q_pos0 // FOX_KEY_BLOCK
    cq_all = cq_ref[0]
    head_lane = lax.broadcasted_iota(I32, (1, N_HEADS), 1)
    cq = [jnp.sum(jnp.where(head_lane == 2 * hp + h, cq_all, 0.0), axis=1, keepdims=True) for h in range(2)]

    def block(j, carry, masked):
        ms, ls, acc = carry
        k0 = pl.multiple_of(j * FOX_KEY_BLOCK, FOX_KEY_BLOCK)
        kb = k_ref[0, pl.ds(k0, FOX_KEY_BLOCK), :]
        vb = v_ref[0, pl.ds(k0, FOX_KEY_BLOCK), :]
        ck = ck_ref[0, 0, j]
        if masked:
            kpos = k0 + lax.broadcasted_iota(I32, (tq, FOX_KEY_BLOCK), 1)
            qpos = q_pos0 + lax.broadcasted_iota(I32, (tq, FOX_KEY_BLOCK), 0)
            mask = kpos <= qpos
        ps, new_m, new_l, scales = [], [], [], []
        for h in range(2):
            s = lax.dot_general(qh[h], kb, (((1,), (1,)), ((), ())), preferred_element_type=F32)
            s = s + cq[h] - ck[h:h + 1, :]
            if masked:
                s = jnp.where(mask, s, -jnp.inf)
            m = jnp.maximum(ms[h], jnp.max(s, axis=1, keepdims=True))
            p = jnp.exp(s - m)
            a = jnp.exp(ms[h] - m)
            new_m.append(m)
            new_l.append(a * ls[h] + jnp.sum(p, axis=1, keepdims=True))
            scales.append(a)
            ps.append(p.astype(BF16))
        zero_v = jnp.zeros_like(vb)
        v2 = jnp.concatenate([jnp.where(first, vb, zero_v), jnp.where(first, zero_v, vb)], axis=0)
        pv = jnp.dot(jnp.concatenate(ps, axis=1), v2, preferred_element_type=F32)
        acc = acc * jnp.where(first, scales[0], scales[1]) + pv
        return (tuple(new_m), tuple(new_l), acc)

    neg = jnp.full((tq, 1), -jnp.inf, F32)
    zero = jnp.zeros((tq, 1), F32)
    carry = block(diag, ((neg, neg), (zero, zero), jnp.zeros((tq, LANES), F32)), True)

    def body(it, carry):
        return block(diag - 1 - it, carry, False)

    _, ls, acc = lax.fori_loop(0, diag, body, carry)
    o_ref[0] = acc / jnp.where(first, ls[0], ls[1])


def _forgetting_attention(q, k, v, cq, ck, tq, past):
    b, t, _ = q.shape
    lk = k.shape[1]
    pairs = GROUP_WIDTH // LANES
    return pl.pallas_call(
        functools.partial(_fox_kernel, tq=tq, past=past),
        grid=(b, pairs, t // tq),
        in_specs=[
            pl.BlockSpec((1, tq, LANES), lambda bi, hp, i: (bi, i, hp)),
            pl.BlockSpec((1, lk, LANES), lambda bi, hp, i: (bi, 0, hp)),
            pl.BlockSpec((1, lk, LANES), lambda bi, hp, i: (bi, 0, hp)),
            pl.BlockSpec((1, tq, N_HEADS), lambda bi, hp, i: (bi, i, 0)),
            pl.BlockSpec((1, 1, lk // FOX_KEY_BLOCK, 2, FOX_KEY_BLOCK), lambda bi, hp, i: (bi, hp, 0, 0, 0)),
        ],
        out_specs=pl.BlockSpec((1, tq, LANES), lambda bi, hp, i: (bi, i, hp)),
        out_shape=jax.ShapeDtypeStruct((b, t, GROUP_WIDTH), F32),
        compiler_params=_params("parallel", "parallel", "arbitrary"),
        name="forgetting_attention",
    )(q, k, v, cq, ck)


def _mixout_kernel(osb_ref, ofx_ref, x_ref, gn_ref, w_ref, g_ref, b_ref, y_ref):
    def rms(o, g):
        return o * lax.rsqrt(jnp.mean(o * o, axis=-1, keepdims=True) + GN_EPS) * g

    gn = gn_ref[...]
    o = jnp.concatenate([rms(osb_ref[...], gn[:, :GROUP_WIDTH]), rms(ofx_ref[...], gn[:, GROUP_WIDTH:])], axis=1)
    mix = jnp.dot(o.astype(BF16), w_ref[...], preferred_element_type=F32)
    y_ref[...] = _layer_norm_rows(DN_ALPHA * x_ref[...] + mix, g_ref[...], b_ref[...])


def _mix_out(osb, ofx, x2d, w_gn, w_out, g, b, tm):
    m = x2d.shape[0]
    row = pl.BlockSpec((1, D_MODEL), lambda i: (0, 0))
    return pl.pallas_call(
        _mixout_kernel,
        grid=(m // tm,),
        in_specs=[
            pl.BlockSpec((tm, GROUP_WIDTH), lambda i: (i, 0)),
            pl.BlockSpec((tm, GROUP_WIDTH), lambda i: (i, 0)),
            pl.BlockSpec((tm, D_MODEL), lambda i: (i, 0)),
            row,
            pl.BlockSpec((D_MODEL, D_MODEL), lambda i: (0, 0)),
            row, row,
        ],
        out_specs=pl.BlockSpec((tm, D_MODEL), lambda i: (i, 0)),
        out_shape=jax.ShapeDtypeStruct((m, D_MODEL), F32),
        compiler_params=_params("parallel"),
        name="mix_out_ln1",
    )(osb, ofx, x2d, w_gn, w_out, g, b)


def _memkv_kernel(m_ref, wk_ref, wv_ref, k_ref, v_ref):
    mb = m_ref[...].astype(BF16)
    k_ref[...] = jnp.dot(mb, wk_ref[...], preferred_element_type=F32)
    v_ref[...] = jnp.dot(mb, wv_ref[...], preferred_element_type=F32)


def _mem_kv(mem2d, w_mk, w_mv, tm):
    m = mem2d.shape[0]
    wspec = pl.BlockSpec((D_MODEL, D_MODEL), lambda i: (0, 0))
    blk = pl.BlockSpec((tm, D_MODEL), lambda i: (i, 0))
    out = jax.ShapeDtypeStruct((m, D_MODEL), F32)
    return pl.pallas_call(
        _memkv_kernel,
        grid=(m // tm,),
        in_specs=[blk, wspec, wspec],
        out_specs=[blk, blk],
        out_shape=[out, out],
        compiler_params=_params("parallel"),
        name="mem_kv",
    )(mem2d, w_mk, w_mv)


def _memattn_kernel(x_ref, mk_ref, mv_ref, wq_ref, wo_ref, g_ref, b_ref, y_ref):
    x = x_ref[0]
    q = jnp.dot(x.astype(BF16), wq_ref[...], preferred_element_type=F32)
    qb = (q * (MEM_HEAD_DIM ** -0.5)).astype(BF16)
    mk = mk_ref[0].astype(BF16)
    mv = mv_ref[0].astype(BF16)
    outs = []
    for h in range(MEM_HEADS):
        sl = slice(h * MEM_HEAD_DIM, (h + 1) * MEM_HEAD_DIM)
        s = lax.dot_general(qb[:, sl], mk[:, sl], (((1,), (1,)), ((), ())), preferred_element_type=F32)
        p = jnp.exp(s - jnp.max(s, axis=1, keepdims=True))
        o = jnp.dot(p.astype(BF16), mv[:, sl], preferred_element_type=F32)
        outs.append(o / jnp.sum(p, axis=1, keepdims=True))
    o = jnp.concatenate(outs, axis=1).astype(BF16)
    att = jnp.dot(o, wo_ref[...], preferred_element_type=F32)
    y_ref[0] = _layer_norm_rows(DN_ALPHA * x + att, g_ref[...], b_ref[...])


def _mem_attention(x3d, mk, mv, w_mq, w_mo, g, b, tm):
    bsz, t, _ = x3d.shape
    n_mem = mk.shape[1]
    row = pl.BlockSpec((1, D_MODEL), lambda bi, i: (0, 0))
    wspec = pl.BlockSpec((D_MODEL, D_MODEL), lambda bi, i: (0, 0))
    return pl.pallas_call(
        _memattn_kernel,
        grid=(bsz, t // tm),
        in_specs=[
            pl.BlockSpec((1, tm, D_MODEL), lambda bi, i: (bi, i, 0)),
            pl.BlockSpec((1, n_mem, D_MODEL), lambda bi, i: (bi, 0, 0)),
            pl.BlockSpec((1, n_mem, D_MODEL), lambda bi, i: (bi, 0, 0)),
            wspec, wspec, row, row,
        ],
        out_specs=pl.BlockSpec((1, tm, D_MODEL), lambda bi, i: (bi, i, 0)),
        out_shape=jax.ShapeDtypeStruct((bsz, t, D_MODEL), F32),
        compiler_params=_params("parallel", "parallel"),
        name="mem_attention_ln2",
    )(x3d, mk, mv, w_mq, w_mo, g, b)


def _topk_rows(s, key, k, big):
    vals, keys = [], []
    for r in range(k):
        m = jnp.max(s, axis=0, keepdims=True)
        km = jnp.min(jnp.where(s == m, key, big), axis=0, keepdims=True)
        vals.append(m)
        keys.append(km)
        if r + 1 < k:
            s = jnp.where(key == km, -jnp.inf, s)
    return jnp.concatenate(vals, axis=0), jnp.concatenate(keys, axis=0)


HALF_EXPERTS = PEER_EXPERTS // 2
HALF_SHIFT = HALF_EXPERTS.bit_length() - 1
HIGH_MASK = -65536
CODE_SHIFT_BIT = 16


def _row_code(e):
    return ((e & (HALF_EXPERTS - 1)) * SUBLANES) | ((e >> HALF_SHIFT) << (CODE_SHIFT_BIT + 4))


def _route_kernel(x_ref, w_ref, ka_ref, kb_ref, idx_ref, g_ref, q_scr):
    tm = x_ref.shape[0]
    xb = x_ref[...].astype(BF16)
    for c in range(2 * PEER_HEADS):
        q_scr[c] = jnp.dot(xb, w_ref[:, c * PEER_HALF:(c + 1) * PEER_HALF],
                           preferred_element_type=F32).astype(BF16)
    key_io = lax.broadcasted_iota(I32, (PEER_NKEYS, tm), 0)
    n_cand = PEER_TOPK * PEER_TOPK
    nt = (((1,), (1,)), ((), ()))
    half = PEER_TOPK // 2
    cols = [PEER_TOPK, half] + [SUBLANES] * (half - 2)
    sub = lambda n: lax.broadcasted_iota(I32, (n, tm), 0)
    pos = jnp.concatenate([i * PEER_TOPK + sub(n) for i, n in enumerate(cols)] + [(half + sub(half)) * PEER_TOPK],
                          axis=0) * PEER_EXPERTS

    def head(h):
        sa = lax.dot_general(ka_ref[h], q_scr[2 * h], nt, preferred_element_type=F32)
        sb = lax.dot_general(kb_ref[h], q_scr[2 * h + 1], nt, preferred_element_type=F32)
        va, ia = _topk_rows(sa, key_io, PEER_TOPK, PEER_NKEYS)
        vb, ib = _topk_rows(sb, key_io, PEER_TOPK, PEER_NKEYS)
        cand = jnp.concatenate([va[i:i + 1] + vb[:n] for i, n in enumerate(cols)] + [va[half:] + vb[0:1]], axis=0)
        cidx = jnp.concatenate([ia[i:i + 1] * PEER_NKEYS + ib[:n] for i, n in enumerate(cols)]
                               + [ia[half:] * PEER_NKEYS + ib[0:1]], axis=0)
        top, tkey = _topk_rows(cand, pos + cidx, PEER_TOPK, n_cand * PEER_EXPERTS)
        e = jnp.exp(top - top[0:1])
        r0 = pl.multiple_of(h * PEER_TOPK, PEER_TOPK)
        idx_ref[0, pl.ds(r0, PEER_TOPK), :] = _row_code(tkey & (PEER_EXPERTS - 1))
        g_ref[0, pl.ds(r0, PEER_TOPK), :] = e / jnp.sum(e, axis=0, keepdims=True)

    def head_pair(i, carry):
        head(2 * i)
        head(2 * i + 1)
        return carry

    lax.fori_loop(0, PEER_HEADS // 2, head_pair, 0)


def _peer_route(x2d, w_pq, keys_a, keys_b, tm):
    m = x2d.shape[0]
    nt = m // tm
    kspec = pl.BlockSpec((PEER_HEADS, PEER_NKEYS, PEER_HALF), lambda i: (0, 0, 0))
    ospec = pl.BlockSpec((1, PEER_PICKS, tm), lambda i: (i, 0, 0))
    return pl.pallas_call(
        _route_kernel,
        grid=(nt,),
        in_specs=[pl.BlockSpec((tm, D_MODEL), lambda i: (i, 0)),
                  pl.BlockSpec((D_MODEL, 2 * PEER_HEADS * PEER_HALF), lambda i: (0, 0)),
                  kspec, kspec],
        out_specs=[ospec, ospec],
        out_shape=[jax.ShapeDtypeStruct((nt, PEER_PICKS, tm), I32),
                   jax.ShapeDtypeStruct((nt, PEER_PICKS, tm), F32)],
        scratch_shapes=[pltpu.VMEM((2 * PEER_HEADS, tm, PEER_HALF), BF16)],
        compiler_params=_params("parallel"),
        name="peer_route",
    )(x2d, w_pq, keys_a, keys_b)


def _pack_table(t):
    bits = lax.bitcast_convert_type(t.astype(BF16), jnp.uint16).astype(jnp.uint32)
    word = (bits[:HALF_EXPERTS] << 16) | bits[HALF_EXPERTS:]
    return lax.bitcast_convert_type(word, I32).reshape(HALF_EXPERTS * SUBLANES, LANES)


def _table_spec():
    return pl.BlockSpec((HALF_EXPERTS * SUBLANES, LANES), lambda i: (0, 0), pipeline_mode=pl.Buffered(1))


def _table_row(tab_ref, code):
    off = pl.multiple_of(code & ((1 << CODE_SHIFT_BIT) - 1), SUBLANES)
    row = tab_ref[pl.ds(off, SUBLANES), :]
    return pltpu.bitcast((row << (code >> CODE_SHIFT_BIT)) & HIGH_MASK, F32)


BIT_REVERSED = (0, 4, 2, 6, 1, 5, 3, 7)


def _sublane_sums(prods, sub_io):
    def merge(a, b, h):
        low = (sub_io & h) == 0
        return jnp.where(low, a, pltpu.roll(b, h, 0)) + jnp.where(low, pltpu.roll(a, SUBLANES - h, 0), b)

    p = [prods[BIT_REVERSED[k]] for k in range(SUBLANES)]
    t = [merge(p[2 * k], p[2 * k + 1], 4) for k in range(4)]
    u = [merge(t[2 * k], t[2 * k + 1], 2) for k in range(2)]
    return merge(u[0], u[1], 1)


REDUCE_UNROLL = 8
OFFSET_MASK = (1 << CODE_SHIFT_BIT) - SUBLANES


def _peer_in_kernel(idx_ref, x_ref, code_ref, g_ref, tab_ref, w_ref, part_ref):
    tm = x_ref.shape[0]
    sub_io = lax.broadcasted_iota(I32, (SUBLANES, LANES), 0)
    tok_io = lax.broadcasted_iota(I32, (PEER_PICKS, tm), 1)

    def token(t, carry):
        x = x_ref[t]
        for g8 in range(PEER_PICKS // SUBLANES):
            prods = [x * _table_row(tab_ref, idx_ref[0, t, g8 * SUBLANES + s]) for s in range(SUBLANES)]
            part_ref[t, g8 * SUBLANES:(g8 + 1) * SUBLANES, :] = _sublane_sums(prods, sub_io)
        return carry

    lax.fori_loop(0, tm, token, 0)

    def reduce(i, h_t):
        for k in range(REDUCE_UNROLL):
            t = i * REDUCE_UNROLL + k
            col = jnp.sum(part_ref[t], axis=1, keepdims=True)
            h_t = jnp.where(tok_io == t, col, h_t)
        return h_t

    h_t = lax.fori_loop(0, tm // REDUCE_UNROLL, reduce, jnp.zeros((PEER_PICKS, tm), F32))
    gelu = 0.5 * h_t * (1.0 + lax.erf(h_t * (2.0 ** -0.5)))
    w_bits = pltpu.bitcast((gelu * g_ref[0]).astype(BF16).astype(F32), I32)
    code = code_ref[0]
    w_ref[0] = w_bits | (code & OFFSET_MASK) | (code >> (CODE_SHIFT_BIT + 4))


def _peer_in(idx_t, x3, code, gate, table, tm):
    nt = idx_t.shape[0]
    tspec = pl.BlockSpec((1, PEER_PICKS, tm), lambda i: (i, 0, 0))
    return pl.pallas_call(
        _peer_in_kernel,
        grid=(nt,),
        in_specs=[pl.BlockSpec((1, tm, PEER_PICKS), lambda i: (i, 0, 0), memory_space=pltpu.SMEM),
                  pl.BlockSpec((tm, SUBLANES, LANES), lambda i: (i, 0, 0)),
                  tspec, tspec,
                  _table_spec()],
        out_specs=tspec,
        out_shape=jax.ShapeDtypeStruct((nt, PEER_PICKS, tm), I32),
        scratch_shapes=[pltpu.VMEM((tm, PEER_PICKS, LANES), F32)],
        compiler_params=_params("arbitrary"),
        name="peer_expert_in",
    )(idx_t, x3, code, gate, table)


def _peer_out_kernel(word_ref, x_ref, g_ref, b_ref, tab_ref, y_ref):
    tm = x_ref.shape[0]

    def token(t, carry):
        acc = DN_ALPHA * x_ref[t]
        for p in range(PEER_PICKS):
            word = word_ref[0, t, p]
            off = pl.multiple_of(word & OFFSET_MASK, SUBLANES)
            row = tab_ref[pl.ds(off, SUBLANES), :]
            wv = jnp.full((SUBLANES, LANES), word, I32)
            val = pltpu.bitcast((row << ((wv & 1) << 4)) & HIGH_MASK, F32)
            acc = acc + pltpu.bitcast(wv & HIGH_MASK, F32) * val
        y_ref[t] = acc
        return carry

    lax.fori_loop(0, tm, token, 0)
    r = y_ref[...]
    n = float(D_MODEL)
    mu = jnp.sum(jnp.sum(r, axis=2, keepdims=True), axis=1, keepdims=True) / n
    d = r - mu
    var = jnp.sum(jnp.sum(d * d, axis=2, keepdims=True), axis=1, keepdims=True) / n
    y_ref[...] = d * lax.rsqrt(var + LN_EPS) * g_ref[...] + b_ref[...]


def _peer_out(words_t, x3, g3, b3, table, tm):
    nt = words_t.shape[0]
    m = x3.shape[0]
    sspec = pl.BlockSpec((1, tm, PEER_PICKS), lambda i: (i, 0, 0), memory_space=pltpu.SMEM)
    vec = pl.BlockSpec((1, SUBLANES, LANES), lambda i: (0, 0, 0))
    xspec = pl.BlockSpec((tm, SUBLANES, LANES), lambda i: (i, 0, 0))
    return pl.pallas_call(
        _peer_out_kernel,
        grid=(nt,),
        in_specs=[sspec, xspec, vec, vec, _table_spec()],
        out_specs=xspec,
        out_shape=jax.ShapeDtypeStruct((m, SUBLANES, LANES), F32),
        compiler_params=_params("arbitrary"),
        name="peer_expert_out",
    )(words_t, x3, g3, b3, table)


def _pick_tile(n, pref):
    t = pref
    while n % t:
        t //= 2
    return t


def _layer(x, past, mem_k, mem_v, wts):
    b, t, _ = x.shape
    m = b * t
    x2d = x.reshape(m, D_MODEL)
    tm = _pick_tile(m, 256)

    (qsb, ksb, vsb, qfx, kfx, vfx, ksbb, vsbb, kfxb, vfxb, lf) = _in_projection(
        x2d, wts["w_in_main"], wts["w_in_f"], wts["b_f"], tm)
    state = tuple(a.reshape(b, t, N_HEADS, HEAD_DIM) for a in (ksb, vsb, kfx, vfx)) + (lf.reshape(b, t, N_HEADS),)

    r3 = lambda a: a.reshape(b, t, GROUP_WIDTH)
    if past is None:
        p = 0
        k_sb, v_sb, k_fx, v_fx = r3(ksbb), r3(vsbb), r3(kfxb), r3(vfxb)
        lf_all = lf.reshape(b, t, N_HEADS)
    else:
        p = past[0].shape[1]
        pad = (-(p + t)) % FOX_KEY_BLOCK

        def cat(c, new):
            parts = [c.reshape(b, p, GROUP_WIDTH).astype(BF16), r3(new)]
            if pad:
                parts.append(jnp.zeros((b, pad, GROUP_WIDTH), BF16))
            return jnp.concatenate(parts, axis=1)

        k_sb, v_sb, k_fx, v_fx = cat(past[0], ksbb), cat(past[1], vsbb), cat(past[2], kfxb), cat(past[3], vfxb)
        parts = [past[4].astype(F32), lf.reshape(b, t, N_HEADS)]
        if pad:
            parts.append(jnp.zeros((b, pad, N_HEADS), F32))
        lf_all = jnp.concatenate(parts, axis=1)

    c_all = _forget_cumsum(lf_all)
    lk = c_all.shape[1]
    cq = c_all[:, p:p + t]
    ck = c_all.transpose(0, 2, 1).reshape(b, N_HEADS // 2, 2, lk // FOX_KEY_BLOCK, FOX_KEY_BLOCK)
    ck = ck.transpose(0, 1, 3, 2, 4)

    tq = min(QUERY_BLOCK, t)
    assert FOX_KEY_BLOCK % tq == 0 and t % tq == 0 and p % tq == 0 and lk % FOX_KEY_BLOCK == 0
    o_sb = _stick_breaking_attention(r3(qsb), k_sb, v_sb, tq, p)
    o_fx = _forgetting_attention(r3(qfx), k_fx, v_fx, cq, ck, tq, p)

    x1 = _mix_out(o_sb.reshape(m, GROUP_WIDTH), o_fx.reshape(m, GROUP_WIDTH), x2d,
                  wts["w_gn"], wts["w_out"], wts["ln1_g"], wts["ln1_b"], tm)
    x2 = _mem_attention(x1.reshape(b, t, D_MODEL), mem_k, mem_v, wts["w_mq"], wts["w_mo"],
                        wts["ln2_g"], wts["ln2_b"], _pick_tile(t, 256))
    x2d2 = x2.reshape(m, D_MODEL)

    tr = LANES
    idx, gate = _peer_route(x2d2, wts["w_pq"], wts["keys_a"], wts["keys_b"], tr)
    x3 = x2d2.reshape(m, SUBLANES, LANES)
    words = _peer_in(idx.transpose(0, 2, 1), x3, idx, gate, wts["table_u"], tr)
    y = _peer_out(words.transpose(0, 2, 1), x3, wts["ln3_g3"], wts["ln3_b3"], wts["table_v"], tr)
    return y.reshape(b, t, D_MODEL), state


def kernel(x_prompt, x_sample, mem_prompt, cache_sb_k, cache_sb_v, cache_fox_k, cache_fox_v, cache_fox_logf,
           cache_mem_k, cache_mem_v, w_in, b_f, w_gn, w_out, ln1_g, ln1_b, w_mq, w_mk, w_mv, w_mo, ln2_g, ln2_b,
           w_pq, peer_keys_a, peer_keys_b, peer_u, peer_v, ln3_g, ln3_b):
    depth = w_in.shape[0]
    hp, hs = x_prompt, x_sample
    bp = x_prompt.shape[0]
    n_mem = mem_prompt.shape[1]
    mix_cols = 6 * GROUP_WIDTH
    outs_p = [[] for _ in range(7)]
    outs_s = [[] for _ in range(5)]
    row = lambda a: a.reshape(1, D_MODEL)
    for l in range(depth):
        wts = {
            "w_in_main": w_in[l][:, :mix_cols].astype(BF16),
            "w_in_f": jnp.pad(w_in[l][:, mix_cols:], ((0, 0), (0, LANES - N_HEADS))).astype(BF16),
            "b_f": jnp.pad(b_f[l], (0, LANES - N_HEADS)).reshape(1, LANES),
            "w_gn": row(w_gn[l]), "w_out": w_out[l].astype(BF16),
            "ln1_g": row(ln1_g[l]), "ln1_b": row(ln1_b[l]),
            "w_mq": w_mq[l].astype(BF16), "w_mo": w_mo[l].astype(BF16),
            "ln2_g": row(ln2_g[l]), "ln2_b": row(ln2_b[l]),
            "w_pq": w_pq[l].astype(BF16),
            "keys_a": peer_keys_a[l].astype(BF16), "keys_b": peer_keys_b[l].astype(BF16),
            "table_u": _pack_table(peer_u[l]), "table_v": _pack_table(peer_v[l]),
            "ln3_g3": ln3_g[l].reshape(1, SUBLANES, LANES), "ln3_b3": ln3_b[l].reshape(1, SUBLANES, LANES),
        }
        mem2d = mem_prompt.reshape(bp * n_mem, D_MODEL)
        mk_p, mv_p = _mem_kv(mem2d, w_mk[l].astype(BF16), w_mv[l].astype(BF16), _pick_tile(bp * n_mem, 512))
        mk_p = mk_p.reshape(bp, n_mem, D_MODEL)
        mv_p = mv_p.reshape(bp, n_mem, D_MODEL)
        hp, st_p = _layer(hp, None, mk_p, mv_p, wts)
        bs = x_sample.shape[0]
        past = (cache_sb_k[l], cache_sb_v[l], cache_fox_k[l], cache_fox_v[l], cache_fox_logf[l])
        hs, st_s = _layer(hs, past, cache_mem_k[l].reshape(bs, -1, D_MODEL), cache_mem_v[l].reshape(bs, -1, D_MODEL), wts)
        for i in range(5):
            outs_p[i].append(st_p[i])
            outs_s[i].append(st_s[i])
        outs_p[5].append(mk_p.reshape(bp, n_mem, MEM_HEADS, MEM_HEAD_DIM))
        outs_p[6].append(mv_p.reshape(bp, n_mem, MEM_HEADS, MEM_HEAD_DIM))
    stack = lambda xs: jnp.stack(xs)
    return (hp, hs) + tuple(stack(o) for o in outs_p) + tuple(stack(o) for o in outs_s)
```

```python
import functools

import jax
import jax.numpy as jnp
from jax import lax
from jax.experimental import pallas as pl
from jax.experimental.pallas import tpu as pltpu

F32 = jnp.float32
BF16 = jnp.bfloat16
I32 = jnp.int32

D_MODEL = 1024
HEAD_DIM = 64
N_HEADS = 8
GROUP_WIDTH = N_HEADS * HEAD_DIM
MEM_HEADS = 4
MEM_HEAD_DIM = D_MODEL // MEM_HEADS
PEER_HEADS = 8
PEER_NKEYS = 128
PEER_TOPK = 16
PEER_HALF = 128
PEER_PICKS = PEER_HEADS * PEER_TOPK
PEER_EXPERTS = PEER_NKEYS * PEER_NKEYS
DN_ALPHA = 2.0 ** 0.25
LN_EPS = 1e-5
GN_EPS = 1e-6

LANES = 128
SUBLANES = 8
KEY_BLOCK = 128
QUERY_BLOCK = 256
FOX_TILE = 256 * 256
VMEM_LIMIT = 56 * 1024 * 1024


def _params(*sem):
    return pltpu.CompilerParams(dimension_semantics=sem, vmem_limit_bytes=VMEM_LIMIT)


def _log_sigmoid(x):
    return jnp.minimum(x, 0.0) - jnp.log1p(jnp.exp(-jnp.abs(x)))


def _layer_norm_rows(r, g, b):
    mu = jnp.mean(r, axis=-1, keepdims=True)
    d = r - mu
    var = jnp.mean(d * d, axis=-1, keepdims=True)
    return d * lax.rsqrt(var + LN_EPS) * g + b


def _inproj_kernel(x_ref, w_ref, wf_ref, bf_ref,
                   qsb_ref, ksb_ref, vsb_ref, qfx_ref, kfx_ref, vfx_ref,
                   ksbb_ref, vsbb_ref, kfxb_ref, vfxb_ref, lf_ref):
    xb = x_ref[...].astype(BF16)

    def proj(j):
        return jnp.dot(xb, w_ref[:, j * GROUP_WIDTH:(j + 1) * GROUP_WIDTH], preferred_element_type=F32)

    scale = HEAD_DIM ** -0.5
    qsb_ref[...] = (proj(0) * scale).astype(BF16)
    k = proj(1)
    ksb_ref[...] = k
    ksbb_ref[...] = k.astype(BF16)
    v = proj(2)
    vsb_ref[...] = v
    vsbb_ref[...] = v.astype(BF16)
    qfx_ref[...] = (proj(3) * scale).astype(BF16)
    k = proj(4)
    kfx_ref[...] = k
    kfxb_ref[...] = k.astype(BF16)
    v = proj(5)
    vfx_ref[...] = v
    vfxb_ref[...] = v.astype(BF16)
    f = jnp.dot(xb, wf_ref[...], preferred_element_type=F32) + bf_ref[...]
    lf_ref[...] = _log_sigmoid(f)[:, :N_HEADS]


def _in_projection(x2d, w_main, w_f, b_f, tm):
    m = x2d.shape[0]
    f32o = jax.ShapeDtypeStruct((m, GROUP_WIDTH), F32)
    bf16o = jax.ShapeDtypeStruct((m, GROUP_WIDTH), BF16)
    blk = pl.BlockSpec((tm, GROUP_WIDTH), lambda i: (i, 0))
    return pl.pallas_call(
        _inproj_kernel,
        grid=(m // tm,),
        in_specs=[
            pl.BlockSpec((tm, D_MODEL), lambda i: (i, 0)),
            pl.BlockSpec((D_MODEL, 6 * GROUP_WIDTH), lambda i: (0, 0)),
            pl.BlockSpec((D_MODEL, LANES), lambda i: (0, 0)),
            pl.BlockSpec((1, LANES), lambda i: (0, 0)),
        ],
        out_specs=[blk] * 10 + [pl.BlockSpec((tm, N_HEADS), lambda i: (i, 0))],
        out_shape=[bf16o, f32o, f32o, bf16o, f32o, f32o, bf16o, bf16o, bf16o, bf16o,
                   jax.ShapeDtypeStruct((m, N_HEADS), F32)],
        compiler_params=_params("parallel"),
        name="in_projection",
    )(x2d, w_main, w_f, b_f)


def _cumsum_kernel(lf_ref, tri_ref, c_ref):
    l, cols = lf_ref.shape

    def chunk(i, carry):
        r0 = pl.multiple_of(i * KEY_BLOCK, KEY_BLOCK)
        v = lf_ref[pl.ds(r0, KEY_BLOCK), :]
        hi = v.astype(BF16)
        r1 = v - hi.astype(F32)
        mid = r1.astype(BF16)
        lo = (r1 - mid.astype(F32)).astype(BF16)
        parts = jnp.concatenate([hi, mid, lo], axis=1)
        s = jnp.dot(tri_ref[...], parts, preferred_element_type=F32)
        c = s[:, :cols] + s[:, cols:2 * cols] + s[:, 2 * cols:] + carry
        c_ref[pl.ds(r0, KEY_BLOCK), :] = c
        return c[KEY_BLOCK - 1:, :]

    lax.fori_loop(0, l // KEY_BLOCK, chunk, jnp.zeros((1, cols), F32))


def _forget_cumsum(lf):
    b, l, _ = lf.shape
    cols = b * N_HEADS
    r = lax.broadcasted_iota(I32, (KEY_BLOCK, KEY_BLOCK), 0)
    c = lax.broadcasted_iota(I32, (KEY_BLOCK, KEY_BLOCK), 1)
    tri = (c <= r).astype(BF16)
    return pl.pallas_call(
        _cumsum_kernel,
        grid=(1,),
        in_specs=[pl.BlockSpec((l, cols), lambda i: (0, 0)),
                  pl.BlockSpec((KEY_BLOCK, KEY_BLOCK), lambda i: (0, 0))],
        out_specs=pl.BlockSpec((l, cols), lambda i: (0, 0)),
        out_shape=jax.ShapeDtypeStruct((l, cols), F32),
        compiler_params=_params("arbitrary"),
        name="forget_cumsum",
    )(lf.transpose(1, 0, 2).reshape(l, cols), tri)


def _head_masks(width):
    lane = lax.broadcasted_iota(I32, (1, width), 1)
    return lane < HEAD_DIM


def _sb_kernel(q_ref, k_ref, v_ref, tri_ref, o_ref, *, tq, past):
    i = pl.program_id(2)
    q = q_ref[0]
    first = _head_masks(LANES)
    zero_q = jnp.zeros_like(q)
    qh = (jnp.where(first, q, zero_q), jnp.where(first, zero_q, q))
    q_pos0 = past + i * tq
    diag = q_pos0 // KEY_BLOCK
    n_diag = max(1, tq // KEY_BLOCK)
    tri = tri_ref[...]

    def block(j, run, acc, masked):
        k0 = pl.multiple_of(j * KEY_BLOCK, KEY_BLOCK)
        kb = k_ref[0, pl.ds(k0, KEY_BLOCK), :]
        vb = v_ref[0, pl.ds(k0, KEY_BLOCK), :]
        if masked:
            kpos = k0 + lax.broadcasted_iota(I32, (tq, KEY_BLOCK), 1)
            qpos = q_pos0 + lax.broadcasted_iota(I32, (tq, KEY_BLOCK), 0)
            mask = kpos < qpos
        ws = []
        new_run = []
        for h in range(2):
            z = lax.dot_general(qh[h], kb, (((1,), (1,)), ((), ())), preferred_element_type=F32)
            sp = jnp.maximum(z, 0.0) + jnp.log(1.0 + jnp.exp(-jnp.abs(z)))
            log_beta = z - sp
            if masked:
                sp = jnp.where(mask, sp, 0.0)
            hi = sp.astype(BF16)
            lo = (sp - hi.astype(F32)).astype(BF16)
            c = jnp.dot(jnp.concatenate([hi, lo], axis=1), tri, preferred_element_type=F32)
            w = jnp.exp(log_beta + c[:, :KEY_BLOCK] + run[h])
            if masked:
                w = jnp.where(mask, w, 0.0)
            new_run.append(run[h] + c[:, KEY_BLOCK:])
            ws.append(w.astype(BF16))
        zero_v = jnp.zeros_like(vb)
        v2 = jnp.concatenate([jnp.where(first, vb, zero_v), jnp.where(first, zero_v, vb)], axis=0)
        acc = acc + jnp.dot(jnp.concatenate(ws, axis=1), v2, preferred_element_type=F32)
        return tuple(new_run), acc

    def alive(run):
        return (jnp.max(jnp.maximum(run[0], run[1])) > EXP_UNDERFLOW).astype(I32)

    zeros = jnp.zeros((tq, KEY_BLOCK), F32)
    run, acc = (zeros, zeros), jnp.zeros((tq, LANES), F32)
    for d in reversed(range(n_diag)):
        run, acc = block(diag + d, run, acc, True)

    def cond(state):
        it, live, _, _ = state
        return (it < diag) & (live > 0)

    def body(state):
        it, _, run, acc = state
        run, acc = block(diag - 1 - it, run, acc, False)
        return it + 1, alive(run), run, acc

    _, _, _, acc = lax.while_loop(cond, body, (jnp.int32(0), alive(run), run, acc))
    o_ref[0] = acc


EXP_UNDERFLOW = -105.0


def _cumsum_rhs():
    r = lax.broadcasted_iota(I32, (2 * KEY_BLOCK, 2 * KEY_BLOCK), 0) % KEY_BLOCK
    c = lax.broadcasted_iota(I32, (2 * KEY_BLOCK, 2 * KEY_BLOCK), 1)
    return -((c >= KEY_BLOCK) | (r > c)).astype(BF16)


def _stick_breaking_attention(q, k, v, tq, past):
    b, t, _ = q.shape
    lk = k.shape[1]
    pairs = GROUP_WIDTH // LANES
    return pl.pallas_call(
        functools.partial(_sb_kernel, tq=tq, past=past),
        grid=(b, pairs, t // tq),
        in_specs=[
            pl.BlockSpec((1, tq, LANES), lambda bi, hp, i: (bi, i, hp)),
            pl.BlockSpec((1, lk, LANES), lambda bi, hp, i: (bi, 0, hp)),
            pl.BlockSpec((1, lk, LANES), lambda bi, hp, i: (bi, 0, hp)),
            pl.BlockSpec((2 * KEY_BLOCK, 2 * KEY_BLOCK), lambda bi, hp, i: (0, 0)),
        ],
        out_specs=pl.BlockSpec((1, tq, LANES), lambda bi, hp, i: (bi, i, hp)),
        out_shape=jax.ShapeDtypeStruct((b, t, GROUP_WIDTH), F32),
        compiler_params=_params("parallel", "parallel", "arbitrary"),
        name="stick_breaking_attention",
    )(q, k, v, _cumsum_rhs())


def _fox_kernel(q_ref, k_ref, v_ref, cq_ref, ck_ref, o_ref, *, tq, past):
    i = pl.program_id(2)
    hp = pl.program_id(1)
    q = q_ref[0]
    first = _head_masks(LANES)
    zero_q = jnp.zeros_like(q)
    qh = (jnp.where(first, q, zero_q), jnp.where(first, zero_q, q))
    q_pos0 = past + i * tq
    kblk = ck_ref.shape[-1]
    diag = q_pos0 // kblk
    cq_all = cq_ref[0]
    head_lane = lax.broadcasted_iota(I32, (1, N_HEADS), 1)
    cq = [jnp.sum(jnp.where(head_lane == 2 * hp + h, cq_all, 0.0), axis=1, keepdims=True) for h in range(2)]

    def block(j, carry, masked):
        ms, ls, acc = carry
        k0 = pl.multiple_of(j * kblk, kblk)
        kb = k_ref[0, pl.ds(k0, kblk), :]
        vb = v_ref[0, pl.ds(k0, kblk), :]
        ck = ck_ref[0, 0, j]
        if masked:
            kpos = k0 + lax.broadcasted_iota(I32, (tq, kblk), 1)
            qpos = q_pos0 + lax.broadcasted_iota(I32, (tq, kblk), 0)
            mask = kpos <= qpos
        ps, new_m, new_l, scales = [], [], [], []
        for h in range(2):
            s = lax.dot_general(qh[h], kb, (((1,), (1,)), ((), ())), preferred_element_type=F32)
            s = s + cq[h] - ck[h:h + 1, :]
            if masked:
                s = jnp.where(mask, s, -jnp.inf)
            m = jnp.maximum(ms[h], jnp.max(s, axis=1, keepdims=True))
            p = jnp.exp(s - m)
            a = jnp.exp(ms[h] - m)
            new_m.append(m)
            new_l.append(a * ls[h] + jnp.sum(p, axis=1, keepdims=True))
            scales.append(a)
            ps.append(p.astype(BF16))
        zero_v = jnp.zeros_like(vb)
        v2 = jnp.concatenate([jnp.where(first, vb, zero_v), jnp.where(first, zero_v, vb)], axis=0)
        pv = jnp.dot(jnp.concatenate(ps, axis=1), v2, preferred_element_type=F32)
        acc = acc * jnp.where(first, scales[0], scales[1]) + pv
        return (tuple(new_m), tuple(new_l), acc)

    neg = jnp.full((tq, 1), -jnp.inf, F32)
    zero = jnp.zeros((tq, 1), F32)
    carry = block(diag, ((neg, neg), (zero, zero), jnp.zeros((tq, LANES), F32)), True)

    def body(it, carry):
        return block(diag - 1 - it, carry, False)

    _, ls, acc = lax.fori_loop(0, diag, body, carry)
    o_ref[0] = acc / jnp.where(first, ls[0], ls[1])


def _forgetting_attention(q, k, v, cq, ck, tq, past):
    b, t, _ = q.shape
    lk = k.shape[1]
    pairs = GROUP_WIDTH // LANES
    return pl.pallas_call(
        functools.partial(_fox_kernel, tq=tq, past=past),
        grid=(b, pairs, t // tq),
        in_specs=[
            pl.BlockSpec((1, tq, LANES), lambda bi, hp, i: (bi, i, hp)),
            pl.BlockSpec((1, lk, LANES), lambda bi, hp, i: (bi, 0, hp)),
            pl.BlockSpec((1, lk, LANES), lambda bi, hp, i: (bi, 0, hp)),
            pl.BlockSpec((1, tq, N_HEADS), lambda bi, hp, i: (bi, i, 0)),
            pl.BlockSpec((1, 1) + ck.shape[2:], lambda bi, hp, i: (bi, hp, 0, 0, 0)),
        ],
        out_specs=pl.BlockSpec((1, tq, LANES), lambda bi, hp, i: (bi, i, hp)),
        out_shape=jax.ShapeDtypeStruct((b, t, GROUP_WIDTH), F32),
        compiler_params=_params("parallel", "parallel", "arbitrary"),
        name="forgetting_attention",
    )(q, k, v, cq, ck)


def _mixout_kernel(osb_ref, ofx_ref, x_ref, gn_ref, w_ref, g_ref, b_ref, y_ref):
    def rms(o, g):
        return o * lax.rsqrt(jnp.mean(o * o, axis=-1, keepdims=True) + GN_EPS) * g

    gn = gn_ref[...]
    o = jnp.concatenate([rms(osb_ref[...], gn[:, :GROUP_WIDTH]), rms(ofx_ref[...], gn[:, GROUP_WIDTH:])], axis=1)
    mix = jnp.dot(o.astype(BF16), w_ref[...], preferred_element_type=F32)
    y_ref[...] = _layer_norm_rows(DN_ALPHA * x_ref[...] + mix, g_ref[...], b_ref[...])


def _mix_out(osb, ofx, x2d, w_gn, w_out, g, b, tm):
    m = x2d.shape[0]
    row = pl.BlockSpec((1, D_MODEL), lambda i: (0, 0))
    return pl.pallas_call(
        _mixout_kernel,
        grid=(m // tm,),
        in_specs=[
            pl.BlockSpec((tm, GROUP_WIDTH), lambda i: (i, 0)),
            pl.BlockSpec((tm, GROUP_WIDTH), lambda i: (i, 0)),
            pl.BlockSpec((tm, D_MODEL), lambda i: (i, 0)),
            row,
            pl.BlockSpec((D_MODEL, D_MODEL), lambda i: (0, 0)),
            row, row,
        ],
        out_specs=pl.BlockSpec((tm, D_MODEL), lambda i: (i, 0)),
        out_shape=jax.ShapeDtypeStruct((m, D_MODEL), F32),
        compiler_params=_params("parallel"),
        name="mix_out_ln1",
    )(osb, ofx, x2d, w_gn, w_out, g, b)


def _memkv_kernel(m_ref, wk_ref, wv_ref, k_ref, v_ref):
    mb = m_ref[...].astype(BF16)
    k_ref[...] = jnp.dot(mb, wk_ref[...], preferred_element_type=F32)
    v_ref[...] = jnp.dot(mb, wv_ref[...], preferred_element_type=F32)


def _mem_kv(mem2d, w_mk, w_mv, tm):
    m = mem2d.shape[0]
    wspec = pl.BlockSpec((D_MODEL, D_MODEL), lambda i: (0, 0))
    blk = pl.BlockSpec((tm, D_MODEL), lambda i: (i, 0))
    out = jax.ShapeDtypeStruct((m, D_MODEL), F32)
    return pl.pallas_call(
        _memkv_kernel,
        grid=(m // tm,),
        in_specs=[blk, wspec, wspec],
        out_specs=[blk, blk],
        out_shape=[out, out],
        compiler_params=_params("parallel"),
        name="mem_kv",
    )(mem2d, w_mk, w_mv)


def _memattn_kernel(x_ref, mk_ref, mv_ref, wq_ref, wo_ref, g_ref, b_ref, y_ref):
    x = x_ref[0]
    q = jnp.dot(x.astype(BF16), wq_ref[...], preferred_element_type=F32)
    qb = (q * (MEM_HEAD_DIM ** -0.5)).astype(BF16)
    mk = mk_ref[0].astype(BF16)
    mv = mv_ref[0].astype(BF16)
    outs = []
    for h in range(MEM_HEADS):
        sl = slice(h * MEM_HEAD_DIM, (h + 1) * MEM_HEAD_DIM)
        s = lax.dot_general(qb[:, sl], mk[:, sl], (((1,), (1,)), ((), ())), preferred_element_type=F32)
        p = jnp.exp(s - jnp.max(s, axis=1, keepdims=True))
        o = jnp.dot(p.astype(BF16), mv[:, sl], preferred_element_type=F32)
        outs.append(o / jnp.sum(p, axis=1, keepdims=True))
    o = jnp.concatenate(outs, axis=1).astype(BF16)
    att = jnp.dot(o, wo_ref[...], preferred_element_type=F32)
    y_ref[0] = _layer_norm_rows(DN_ALPHA * x + att, g_ref[...], b_ref[...])


def _mem_attention(x3d, mk, mv, w_mq, w_mo, g, b, tm):
    bsz, t, _ = x3d.shape
    n_mem = mk.shape[1]
    row = pl.BlockSpec((1, D_MODEL), lambda bi, i: (0, 0))
    wspec = pl.BlockSpec((D_MODEL, D_MODEL), lambda bi, i: (0, 0))
    return pl.pallas_call(
        _memattn_kernel,
        grid=(bsz, t // tm),
        in_specs=[
            pl.BlockSpec((1, tm, D_MODEL), lambda bi, i: (bi, i, 0)),
            pl.BlockSpec((1, n_mem, D_MODEL), lambda bi, i: (bi, 0, 0)),
            pl.BlockSpec((1, n_mem, D_MODEL), lambda bi, i: (bi, 0, 0)),
            wspec, wspec, row, row,
        ],
        out_specs=pl.BlockSpec((1, tm, D_MODEL), lambda bi, i: (bi, i, 0)),
        out_shape=jax.ShapeDtypeStruct((bsz, t, D_MODEL), F32),
        compiler_params=_params("parallel", "parallel"),
        name="mem_attention_ln2",
    )(x3d, mk, mv, w_mq, w_mo, g, b)


def _topk_rows(s, key, k, big):
    vals, keys = [], []
    for r in range(k):
        m = jnp.max(s, axis=0, keepdims=True)
        km = jnp.min(jnp.where(s == m, key, big), axis=0, keepdims=True)
        vals.append(m)
        keys.append(km)
        if r + 1 < k:
            s = jnp.where(key == km, -jnp.inf, s)
    return jnp.concatenate(vals, axis=0), jnp.concatenate(keys, axis=0)


HALF_EXPERTS = PEER_EXPERTS // 2
HALF_SHIFT = HALF_EXPERTS.bit_length() - 1
HIGH_MASK = -65536
CODE_SHIFT_BIT = 16
ROUTE_HEADS_PER_STEP = 4


def _row_code(e):
    return ((e & (HALF_EXPERTS - 1)) * SUBLANES) | ((e >> HALF_SHIFT) << (CODE_SHIFT_BIT + 4))


def _route_kernel(x_ref, w_ref, ka_ref, kb_ref, idx_ref, g_ref, q_scr):
    tm = x_ref.shape[0]
    xb = x_ref[...].astype(BF16)
    for c in range(2 * PEER_HEADS):
        q_scr[c] = jnp.dot(xb, w_ref[:, c * PEER_HALF:(c + 1) * PEER_HALF],
                           preferred_element_type=F32).astype(BF16)
    key_io = lax.broadcasted_iota(I32, (PEER_NKEYS, tm), 0)
    n_cand = PEER_TOPK * PEER_TOPK
    nt = (((1,), (1,)), ((), ()))
    half = PEER_TOPK // 2
    cols = [PEER_TOPK, half] + [SUBLANES] * (half - 2)
    sub = lambda n: lax.broadcasted_iota(I32, (n, tm), 0)
    pos = jnp.concatenate([i * PEER_TOPK + sub(n) for i, n in enumerate(cols)] + [(half + sub(half)) * PEER_TOPK],
                          axis=0) * PEER_EXPERTS

    def head(h):
        sa = lax.dot_general(ka_ref[h], q_scr[2 * h], nt, preferred_element_type=F32)
        sb = lax.dot_general(kb_ref[h], q_scr[2 * h + 1], nt, preferred_element_type=F32)
        va, ia = _topk_rows(sa, key_io, PEER_TOPK, PEER_NKEYS)
        vb, ib = _topk_rows(sb, key_io, PEER_TOPK, PEER_NKEYS)
        cand = jnp.concatenate([va[i:i + 1] + vb[:n] for i, n in enumerate(cols)] + [va[half:] + vb[0:1]], axis=0)
        cidx = jnp.concatenate([ia[i:i + 1] * PEER_NKEYS + ib[:n] for i, n in enumerate(cols)]
                               + [ia[half:] * PEER_NKEYS + ib[0:1]], axis=0)
        top, tkey = _topk_rows(cand, pos + cidx, PEER_TOPK, n_cand * PEER_EXPERTS)
        e = jnp.exp(top - top[0:1])
        r0 = pl.multiple_of(h * PEER_TOPK, PEER_TOPK)
        idx_ref[0, pl.ds(r0, PEER_TOPK), :] = _row_code(tkey & (PEER_EXPERTS - 1))
        g_ref[0, pl.ds(r0, PEER_TOPK), :] = e / jnp.sum(e, axis=0, keepdims=True)

    def head_group(i, carry):
        for k in range(ROUTE_HEADS_PER_STEP):
            head(ROUTE_HEADS_PER_STEP * i + k)
        return carry

    lax.fori_loop(0, PEER_HEADS // ROUTE_HEADS_PER_STEP, head_group, 0)


def _peer_route(x2d, w_pq, keys_a, keys_b, tm):
    m = x2d.shape[0]
    nt = m // tm
    kspec = pl.BlockSpec((PEER_HEADS, PEER_NKEYS, PEER_HALF), lambda i: (0, 0, 0))
    ospec = pl.BlockSpec((1, PEER_PICKS, tm), lambda i: (i, 0, 0))
    return pl.pallas_call(
        _route_kernel,
        grid=(nt,),
        in_specs=[pl.BlockSpec((tm, D_MODEL), lambda i: (i, 0)),
                  pl.BlockSpec((D_MODEL, 2 * PEER_HEADS * PEER_HALF), lambda i: (0, 0)),
                  kspec, kspec],
        out_specs=[ospec, ospec],
        out_shape=[jax.ShapeDtypeStruct((nt, PEER_PICKS, tm), I32),
                   jax.ShapeDtypeStruct((nt, PEER_PICKS, tm), F32)],
        scratch_shapes=[pltpu.VMEM((2 * PEER_HEADS, tm, PEER_HALF), BF16)],
        compiler_params=_params("parallel"),
        name="peer_route",
    )(x2d, w_pq, keys_a, keys_b)


def _pack_table(t):
    bits = lax.bitcast_convert_type(t.astype(BF16), jnp.uint16).astype(jnp.uint32)
    word = (bits[:HALF_EXPERTS] << 16) | bits[HALF_EXPERTS:]
    return lax.bitcast_convert_type(word, I32).reshape(HALF_EXPERTS * SUBLANES, LANES)


def _table_spec():
    return pl.BlockSpec((HALF_EXPERTS * SUBLANES, LANES), lambda i: (0, 0), pipeline_mode=pl.Buffered(1))


def _table_row(tab_ref, code):
    off = pl.multiple_of(code & ((1 << CODE_SHIFT_BIT) - 1), SUBLANES)
    row = tab_ref[pl.ds(off, SUBLANES), :]
    return pltpu.bitcast((row << (code >> CODE_SHIFT_BIT)) & HIGH_MASK, F32)


def _rows_to_tiles(x_ref, tiles_ref):
    for s in range(SUBLANES):
        tiles_ref[:, s, :] = x_ref[:, s * LANES:(s + 1) * LANES]


def _tiles_to_rows(tiles_ref, y_ref):
    for s in range(SUBLANES):
        y_ref[:, s * LANES:(s + 1) * LANES] = tiles_ref[:, s, :]


BIT_REVERSED = (0, 4, 2, 6, 1, 5, 3, 7)


def _sublane_sums(prods, sub_io):
    def merge(a, b, h):
        low = (sub_io & h) == 0
        return jnp.where(low, a, pltpu.roll(b, h, 0)) + jnp.where(low, pltpu.roll(a, SUBLANES - h, 0), b)

    p = [prods[BIT_REVERSED[k]] for k in range(SUBLANES)]
    t = [merge(p[2 * k], p[2 * k + 1], 4) for k in range(4)]
    u = [merge(t[2 * k], t[2 * k + 1], 2) for k in range(2)]
    return merge(u[0], u[1], 1)


REDUCE_UNROLL = 8
OFFSET_MASK = (1 << CODE_SHIFT_BIT) - SUBLANES


def _peer_in_kernel(idx_ref, x_ref, code_ref, g_ref, tab_ref, w_ref, part_ref, xt_ref):
    tm = x_ref.shape[0]
    sub_io = lax.broadcasted_iota(I32, (SUBLANES, LANES), 0)
    tok_io = lax.broadcasted_iota(I32, (PEER_PICKS, tm), 1)
    _rows_to_tiles(x_ref, xt_ref)

    def token(t, carry):
        x = xt_ref[t]
        for g8 in range(PEER_PICKS // SUBLANES):
            prods = [x * _table_row(tab_ref, idx_ref[0, t, g8 * SUBLANES + s]) for s in range(SUBLANES)]
            part_ref[t, g8 * SUBLANES:(g8 + 1) * SUBLANES, :] = _sublane_sums(prods, sub_io)
        return carry

    lax.fori_loop(0, tm, token, 0)

    def reduce(i, h_t):
        for k in range(REDUCE_UNROLL):
            t = i * REDUCE_UNROLL + k
            col = jnp.sum(part_ref[t], axis=1, keepdims=True)
            h_t = jnp.where(tok_io == t, col, h_t)
        return h_t

    h_t = lax.fori_loop(0, tm // REDUCE_UNROLL, reduce, jnp.zeros((PEER_PICKS, tm), F32))
    gelu = 0.5 * h_t * (1.0 + lax.erf(h_t * (2.0 ** -0.5)))
    w_bits = pltpu.bitcast((gelu * g_ref[0]).astype(BF16).astype(F32), I32)
    code = code_ref[0]
    w_ref[0] = w_bits | (code & OFFSET_MASK) | (code >> (CODE_SHIFT_BIT + 4))


def _peer_in(idx_t, x2d, code, gate, table, tm):
    nt = idx_t.shape[0]
    assert tm == LANES
    tspec = pl.BlockSpec((1, PEER_PICKS, tm), lambda i: (i, 0, 0))
    return pl.pallas_call(
        _peer_in_kernel,
        grid=(nt,),
        in_specs=[pl.BlockSpec((1, tm, PEER_PICKS), lambda i: (i, 0, 0), memory_space=pltpu.SMEM),
                  pl.BlockSpec((tm, D_MODEL), lambda i: (i, 0)),
                  tspec, tspec,
                  _table_spec()],
        out_specs=tspec,
        out_shape=jax.ShapeDtypeStruct((nt, PEER_PICKS, tm), I32),
        scratch_shapes=[pltpu.VMEM((tm, PEER_PICKS, LANES), F32), pltpu.VMEM((tm, SUBLANES, LANES), F32)],
        compiler_params=_params("arbitrary"),
        name="peer_expert_in",
    )(idx_t, x2d, code, gate, table)


def _peer_out_kernel(word_ref, x_ref, g_ref, b_ref, tab_ref, y_ref, xt_ref):
    tm = x_ref.shape[0]
    _rows_to_tiles(x_ref, xt_ref)

    def token(t, carry):
        acc = DN_ALPHA * xt_ref[t]
        for p in range(PEER_PICKS):
            word = word_ref[0, t, p]
            off = pl.multiple_of(word & OFFSET_MASK, SUBLANES)
            row = tab_ref[pl.ds(off, SUBLANES), :]
            wv = jnp.full((SUBLANES, LANES), word, I32)
            val = pltpu.bitcast((row << ((wv & 1) << 4)) & HIGH_MASK, F32)
            acc = acc + pltpu.bitcast(wv & HIGH_MASK, F32) * val
        xt_ref[t] = acc
        return carry

    lax.fori_loop(0, tm, token, 0)
    _tiles_to_rows(xt_ref, y_ref)
    y_ref[...] = _layer_norm_rows(y_ref[...], g_ref[...], b_ref[...])


def _peer_out(words_t, x2d, g, b, table, tm):
    nt = words_t.shape[0]
    m = x2d.shape[0]
    sspec = pl.BlockSpec((1, tm, PEER_PICKS), lambda i: (i, 0, 0), memory_space=pltpu.SMEM)
    vec = pl.BlockSpec((1, D_MODEL), lambda i: (0, 0))
    xspec = pl.BlockSpec((tm, D_MODEL), lambda i: (i, 0))
    return pl.pallas_call(
        _peer_out_kernel,
        grid=(nt,),
        in_specs=[sspec, xspec, vec, vec, _table_spec()],
        out_specs=xspec,
        out_shape=jax.ShapeDtypeStruct((m, D_MODEL), F32),
        scratch_shapes=[pltpu.VMEM((tm, SUBLANES, LANES), F32)],
        compiler_params=_params("arbitrary"),
        name="peer_expert_out",
    )(words_t, x2d, g, b, table)


def _pick_tile(n, pref):
    t = pref
    while n % t:
        t //= 2
    return t


def _layer(x, past, mem_k, mem_v, wts):
    b, t, _ = x.shape
    m = b * t
    x2d = x.reshape(m, D_MODEL)
    tm = _pick_tile(m, 256)

    (qsb, ksb, vsb, qfx, kfx, vfx, ksbb, vsbb, kfxb, vfxb, lf) = _in_projection(
        x2d, wts["w_in_main"], wts["w_in_f"], wts["b_f"], tm)
    state = tuple(a.reshape(b, t, N_HEADS, HEAD_DIM) for a in (ksb, vsb, kfx, vfx)) + (lf.reshape(b, t, N_HEADS),)

    r3 = lambda a: a.reshape(b, t, GROUP_WIDTH)
    tq = min(QUERY_BLOCK, t)
    kblk = FOX_TILE // tq
    if past is None:
        p = 0
        k_sb, v_sb, k_fx, v_fx = r3(ksbb), r3(vsbb), r3(kfxb), r3(vfxb)
        lf_all = lf.reshape(b, t, N_HEADS)
    else:
        p = past[0].shape[1]
        pad = (-(p + t)) % kblk

        def cat(c, new):
            parts = [c.reshape(b, p, GROUP_WIDTH).astype(BF16), r3(new)]
            if pad:
                parts.append(jnp.zeros((b, pad, GROUP_WIDTH), BF16))
            return jnp.concatenate(parts, axis=1)

        k_sb, v_sb, k_fx, v_fx = cat(past[0], ksbb), cat(past[1], vsbb), cat(past[2], kfxb), cat(past[3], vfxb)
        parts = [past[4].astype(F32), lf.reshape(b, t, N_HEADS)]
        if pad:
            parts.append(jnp.zeros((b, pad, N_HEADS), F32))
        lf_all = jnp.concatenate(parts, axis=1)

    c_all = _forget_cumsum(lf_all)
    lk = c_all.shape[0]
    cq = c_all[p:p + t].reshape(t, b, N_HEADS).transpose(1, 0, 2)
    ck = c_all.T.reshape(b, N_HEADS // 2, 2, lk // kblk, kblk).transpose(0, 1, 3, 2, 4)

    assert kblk % tq == 0 and t % tq == 0 and p % tq == 0 and lk % kblk == 0 and kblk % KEY_BLOCK == 0
    o_sb = _stick_breaking_attention(r3(qsb), k_sb, v_sb, tq, p)
    o_fx = _forgetting_attention(r3(qfx), k_fx, v_fx, cq, ck, tq, p)

    x1 = _mix_out(o_sb.reshape(m, GROUP_WIDTH), o_fx.reshape(m, GROUP_WIDTH), x2d,
                  wts["w_gn"], wts["w_out"], wts["ln1_g"], wts["ln1_b"], tm)
    x2 = _mem_attention(x1.reshape(b, t, D_MODEL), mem_k, mem_v, wts["w_mq"], wts["w_mo"],
                        wts["ln2_g"], wts["ln2_b"], _pick_tile(t, 256))
    x2d2 = x2.reshape(m, D_MODEL)

    tr = LANES
    idx, gate = _peer_route(x2d2, wts["w_pq"], wts["keys_a"], wts["keys_b"], tr)
    words = _peer_in(idx.transpose(0, 2, 1), x2d2, idx, gate, wts["table_u"], tr)
    y = _peer_out(words.transpose(0, 2, 1), x2d2, wts["ln3_g"], wts["ln3_b"], wts["table_v"], tr)
    return y.reshape(b, t, D_MODEL), state


def kernel(x_prompt, x_sample, mem_prompt, cache_sb_k, cache_sb_v, cache_fox_k, cache_fox_v, cache_fox_logf,
           cache_mem_k, cache_mem_v, w_in, b_f, w_gn, w_out, ln1_g, ln1_b, w_mq, w_mk, w_mv, w_mo, ln2_g, ln2_b,
           w_pq, peer_keys_a, peer_keys_b, peer_u, peer_v, ln3_g, ln3_b):
    depth = w_in.shape[0]
    hp, hs = x_prompt, x_sample
    bp = x_prompt.shape[0]
    n_mem = mem_prompt.shape[1]
    mix_cols = 6 * GROUP_WIDTH
    outs_p = [[] for _ in range(7)]
    outs_s = [[] for _ in range(5)]
    row = lambda a: a.reshape(1, D_MODEL)
    for l in range(depth):
        wts = {
            "w_in_main": w_in[l][:, :mix_cols].astype(BF16),
            "w_in_f": jnp.pad(w_in[l][:, mix_cols:], ((0, 0), (0, LANES - N_HEADS))).astype(BF16),
            "b_f": jnp.pad(b_f[l], (0, LANES - N_HEADS)).reshape(1, LANES),
            "w_gn": row(w_gn[l]), "w_out": w_out[l].astype(BF16),
            "ln1_g": row(ln1_g[l]), "ln1_b": row(ln1_b[l]),
            "w_mq": w_mq[l].astype(BF16), "w_mo": w_mo[l].astype(BF16),
            "ln2_g": row(ln2_g[l]), "ln2_b": row(ln2_b[l]),
            "w_pq": w_pq[l].astype(BF16),
            "keys_a": peer_keys_a[l].astype(BF16), "keys_b": peer_keys_b[l].astype(BF16),
            "table_u": _pack_table(peer_u[l]), "table_v": _pack_table(peer_v[l]),
            "ln3_g": row(ln3_g[l]), "ln3_b": row(ln3_b[l]),
        }
        mem2d = mem_prompt.reshape(bp * n_mem, D_MODEL)
        mk_p, mv_p = _mem_kv(mem2d, w_mk[l].astype(BF16), w_mv[l].astype(BF16), _pick_tile(bp * n_mem, 512))
        mk_p = mk_p.reshape(bp, n_mem, D_MODEL)
        mv_p = mv_p.reshape(bp, n_mem, D_MODEL)
        hp, st_p = _layer(hp, None, mk_p, mv_p, wts)
        bs = x_sample.shape[0]
        past = (cache_sb_k[l], cache_sb_v[l], cache_fox_k[l], cache_fox_v[l], cache_fox_logf[l])
        hs, st_s = _layer(hs, past, cache_mem_k[l].reshape(bs, -1, D_MODEL), cache_mem_v[l].reshape(bs, -1, D_MODEL), wts)
        for i in range(5):
            outs_p[i].append(st_p[i])
            outs_s[i].append(st_s[i])
        outs_p[5].append(mk_p.reshape(bp, n_mem, MEM_HEADS, MEM_HEAD_DIM))
        outs_p[6].append(mv_p.reshape(bp, n_mem, MEM_HEADS, MEM_HEAD_DIM))
    stack = lambda xs: jnp.stack(xs)
    return (hp, hs) + tuple(stack(o) for o in outs_p) + tuple(stack(o) for o in outs_s)
```

```python
import functools

import jax
import jax.numpy as jnp
from jax import lax
from jax.experimental import pallas as pl
from jax.experimental.pallas import tpu as pltpu

F32 = jnp.float32
BF16 = jnp.bfloat16
I32 = jnp.int32

D_MODEL = 1024
HEAD_DIM = 64
N_HEADS = 8
GROUP_WIDTH = N_HEADS * HEAD_DIM
MEM_HEADS = 4
MEM_HEAD_DIM = D_MODEL // MEM_HEADS
PEER_HEADS = 8
PEER_NKEYS = 128
PEER_TOPK = 16
PEER_HALF = 128
PEER_PICKS = PEER_HEADS * PEER_TOPK
PEER_EXPERTS = PEER_NKEYS * PEER_NKEYS
DN_ALPHA = 2.0 ** 0.25
LN_EPS = 1e-5
GN_EPS = 1e-6

LANES = 128
SUBLANES = 8
KEY_BLOCK = 128
QUERY_BLOCK = 256
FOX_TILE = 256 * 256
VMEM_LIMIT = 56 * 1024 * 1024


def _params(*sem):
    return pltpu.CompilerParams(dimension_semantics=sem, vmem_limit_bytes=VMEM_LIMIT)


def _log_sigmoid(x):
    return jnp.minimum(x, 0.0) - jnp.log1p(jnp.exp(-jnp.abs(x)))


def _layer_norm_rows(r, g, b):
    mu = jnp.mean(r, axis=-1, keepdims=True)
    d = r - mu
    var = jnp.mean(d * d, axis=-1, keepdims=True)
    return d * lax.rsqrt(var + LN_EPS) * g + b


def _inproj_kernel(x_ref, w_ref, wf_ref, bf_ref,
                   qsb_ref, ksb_ref, vsb_ref, qfx_ref, kfx_ref, vfx_ref,
                   ksbb_ref, vsbb_ref, kfxb_ref, vfxb_ref, lf_ref):
    xb = x_ref[...].astype(BF16)

    def proj(j):
        return jnp.dot(xb, w_ref[:, j * GROUP_WIDTH:(j + 1) * GROUP_WIDTH], preferred_element_type=F32)

    scale = HEAD_DIM ** -0.5
    qsb_ref[...] = (proj(0) * scale).astype(BF16)
    k = proj(1)
    ksb_ref[...] = k
    ksbb_ref[...] = k.astype(BF16)
    v = proj(2)
    vsb_ref[...] = v
    vsbb_ref[...] = v.astype(BF16)
    qfx_ref[...] = (proj(3) * scale).astype(BF16)
    k = proj(4)
    kfx_ref[...] = k
    kfxb_ref[...] = k.astype(BF16)
    v = proj(5)
    vfx_ref[...] = v
    vfxb_ref[...] = v.astype(BF16)
    f = jnp.dot(xb, wf_ref[...], preferred_element_type=F32) + bf_ref[...]
    lf_ref[...] = _log_sigmoid(f)[:, :N_HEADS]


def _in_projection(x2d, w_main, w_f, b_f, tm):
    m = x2d.shape[0]
    f32o = jax.ShapeDtypeStruct((m, GROUP_WIDTH), F32)
    bf16o = jax.ShapeDtypeStruct((m, GROUP_WIDTH), BF16)
    blk = pl.BlockSpec((tm, GROUP_WIDTH), lambda i: (i, 0))
    return pl.pallas_call(
        _inproj_kernel,
        grid=(m // tm,),
        in_specs=[
            pl.BlockSpec((tm, D_MODEL), lambda i: (i, 0)),
            pl.BlockSpec((D_MODEL, 6 * GROUP_WIDTH), lambda i: (0, 0)),
            pl.BlockSpec((D_MODEL, LANES), lambda i: (0, 0)),
            pl.BlockSpec((1, LANES), lambda i: (0, 0)),
        ],
        out_specs=[blk] * 10 + [pl.BlockSpec((tm, N_HEADS), lambda i: (i, 0))],
        out_shape=[bf16o, f32o, f32o, bf16o, f32o, f32o, bf16o, bf16o, bf16o, bf16o,
                   jax.ShapeDtypeStruct((m, N_HEADS), F32)],
        compiler_params=_params("parallel"),
        name="in_projection",
    )(x2d, w_main, w_f, b_f)


def _cumsum_kernel(lf_ref, tri_ref, c_ref):
    l, cols = lf_ref.shape

    def chunk(i, carry):
        r0 = pl.multiple_of(i * KEY_BLOCK, KEY_BLOCK)
        v = lf_ref[pl.ds(r0, KEY_BLOCK), :]
        hi = v.astype(BF16)
        r1 = v - hi.astype(F32)
        mid = r1.astype(BF16)
        lo = (r1 - mid.astype(F32)).astype(BF16)
        parts = jnp.concatenate([hi, mid, lo], axis=1)
        s = jnp.dot(tri_ref[...], parts, preferred_element_type=F32)
        c = s[:, :cols] + s[:, cols:2 * cols] + s[:, 2 * cols:] + carry
        c_ref[pl.ds(r0, KEY_BLOCK), :] = c
        return c[KEY_BLOCK - 1:, :]

    lax.fori_loop(0, l // KEY_BLOCK, chunk, jnp.zeros((1, cols), F32))


def _forget_cumsum(lf):
    b, l, _ = lf.shape
    cols = b * N_HEADS
    r = lax.broadcasted_iota(I32, (KEY_BLOCK, KEY_BLOCK), 0)
    c = lax.broadcasted_iota(I32, (KEY_BLOCK, KEY_BLOCK), 1)
    tri = (c <= r).astype(BF16)
    return pl.pallas_call(
        _cumsum_kernel,
        grid=(1,),
        in_specs=[pl.BlockSpec((l, cols), lambda i: (0, 0)),
                  pl.BlockSpec((KEY_BLOCK, KEY_BLOCK), lambda i: (0, 0))],
        out_specs=pl.BlockSpec((l, cols), lambda i: (0, 0)),
        out_shape=jax.ShapeDtypeStruct((l, cols), F32),
        compiler_params=_params("arbitrary"),
        name="forget_cumsum",
    )(lf.transpose(1, 0, 2).reshape(l, cols), tri)


def _head_masks(width):
    lane = lax.broadcasted_iota(I32, (1, width), 1)
    return lane < HEAD_DIM


def _sb_kernel(q_ref, k_ref, v_ref, tri_ref, o_ref, *, tq, past):
    i = pl.program_id(2)
    q = q_ref[0]
    first = _head_masks(LANES)
    zero_q = jnp.zeros_like(q)
    qh = (jnp.where(first, q, zero_q), jnp.where(first, zero_q, q))
    q_pos0 = past + i * tq
    diag = q_pos0 // KEY_BLOCK
    n_diag = max(1, tq // KEY_BLOCK)
    tri = tri_ref[...]

    def block(j, run, acc, masked):
        k0 = pl.multiple_of(j * KEY_BLOCK, KEY_BLOCK)
        kb = k_ref[0, pl.ds(k0, KEY_BLOCK), :]
        vb = v_ref[0, pl.ds(k0, KEY_BLOCK), :]
        if masked:
            kpos = k0 + lax.broadcasted_iota(I32, (tq, KEY_BLOCK), 1)
            qpos = q_pos0 + lax.broadcasted_iota(I32, (tq, KEY_BLOCK), 0)
            mask = kpos < qpos
        ws = []
        new_run = []
        for h in range(2):
            z = lax.dot_general(qh[h], kb, (((1,), (1,)), ((), ())), preferred_element_type=F32)
            sp = jnp.maximum(z, 0.0) + jnp.log(1.0 + jnp.exp(-jnp.abs(z)))
            log_beta = z - sp
            if masked:
                sp = jnp.where(mask, sp, 0.0)
            hi = sp.astype(BF16)
            lo = (sp - hi.astype(F32)).astype(BF16)
            c = jnp.dot(jnp.concatenate([hi, lo], axis=1), tri, preferred_element_type=F32)
            w = jnp.exp(log_beta + c[:, :KEY_BLOCK] + run[h])
            if masked:
                w = jnp.where(mask, w, 0.0)
            new_run.append(run[h] + c[:, KEY_BLOCK:])
            ws.append(w.astype(BF16))
        zero_v = jnp.zeros_like(vb)
        v2 = jnp.concatenate([jnp.where(first, vb, zero_v), jnp.where(first, zero_v, vb)], axis=0)
        acc = acc + jnp.dot(jnp.concatenate(ws, axis=1), v2, preferred_element_type=F32)
        return tuple(new_run), acc

    def alive(run):
        return (jnp.max(jnp.maximum(run[0], run[1])) > EXP_UNDERFLOW).astype(I32)

    zeros = jnp.zeros((tq, KEY_BLOCK), F32)
    run, acc = (zeros, zeros), jnp.zeros((tq, LANES), F32)
    for d in reversed(range(n_diag)):
        run, acc = block(diag + d, run, acc, True)

    def cond(state):
        it, live, _, _ = state
        return (it < diag) & (live > 0)

    def body(state):
        it, _, run, acc = state
        run, acc = block(diag - 1 - it, run, acc, False)
        return it + 1, alive(run), run, acc

    _, _, _, acc = lax.while_loop(cond, body, (jnp.int32(0), alive(run), run, acc))
    o_ref[0] = acc


EXP_UNDERFLOW = -105.0


def _cumsum_rhs():
    r = lax.broadcasted_iota(I32, (2 * KEY_BLOCK, 2 * KEY_BLOCK), 0) % KEY_BLOCK
    c = lax.broadcasted_iota(I32, (2 * KEY_BLOCK, 2 * KEY_BLOCK), 1)
    return -((c >= KEY_BLOCK) | (r > c)).astype(BF16)


def _stick_breaking_attention(q, k, v, tq, past):
    b, t, _ = q.shape
    lk = k.shape[1]
    pairs = GROUP_WIDTH // LANES
    return pl.pallas_call(
        functools.partial(_sb_kernel, tq=tq, past=past),
        grid=(b, pairs, t // tq),
        in_specs=[
            pl.BlockSpec((1, tq, LANES), lambda bi, hp, i: (bi, i, hp)),
            pl.BlockSpec((1, lk, LANES), lambda bi, hp, i: (bi, 0, hp)),
            pl.BlockSpec((1, lk, LANES), lambda bi, hp, i: (bi, 0, hp)),
            pl.BlockSpec((2 * KEY_BLOCK, 2 * KEY_BLOCK), lambda bi, hp, i: (0, 0)),
        ],
        out_specs=pl.BlockSpec((1, tq, LANES), lambda bi, hp, i: (bi, i, hp)),
        out_shape=jax.ShapeDtypeStruct((b, t, GROUP_WIDTH), F32),
        compiler_params=_params("parallel", "parallel", "arbitrary"),
        name="stick_breaking_attention",
    )(q, k, v, _cumsum_rhs())


def _fox_kernel(q_ref, k_ref, v_ref, cq_ref, ck_ref, o_ref, *, tq, past):
    i = pl.program_id(2)
    hp = pl.program_id(1)
    q = q_ref[0]
    first = _head_masks(LANES)
    zero_q = jnp.zeros_like(q)
    qh = (jnp.where(first, q, zero_q), jnp.where(first, zero_q, q))
    q_pos0 = past + i * tq
    kblk = ck_ref.shape[-1]
    diag = q_pos0 // kblk
    cq_all = cq_ref[0]
    head_lane = lax.broadcasted_iota(I32, (1, N_HEADS), 1)
    cq = [jnp.sum(jnp.where(head_lane == 2 * hp + h, cq_all, 0.0), axis=1, keepdims=True) for h in range(2)]

    def block(j, carry, masked):
        ms, ls, acc = carry
        k0 = pl.multiple_of(j * kblk, kblk)
        kb = k_ref[0, pl.ds(k0, kblk), :]
        vb = v_ref[0, pl.ds(k0, kblk), :]
        ck = ck_ref[0, 0, j]
        if masked:
            kpos = k0 + lax.broadcasted_iota(I32, (tq, kblk), 1)
            qpos = q_pos0 + lax.broadcasted_iota(I32, (tq, kblk), 0)
            mask = kpos <= qpos
        ps, new_m, new_l, scales = [], [], [], []
        for h in range(2):
            s = lax.dot_general(qh[h], kb, (((1,), (1,)), ((), ())), preferred_element_type=F32)
            s = s + cq[h] - ck[h:h + 1, :]
            if masked:
                s = jnp.where(mask, s, -jnp.inf)
            m = jnp.maximum(ms[h], jnp.max(s, axis=1, keepdims=True))
            p = jnp.exp(s - m)
            a = jnp.exp(ms[h] - m)
            new_m.append(m)
            new_l.append(a * ls[h] + jnp.sum(p, axis=1, keepdims=True))
            scales.append(a)
            ps.append(p.astype(BF16))
        zero_v = jnp.zeros_like(vb)
        v2 = jnp.concatenate([jnp.where(first, vb, zero_v), jnp.where(first, zero_v, vb)], axis=0)
        pv = jnp.dot(jnp.concatenate(ps, axis=1), v2, preferred_element_type=F32)
        acc = acc * jnp.where(first, scales[0], scales[1]) + pv
        return (tuple(new_m), tuple(new_l), acc)

    neg = jnp.full((tq, 1), -jnp.inf, F32)
    zero = jnp.zeros((tq, 1), F32)
    carry = block(diag, ((neg, neg), (zero, zero), jnp.zeros((tq, LANES), F32)), True)

    def body(it, carry):
        return block(diag - 1 - it, carry, False)

    _, ls, acc = lax.fori_loop(0, diag, body, carry)
    o_ref[0] = acc / jnp.where(first, ls[0], ls[1])


def _forgetting_attention(q, k, v, cq, ck, tq, past):
    b, t, _ = q.shape
    lk = k.shape[1]
    pairs = GROUP_WIDTH // LANES
    return pl.pallas_call(
        functools.partial(_fox_kernel, tq=tq, past=past),
        grid=(b, pairs, t // tq),
        in_specs=[
            pl.BlockSpec((1, tq, LANES), lambda bi, hp, i: (bi, i, hp)),
            pl.BlockSpec((1, lk, LANES), lambda bi, hp, i: (bi, 0, hp)),
            pl.BlockSpec((1, lk, LANES), lambda bi, hp, i: (bi, 0, hp)),
            pl.BlockSpec((1, tq, N_HEADS), lambda bi, hp, i: (bi, i, 0)),
            pl.BlockSpec((1, 1) + ck.shape[2:], lambda bi, hp, i: (bi, hp, 0, 0, 0)),
        ],
        out_specs=pl.BlockSpec((1, tq, LANES), lambda bi, hp, i: (bi, i, hp)),
        out_shape=jax.ShapeDtypeStruct((b, t, GROUP_WIDTH), F32),
        compiler_params=_params("parallel", "parallel", "arbitrary"),
        name="forgetting_attention",
    )(q, k, v, cq, ck)


def _mixout_kernel(osb_ref, ofx_ref, x_ref, gn_ref, w_ref, g_ref, b_ref, y_ref):
    def rms(o, g):
        return o * lax.rsqrt(jnp.mean(o * o, axis=-1, keepdims=True) + GN_EPS) * g

    gn = gn_ref[...]
    o = jnp.concatenate([rms(osb_ref[...], gn[:, :GROUP_WIDTH]), rms(ofx_ref[...], gn[:, GROUP_WIDTH:])], axis=1)
    mix = jnp.dot(o.astype(BF16), w_ref[...], preferred_element_type=F32)
    y_ref[...] = _layer_norm_rows(DN_ALPHA * x_ref[...] + mix, g_ref[...], b_ref[...])


def _mix_out(osb, ofx, x2d, w_gn, w_out, g, b, tm):
    m = x2d.shape[0]
    row = pl.BlockSpec((1, D_MODEL), lambda i: (0, 0))
    return pl.pallas_call(
        _mixout_kernel,
        grid=(m // tm,),
        in_specs=[
            pl.BlockSpec((tm, GROUP_WIDTH), lambda i: (i, 0)),
            pl.BlockSpec((tm, GROUP_WIDTH), lambda i: (i, 0)),
            pl.BlockSpec((tm, D_MODEL), lambda i: (i, 0)),
            row,
            pl.BlockSpec((D_MODEL, D_MODEL), lambda i: (0, 0)),
            row, row,
        ],
        out_specs=pl.BlockSpec((tm, D_MODEL), lambda i: (i, 0)),
        out_shape=jax.ShapeDtypeStruct((m, D_MODEL), F32),
        compiler_params=_params("parallel"),
        name="mix_out_ln1",
    )(osb, ofx, x2d, w_gn, w_out, g, b)


def _memkv_kernel(m_ref, wk_ref, wv_ref, k_ref, v_ref):
    mb = m_ref[...].astype(BF16)
    k_ref[...] = jnp.dot(mb, wk_ref[...], preferred_element_type=F32)
    v_ref[...] = jnp.dot(mb, wv_ref[...], preferred_element_type=F32)


def _mem_kv(mem2d, w_mk, w_mv, tm):
    m = mem2d.shape[0]
    wspec = pl.BlockSpec((D_MODEL, D_MODEL), lambda i: (0, 0))
    blk = pl.BlockSpec((tm, D_MODEL), lambda i: (i, 0))
    out = jax.ShapeDtypeStruct((m, D_MODEL), F32)
    return pl.pallas_call(
        _memkv_kernel,
        grid=(m // tm,),
        in_specs=[blk, wspec, wspec],
        out_specs=[blk, blk],
        out_shape=[out, out],
        compiler_params=_params("parallel"),
        name="mem_kv",
    )(mem2d, w_mk, w_mv)


def _memattn_kernel(x_ref, mk_ref, mv_ref, wq_ref, wo_ref, g_ref, b_ref, y_ref):
    x = x_ref[0]
    q = jnp.dot(x.astype(BF16), wq_ref[...], preferred_element_type=F32)
    qb = (q * (MEM_HEAD_DIM ** -0.5)).astype(BF16)
    mk = mk_ref[0].astype(BF16)
    mv = mv_ref[0].astype(BF16)
    outs = []
    for h in range(MEM_HEADS):
        sl = slice(h * MEM_HEAD_DIM, (h + 1) * MEM_HEAD_DIM)
        s = lax.dot_general(qb[:, sl], mk[:, sl], (((1,), (1,)), ((), ())), preferred_element_type=F32)
        p = jnp.exp(s - jnp.max(s, axis=1, keepdims=True))
        o = jnp.dot(p.astype(BF16), mv[:, sl], preferred_element_type=F32)
        outs.append(o / jnp.sum(p, axis=1, keepdims=True))
    o = jnp.concatenate(outs, axis=1).astype(BF16)
    att = jnp.dot(o, wo_ref[...], preferred_element_type=F32)
    y_ref[0] = _layer_norm_rows(DN_ALPHA * x + att, g_ref[...], b_ref[...])


def _mem_attention(x3d, mk, mv, w_mq, w_mo, g, b, tm):
    bsz, t, _ = x3d.shape
    n_mem = mk.shape[1]
    row = pl.BlockSpec((1, D_MODEL), lambda bi, i: (0, 0))
    wspec = pl.BlockSpec((D_MODEL, D_MODEL), lambda bi, i: (0, 0))
    return pl.pallas_call(
        _memattn_kernel,
        grid=(bsz, t // tm),
        in_specs=[
            pl.BlockSpec((1, tm, D_MODEL), lambda bi, i: (bi, i, 0)),
            pl.BlockSpec((1, n_mem, D_MODEL), lambda bi, i: (bi, 0, 0)),
            pl.BlockSpec((1, n_mem, D_MODEL), lambda bi, i: (bi, 0, 0)),
            wspec, wspec, row, row,
        ],
        out_specs=pl.BlockSpec((1, tm, D_MODEL), lambda bi, i: (bi, i, 0)),
        out_shape=jax.ShapeDtypeStruct((bsz, t, D_MODEL), F32),
        compiler_params=_params("parallel", "parallel"),
        name="mem_attention_ln2",
    )(x3d, mk, mv, w_mq, w_mo, g, b)


def _topk_rows(s, key, k, big):
    vals, keys = [], []
    for r in range(k):
        m = jnp.max(s, axis=0, keepdims=True)
        km = jnp.min(jnp.where(s == m, key, big), axis=0, keepdims=True)
        vals.append(m)
        keys.append(km)
        if r + 1 < k:
            s = jnp.where(key == km, -jnp.inf, s)
    return jnp.concatenate(vals, axis=0), jnp.concatenate(keys, axis=0)


HALF_EXPERTS = PEER_EXPERTS // 2
HALF_SHIFT = HALF_EXPERTS.bit_length() - 1
HIGH_MASK = -65536
ROUTE_HEADS_PER_STEP = 4


def _route_kernel(x_ref, w_ref, ka_ref, kb_ref, off_ref, sh_ref, g_ref, q_scr):
    tm = x_ref.shape[0]
    xb = x_ref[...].astype(BF16)
    for c in range(2 * PEER_HEADS):
        q_scr[c] = jnp.dot(xb, w_ref[:, c * PEER_HALF:(c + 1) * PEER_HALF],
                           preferred_element_type=F32).astype(BF16)
    key_io = lax.broadcasted_iota(I32, (PEER_NKEYS, tm), 0)
    n_cand = PEER_TOPK * PEER_TOPK
    nt = (((1,), (1,)), ((), ()))
    half = PEER_TOPK // 2
    cols = [PEER_TOPK, half] + [SUBLANES] * (half - 2)
    sub = lambda n: lax.broadcasted_iota(I32, (n, tm), 0)
    pos = jnp.concatenate([i * PEER_TOPK + sub(n) for i, n in enumerate(cols)] + [(half + sub(half)) * PEER_TOPK],
                          axis=0) * PEER_EXPERTS

    def head(h):
        sa = lax.dot_general(ka_ref[h], q_scr[2 * h], nt, preferred_element_type=F32)
        sb = lax.dot_general(kb_ref[h], q_scr[2 * h + 1], nt, preferred_element_type=F32)
        va, ia = _topk_rows(sa, key_io, PEER_TOPK, PEER_NKEYS)
        vb, ib = _topk_rows(sb, key_io, PEER_TOPK, PEER_NKEYS)
        cand = jnp.concatenate([va[i:i + 1] + vb[:n] for i, n in enumerate(cols)] + [va[half:] + vb[0:1]], axis=0)
        cidx = jnp.concatenate([ia[i:i + 1] * PEER_NKEYS + ib[:n] for i, n in enumerate(cols)]
                               + [ia[half:] * PEER_NKEYS + ib[0:1]], axis=0)
        top, tkey = _topk_rows(cand, pos + cidx, PEER_TOPK, n_cand * PEER_EXPERTS)
        e = jnp.exp(top - top[0:1])
        r0 = pl.multiple_of(h * PEER_TOPK, PEER_TOPK)
        off_ref[0, pl.ds(r0, PEER_TOPK), :] = (tkey & (HALF_EXPERTS - 1)) * SUBLANES
        sh_ref[0, pl.ds(r0, PEER_TOPK), :] = ((tkey >> HALF_SHIFT) & 1) * 16
        g_ref[0, pl.ds(r0, PEER_TOPK), :] = e / jnp.sum(e, axis=0, keepdims=True)

    def head_group(i, carry):
        for k in range(ROUTE_HEADS_PER_STEP):
            head(ROUTE_HEADS_PER_STEP * i + k)
        return carry

    lax.fori_loop(0, PEER_HEADS // ROUTE_HEADS_PER_STEP, head_group, 0)


def _peer_route(x2d, w_pq, keys_a, keys_b, tm):
    m = x2d.shape[0]
    nt = m // tm
    kspec = pl.BlockSpec((PEER_HEADS, PEER_NKEYS, PEER_HALF), lambda i: (0, 0, 0))
    ospec = pl.BlockSpec((1, PEER_PICKS, tm), lambda i: (i, 0, 0))
    return pl.pallas_call(
        _route_kernel,
        grid=(nt,),
        in_specs=[pl.BlockSpec((tm, D_MODEL), lambda i: (i, 0)),
                  pl.BlockSpec((D_MODEL, 2 * PEER_HEADS * PEER_HALF), lambda i: (0, 0)),
                  kspec, kspec],
        out_specs=[ospec, ospec, ospec],
        out_shape=[jax.ShapeDtypeStruct((nt, PEER_PICKS, tm), I32),
                   jax.ShapeDtypeStruct((nt, PEER_PICKS, tm), I32),
                   jax.ShapeDtypeStruct((nt, PEER_PICKS, tm), F32)],
        scratch_shapes=[pltpu.VMEM((2 * PEER_HEADS, tm, PEER_HALF), BF16)],
        compiler_params=_params("parallel"),
        name="peer_route",
    )(x2d, w_pq, keys_a, keys_b)


def _pack_table(t):
    bits = lax.bitcast_convert_type(t.astype(BF16), jnp.uint16).astype(jnp.uint32)
    word = (bits[:HALF_EXPERTS] << 16) | bits[HALF_EXPERTS:]
    return lax.bitcast_convert_type(word, I32).reshape(HALF_EXPERTS * SUBLANES, LANES)


def _table_spec():
    return pl.BlockSpec((HALF_EXPERTS * SUBLANES, LANES), lambda i: (0, 0), pipeline_mode=pl.Buffered(1))


def _table_row(tab_ref, off, shift):
    row = tab_ref[pl.ds(pl.multiple_of(off, SUBLANES), SUBLANES), :]
    return pltpu.bitcast((row << shift) & HIGH_MASK, F32)


def _rows_to_tiles(x_ref, tiles_ref):
    for s in range(SUBLANES):
        tiles_ref[:, s, :] = x_ref[:, s * LANES:(s + 1) * LANES]


def _tiles_to_rows(tiles_ref, y_ref):
    for s in range(SUBLANES):
        y_ref[:, s * LANES:(s + 1) * LANES] = tiles_ref[:, s, :]


BIT_REVERSED = (0, 4, 2, 6, 1, 5, 3, 7)


def _sublane_sums(prods, sub_io):
    def merge(a, b, h):
        low = (sub_io & h) == 0
        if 2 * h == SUBLANES:
            return jnp.where(low, a, b) + pltpu.roll(jnp.where(low, b, a), h, 0)
        return jnp.where(low, a, pltpu.roll(b, h, 0)) + jnp.where(low, pltpu.roll(a, SUBLANES - h, 0), b)

    p = [prods[BIT_REVERSED[k]] for k in range(SUBLANES)]
    t = [merge(p[2 * k], p[2 * k + 1], 4) for k in range(4)]
    u = [merge(t[2 * k], t[2 * k + 1], 2) for k in range(2)]
    return merge(u[0], u[1], 1)


REDUCE_UNROLL = 8
OFFSET_MASK = (HALF_EXPERTS - 1) * SUBLANES


def _peer_in_kernel(off_s, sh_s, x_ref, off_ref, sh_ref, g_ref, tab_ref, w_ref, part_ref, xt_ref):
    tm = x_ref.shape[0]
    sub_io = lax.broadcasted_iota(I32, (SUBLANES, LANES), 0)
    tok_io = lax.broadcasted_iota(I32, (PEER_PICKS, tm), 1)
    _rows_to_tiles(x_ref, xt_ref)

    def token(t, carry):
        x = xt_ref[t]
        for g8 in range(PEER_PICKS // SUBLANES):
            picks = [g8 * SUBLANES + s for s in range(SUBLANES)]
            prods = [x * _table_row(tab_ref, off_s[0, t, p], sh_s[0, t, p]) for p in picks]
            part_ref[t, g8 * SUBLANES:(g8 + 1) * SUBLANES, :] = _sublane_sums(prods, sub_io)
        return carry

    lax.fori_loop(0, tm, token, 0)

    def reduce(i, h_t):
        for k in range(REDUCE_UNROLL):
            t = i * REDUCE_UNROLL + k
            col = jnp.sum(part_ref[t], axis=1, keepdims=True)
            h_t = jnp.where(tok_io == t, col, h_t)
        return h_t

    h_t = lax.fori_loop(0, tm // REDUCE_UNROLL, reduce, jnp.zeros((PEER_PICKS, tm), F32))
    gelu = 0.5 * h_t * (1.0 + lax.erf(h_t * (2.0 ** -0.5)))
    w_bits = pltpu.bitcast((gelu * g_ref[0]).astype(BF16).astype(F32), I32)
    w_ref[0] = w_bits | off_ref[0] | (sh_ref[0] >> 4)


def _peer_in(off, sh, x2d, gate, table, tm):
    nt = off.shape[0]
    assert tm == LANES
    tspec = pl.BlockSpec((1, PEER_PICKS, tm), lambda i: (i, 0, 0))
    sspec = pl.BlockSpec((1, tm, PEER_PICKS), lambda i: (i, 0, 0), memory_space=pltpu.SMEM)
    return pl.pallas_call(
        _peer_in_kernel,
        grid=(nt,),
        in_specs=[sspec, sspec,
                  pl.BlockSpec((tm, D_MODEL), lambda i: (i, 0)),
                  tspec, tspec, tspec,
                  _table_spec()],
        out_specs=tspec,
        out_shape=jax.ShapeDtypeStruct((nt, PEER_PICKS, tm), I32),
        scratch_shapes=[pltpu.VMEM((tm, PEER_PICKS, LANES), F32), pltpu.VMEM((tm, SUBLANES, LANES), F32)],
        compiler_params=_params("arbitrary"),
        name="peer_expert_in",
    )(off.transpose(0, 2, 1), sh.transpose(0, 2, 1), x2d, off, sh, gate, table)


def _peer_out_kernel(word_ref, x_ref, g_ref, b_ref, tab_ref, y_ref, xt_ref):
    tm = x_ref.shape[0]
    _rows_to_tiles(x_ref, xt_ref)

    def token(t, carry):
        acc = DN_ALPHA * xt_ref[t]
        for p in range(PEER_PICKS):
            word = word_ref[0, t, p]
            off = pl.multiple_of(word & OFFSET_MASK, SUBLANES)
            row = tab_ref[pl.ds(off, SUBLANES), :]
            wv = jnp.full((SUBLANES, LANES), word, I32)
            val = pltpu.bitcast((row << ((wv & 1) << 4)) & HIGH_MASK, F32)
            acc = acc + pltpu.bitcast(wv & HIGH_MASK, F32) * val
        xt_ref[t] = acc
        return carry

    lax.fori_loop(0, tm, token, 0)
    _tiles_to_rows(xt_ref, y_ref)
    y_ref[...] = _layer_norm_rows(y_ref[...], g_ref[...], b_ref[...])


def _peer_out(words_t, x2d, g, b, table, tm):
    nt = words_t.shape[0]
    m = x2d.shape[0]
    sspec = pl.BlockSpec((1, tm, PEER_PICKS), lambda i: (i, 0, 0), memory_space=pltpu.SMEM)
    vec = pl.BlockSpec((1, D_MODEL), lambda i: (0, 0))
    xspec = pl.BlockSpec((tm, D_MODEL), lambda i: (i, 0))
    return pl.pallas_call(
        _peer_out_kernel,
        grid=(nt,),
        in_specs=[sspec, xspec, vec, vec, _table_spec()],
        out_specs=xspec,
        out_shape=jax.ShapeDtypeStruct((m, D_MODEL), F32),
        scratch_shapes=[pltpu.VMEM((tm, SUBLANES, LANES), F32)],
        compiler_params=_params("arbitrary"),
        name="peer_expert_out",
    )(words_t, x2d, g, b, table)


def _pick_tile(n, pref):
    t = pref
    while n % t:
        t //= 2
    return t


def _layer(x, past, mem_k, mem_v, wts):
    b, t, _ = x.shape
    m = b * t
    x2d = x.reshape(m, D_MODEL)
    tm = _pick_tile(m, 256)

    (qsb, ksb, vsb, qfx, kfx, vfx, ksbb, vsbb, kfxb, vfxb, lf) = _in_projection(
        x2d, wts["w_in_main"], wts["w_in_f"], wts["b_f"], tm)
    state = tuple(a.reshape(b, t, N_HEADS, HEAD_DIM) for a in (ksb, vsb, kfx, vfx)) + (lf.reshape(b, t, N_HEADS),)

    r3 = lambda a: a.reshape(b, t, GROUP_WIDTH)
    tq = min(QUERY_BLOCK, t)
    kblk = FOX_TILE // tq
    if past is None:
        p = 0
        k_sb, v_sb, k_fx, v_fx = r3(ksbb), r3(vsbb), r3(kfxb), r3(vfxb)
        lf_all = lf.reshape(b, t, N_HEADS)
    else:
        p = past[0].shape[1]
        pad = (-(p + t)) % kblk

        def cat(c, new):
            parts = [c.reshape(b, p, GROUP_WIDTH).astype(BF16), r3(new)]
            if pad:
                parts.append(jnp.zeros((b, pad, GROUP_WIDTH), BF16))
            return jnp.concatenate(parts, axis=1)

        k_sb, v_sb, k_fx, v_fx = cat(past[0], ksbb), cat(past[1], vsbb), cat(past[2], kfxb), cat(past[3], vfxb)
        parts = [past[4].astype(F32), lf.reshape(b, t, N_HEADS)]
        if pad:
            parts.append(jnp.zeros((b, pad, N_HEADS), F32))
        lf_all = jnp.concatenate(parts, axis=1)

    c_all = _forget_cumsum(lf_all)
    lk = c_all.shape[0]
    cq = c_all[p:p + t].reshape(t, b, N_HEADS).transpose(1, 0, 2)
    ck = c_all.T.reshape(b, N_HEADS // 2, 2, lk // kblk, kblk).transpose(0, 1, 3, 2, 4)

    assert kblk % tq == 0 and t % tq == 0 and p % tq == 0 and lk % kblk == 0 and kblk % KEY_BLOCK == 0
    o_sb = _stick_breaking_attention(r3(qsb), k_sb, v_sb, tq, p)
    o_fx = _forgetting_attention(r3(qfx), k_fx, v_fx, cq, ck, tq, p)

    x1 = _mix_out(o_sb.reshape(m, GROUP_WIDTH), o_fx.reshape(m, GROUP_WIDTH), x2d,
                  wts["w_gn"], wts["w_out"], wts["ln1_g"], wts["ln1_b"], tm)
    x2 = _mem_attention(x1.reshape(b, t, D_MODEL), mem_k, mem_v, wts["w_mq"], wts["w_mo"],
                        wts["ln2_g"], wts["ln2_b"], _pick_tile(t, 256))
    x2d2 = x2.reshape(m, D_MODEL)

    tr = LANES
    off, sh, gate = _peer_route(x2d2, wts["w_pq"], wts["keys_a"], wts["keys_b"], tr)
    words = _peer_in(off, sh, x2d2, gate, wts["table_u"], tr)
    y = _peer_out(words.transpose(0, 2, 1), x2d2, wts["ln3_g"], wts["ln3_b"], wts["table_v"], tr)
    return y.reshape(b, t, D_MODEL), state


def kernel(x_prompt, x_sample, mem_prompt, cache_sb_k, cache_sb_v, cache_fox_k, cache_fox_v, cache_fox_logf,
           cache_mem_k, cache_mem_v, w_in, b_f, w_gn, w_out, ln1_g, ln1_b, w_mq, w_mk, w_mv, w_mo, ln2_g, ln2_b,
           w_pq, peer_keys_a, peer_keys_b, peer_u, peer_v, ln3_g, ln3_b):
    depth = w_in.shape[0]
    hp, hs = x_prompt, x_sample
    bp = x_prompt.shape[0]
    n_mem = mem_prompt.shape[1]
    mix_cols = 6 * GROUP_WIDTH
    outs_p = [[] for _ in range(7)]
    outs_s = [[] for _ in range(5)]
    row = lambda a: a.reshape(1, D_MODEL)
    for l in range(depth):
        wts = {
            "w_in_main": w_in[l][:, :mix_cols].astype(BF16),
            "w_in_f": jnp.pad(w_in[l][:, mix_cols:], ((0, 0), (0, LANES - N_HEADS))).astype(BF16),
            "b_f": jnp.pad(b_f[l], (0, LANES - N_HEADS)).reshape(1, LANES),
            "w_gn": row(w_gn[l]), "w_out": w_out[l].astype(BF16),
            "ln1_g": row(ln1_g[l]), "ln1_b": row(ln1_b[l]),
            "w_mq": w_mq[l].astype(BF16), "w_mo": w_mo[l].astype(BF16),
            "ln2_g": row(ln2_g[l]), "ln2_b": row(ln2_b[l]),
            "w_pq": w_pq[l].astype(BF16),
            "keys_a": peer_keys_a[l].astype(BF16), "keys_b": peer_keys_b[l].astype(BF16),
            "table_u": _pack_table(peer_u[l]), "table_v": _pack_table(peer_v[l]),
            "ln3_g": row(ln3_g[l]), "ln3_b": row(ln3_b[l]),
        }
        mem2d = mem_prompt.reshape(bp * n_mem, D_MODEL)
        mk_p, mv_p = _mem_kv(mem2d, w_mk[l].astype(BF16), w_mv[l].astype(BF16), _pick_tile(bp * n_mem, 512))
        mk_p = mk_p.reshape(bp, n_mem, D_MODEL)
        mv_p = mv_p.reshape(bp, n_mem, D_MODEL)
        hp, st_p = _layer(hp, None, mk_p, mv_p, wts)
        bs = x_sample.shape[0]
        past = (cache_sb_k[l], cache_sb_v[l], cache_fox_k[l], cache_fox_v[l], cache_fox_logf[l])
        hs, st_s = _layer(hs, past, cache_mem_k[l].reshape(bs, -1, D_MODEL), cache_mem_v[l].reshape(bs, -1, D_MODEL), wts)
        for i in range(5):
            outs_p[i].append(st_p[i])
            outs_s[i].append(st_s[i])
        outs_p[5].append(mk_p.reshape(bp, n_mem, MEM_HEADS, MEM_HEAD_DIM))
        outs_p[6].append(mv_p.reshape(bp, n_mem, MEM_HEADS, MEM_HEAD_DIM))
    stack = lambda xs: jnp.stack(xs)
    return (hp, hs) + tuple(stack(o) for o in outs_p) + tuple(stack(o) for o in outs_s)
```

```python
import functools

import jax
import jax.numpy as jnp
from jax import lax
from jax.experimental import pallas as pl
from jax.experimental.pallas import tpu as pltpu

F32 = jnp.float32
BF16 = jnp.bfloat16
I32 = jnp.int32

D_MODEL = 1024
HEAD_DIM = 64
N_HEADS = 8
GROUP_WIDTH = N_HEADS * HEAD_DIM
MEM_HEADS = 4
MEM_HEAD_DIM = D_MODEL // MEM_HEADS
PEER_HEADS = 8
PEER_NKEYS = 128
PEER_TOPK = 16
PEER_HALF = 128
PEER_PICKS = PEER_HEADS * PEER_TOPK
PEER_EXPERTS = PEER_NKEYS * PEER_NKEYS
DN_ALPHA = 2.0 ** 0.25
LN_EPS = 1e-5
GN_EPS = 1e-6

LANES = 128
SUBLANES = 8
KEY_BLOCK = 128
QUERY_BLOCK = 256
FOX_TILE = 256 * 256
VMEM_LIMIT = 56 * 1024 * 1024


def _params(*sem):
    return pltpu.CompilerParams(dimension_semantics=sem, vmem_limit_bytes=VMEM_LIMIT)


def _log_sigmoid(x):
    return jnp.minimum(x, 0.0) - jnp.log1p(jnp.exp(-jnp.abs(x)))


def _layer_norm_rows(r, g, b):
    mu = jnp.mean(r, axis=-1, keepdims=True)
    d = r - mu
    var = jnp.mean(d * d, axis=-1, keepdims=True)
    return d * lax.rsqrt(var + LN_EPS) * g + b


def _inproj_kernel(x_ref, w_ref, wf_ref, bf_ref,
                   qsb_ref, ksb_ref, vsb_ref, qfx_ref, kfx_ref, vfx_ref,
                   ksbb_ref, vsbb_ref, kfxb_ref, vfxb_ref, lf_ref):
    xb = x_ref[...].astype(BF16)

    def proj(j):
        return jnp.dot(xb, w_ref[:, j * GROUP_WIDTH:(j + 1) * GROUP_WIDTH], preferred_element_type=F32)

    scale = HEAD_DIM ** -0.5
    qsb_ref[...] = (proj(0) * scale).astype(BF16)
    k = proj(1)
    ksb_ref[...] = k
    ksbb_ref[...] = k.astype(BF16)
    v = proj(2)
    vsb_ref[...] = v
    vsbb_ref[...] = v.astype(BF16)
    qfx_ref[...] = (proj(3) * scale).astype(BF16)
    k = proj(4)
    kfx_ref[...] = k
    kfxb_ref[...] = k.astype(BF16)
    v = proj(5)
    vfx_ref[...] = v
    vfxb_ref[...] = v.astype(BF16)
    f = jnp.dot(xb, wf_ref[...], preferred_element_type=F32) + bf_ref[...]
    lf_ref[...] = _log_sigmoid(f)[:, :N_HEADS]


def _in_projection(x2d, w_main, w_f, b_f, tm):
    m = x2d.shape[0]
    f32o = jax.ShapeDtypeStruct((m, GROUP_WIDTH), F32)
    bf16o = jax.ShapeDtypeStruct((m, GROUP_WIDTH), BF16)
    blk = pl.BlockSpec((tm, GROUP_WIDTH), lambda i: (i, 0))
    return pl.pallas_call(
        _inproj_kernel,
        grid=(m // tm,),
        in_specs=[
            pl.BlockSpec((tm, D_MODEL), lambda i: (i, 0)),
            pl.BlockSpec((D_MODEL, 6 * GROUP_WIDTH), lambda i: (0, 0)),
            pl.BlockSpec((D_MODEL, LANES), lambda i: (0, 0)),
            pl.BlockSpec((1, LANES), lambda i: (0, 0)),
        ],
        out_specs=[blk] * 10 + [pl.BlockSpec((tm, N_HEADS), lambda i: (i, 0))],
        out_shape=[bf16o, f32o, f32o, bf16o, f32o, f32o, bf16o, bf16o, bf16o, bf16o,
                   jax.ShapeDtypeStruct((m, N_HEADS), F32)],
        compiler_params=_params("parallel"),
        name="in_projection",
    )(x2d, w_main, w_f, b_f)


def _cumsum_kernel(lf_ref, tri_ref, c_ref):
    l, cols = lf_ref.shape

    def chunk(i, carry):
        r0 = pl.multiple_of(i * KEY_BLOCK, KEY_BLOCK)
        v = lf_ref[pl.ds(r0, KEY_BLOCK), :]
        hi = v.astype(BF16)
        r1 = v - hi.astype(F32)
        mid = r1.astype(BF16)
        lo = (r1 - mid.astype(F32)).astype(BF16)
        parts = jnp.concatenate([hi, mid, lo], axis=1)
        s = jnp.dot(tri_ref[...], parts, preferred_element_type=F32)
        c = s[:, :cols] + s[:, cols:2 * cols] + s[:, 2 * cols:] + carry
        c_ref[pl.ds(r0, KEY_BLOCK), :] = c
        return c[KEY_BLOCK - 1:, :]

    lax.fori_loop(0, l // KEY_BLOCK, chunk, jnp.zeros((1, cols), F32))


def _forget_cumsum(lf):
    b, l, _ = lf.shape
    cols = b * N_HEADS
    r = lax.broadcasted_iota(I32, (KEY_BLOCK, KEY_BLOCK), 0)
    c = lax.broadcasted_iota(I32, (KEY_BLOCK, KEY_BLOCK), 1)
    tri = (c <= r).astype(BF16)
    return pl.pallas_call(
        _cumsum_kernel,
        grid=(1,),
        in_specs=[pl.BlockSpec((l, cols), lambda i: (0, 0)),
                  pl.BlockSpec((KEY_BLOCK, KEY_BLOCK), lambda i: (0, 0))],
        out_specs=pl.BlockSpec((l, cols), lambda i: (0, 0)),
        out_shape=jax.ShapeDtypeStruct((l, cols), F32),
        compiler_params=_params("arbitrary"),
        name="forget_cumsum",
    )(lf.transpose(1, 0, 2).reshape(l, cols), tri)


def _head_masks(width):
    lane = lax.broadcasted_iota(I32, (1, width), 1)
    return lane < HEAD_DIM


def _sb_kernel(q_ref, k_ref, v_ref, tri_ref, o_ref, *, tq, past):
    i = pl.program_id(2)
    q = q_ref[0]
    first = _head_masks(LANES)
    zero_q = jnp.zeros_like(q)
    qh = (jnp.where(first, q, zero_q), jnp.where(first, zero_q, q))
    q_pos0 = past + i * tq
    diag = q_pos0 // KEY_BLOCK
    n_diag = max(1, tq // KEY_BLOCK)
    tri = tri_ref[...]

    def block(j, run, acc, masked):
        k0 = pl.multiple_of(j * KEY_BLOCK, KEY_BLOCK)
        kb = k_ref[0, pl.ds(k0, KEY_BLOCK), :]
        vb = v_ref[0, pl.ds(k0, KEY_BLOCK), :]
        if masked:
            kpos = k0 + lax.broadcasted_iota(I32, (tq, KEY_BLOCK), 1)
            qpos = q_pos0 + lax.broadcasted_iota(I32, (tq, KEY_BLOCK), 0)
            mask = kpos < qpos
        ws = []
        new_run = []
        for h in range(2):
            z = lax.dot_general(qh[h], kb, (((1,), (1,)), ((), ())), preferred_element_type=F32)
            sp = jnp.maximum(z, 0.0) + jnp.log(1.0 + jnp.exp(-jnp.abs(z)))
            log_beta = z - sp
            if masked:
                sp = jnp.where(mask, sp, 0.0)
            hi = sp.astype(BF16)
            lo = (sp - hi.astype(F32)).astype(BF16)
            c = jnp.dot(jnp.concatenate([hi, lo], axis=1), tri, preferred_element_type=F32)
            w = jnp.exp(log_beta + c[:, :KEY_BLOCK] + run[h])
            if masked:
                w = jnp.where(mask, w, 0.0)
            new_run.append(run[h] + c[:, KEY_BLOCK:])
            ws.append(w.astype(BF16))
        zero_v = jnp.zeros_like(vb)
        v2 = jnp.concatenate([jnp.where(first, vb, zero_v), jnp.where(first, zero_v, vb)], axis=0)
        acc = acc + jnp.dot(jnp.concatenate(ws, axis=1), v2, preferred_element_type=F32)
        return tuple(new_run), acc

    def alive(run):
        return (jnp.max(jnp.maximum(run[0], run[1])) > EXP_UNDERFLOW).astype(I32)

    zeros = jnp.zeros((tq, KEY_BLOCK), F32)
    run, acc = (zeros, zeros), jnp.zeros((tq, LANES), F32)
    for d in reversed(range(n_diag)):
        run, acc = block(diag + d, run, acc, True)

    def cond(state):
        it, live, _, _ = state
        return (it < diag) & (live > 0)

    def body(state):
        it, _, run, acc = state
        run, acc = block(diag - 1 - it, run, acc, False)
        return it + 1, alive(run), run, acc

    _, _, _, acc = lax.while_loop(cond, body, (jnp.int32(0), alive(run), run, acc))
    o_ref[0] = acc


EXP_UNDERFLOW = -105.0


def _cumsum_rhs():
    r = lax.broadcasted_iota(I32, (2 * KEY_BLOCK, 2 * KEY_BLOCK), 0) % KEY_BLOCK
    c = lax.broadcasted_iota(I32, (2 * KEY_BLOCK, 2 * KEY_BLOCK), 1)
    return -((c >= KEY_BLOCK) | (r > c)).astype(BF16)


def _stick_breaking_attention(q, k, v, tq, past):
    b, t, _ = q.shape
    lk = k.shape[1]
    pairs = GROUP_WIDTH // LANES
    return pl.pallas_call(
        functools.partial(_sb_kernel, tq=tq, past=past),
        grid=(b, pairs, t // tq),
        in_specs=[
            pl.BlockSpec((1, tq, LANES), lambda bi, hp, i: (bi, i, hp)),
            pl.BlockSpec((1, lk, LANES), lambda bi, hp, i: (bi, 0, hp)),
            pl.BlockSpec((1, lk, LANES), lambda bi, hp, i: (bi, 0, hp)),
            pl.BlockSpec((2 * KEY_BLOCK, 2 * KEY_BLOCK), lambda bi, hp, i: (0, 0)),
        ],
        out_specs=pl.BlockSpec((1, tq, LANES), lambda bi, hp, i: (bi, i, hp)),
        out_shape=jax.ShapeDtypeStruct((b, t, GROUP_WIDTH), F32),
        compiler_params=_params("parallel", "parallel", "arbitrary"),
        name="stick_breaking_attention",
    )(q, k, v, _cumsum_rhs())


def _fox_kernel(q_ref, k_ref, v_ref, cq_ref, ck_ref, o_ref, *, tq, past):
    i = pl.program_id(2)
    hp = pl.program_id(1)
    q = q_ref[0]
    first = _head_masks(LANES)
    zero_q = jnp.zeros_like(q)
    qh = (jnp.where(first, q, zero_q), jnp.where(first, zero_q, q))
    q_pos0 = past + i * tq
    kblk = ck_ref.shape[-1]
    diag = q_pos0 // kblk
    cq_all = cq_ref[0]
    head_lane = lax.broadcasted_iota(I32, (1, N_HEADS), 1)
    cq = [jnp.sum(jnp.where(head_lane == 2 * hp + h, cq_all, 0.0), axis=1, keepdims=True) for h in range(2)]

    def block(j, carry, masked):
        ms, ls, acc = carry
        k0 = pl.multiple_of(j * kblk, kblk)
        kb = k_ref[0, pl.ds(k0, kblk), :]
        vb = v_ref[0, pl.ds(k0, kblk), :]
        ck = ck_ref[0, 0, j]
        if masked:
            kpos = k0 + lax.broadcasted_iota(I32, (tq, kblk), 1)
            qpos = q_pos0 + lax.broadcasted_iota(I32, (tq, kblk), 0)
            mask = kpos <= qpos
        ps, new_m, new_l, scales = [], [], [], []
        for h in range(2):
            s = lax.dot_general(qh[h], kb, (((1,), (1,)), ((), ())), preferred_element_type=F32)
            s = s + cq[h] - ck[h:h + 1, :]
            if masked:
                s = jnp.where(mask, s, -jnp.inf)
            m = jnp.maximum(ms[h], jnp.max(s, axis=1, keepdims=True))
            p = jnp.exp(s - m)
            a = jnp.exp(ms[h] - m)
            new_m.append(m)
            new_l.append(a * ls[h] + jnp.sum(p, axis=1, keepdims=True))
            scales.append(a)
            ps.append(p.astype(BF16))
        zero_v = jnp.zeros_like(vb)
        v2 = jnp.concatenate([jnp.where(first, vb, zero_v), jnp.where(first, zero_v, vb)], axis=0)
        pv = jnp.dot(jnp.concatenate(ps, axis=1), v2, preferred_element_type=F32)
        acc = acc * jnp.where(first, scales[0], scales[1]) + pv
        return (tuple(new_m), tuple(new_l), acc)

    neg = jnp.full((tq, 1), -jnp.inf, F32)
    zero = jnp.zeros((tq, 1), F32)
    carry = block(diag, ((neg, neg), (zero, zero), jnp.zeros((tq, LANES), F32)), True)

    def body(it, carry):
        return block(diag - 1 - it, carry, False)

    _, ls, acc = lax.fori_loop(0, diag, body, carry)
    o_ref[0] = acc / jnp.where(first, ls[0], ls[1])


def _forgetting_attention(q, k, v, cq, ck, tq, past):
    b, t, _ = q.shape
    lk = k.shape[1]
    pairs = GROUP_WIDTH // LANES
    return pl.pallas_call(
        functools.partial(_fox_kernel, tq=tq, past=past),
        grid=(b, pairs, t // tq),
        in_specs=[
            pl.BlockSpec((1, tq, LANES), lambda bi, hp, i: (bi, i, hp)),
            pl.BlockSpec((1, lk, LANES), lambda bi, hp, i: (bi, 0, hp)),
            pl.BlockSpec((1, lk, LANES), lambda bi, hp, i: (bi, 0, hp)),
            pl.BlockSpec((1, tq, N_HEADS), lambda bi, hp, i: (bi, i, 0)),
            pl.BlockSpec((1, 1) + ck.shape[2:], lambda bi, hp, i: (bi, hp, 0, 0, 0)),
        ],
        out_specs=pl.BlockSpec((1, tq, LANES), lambda bi, hp, i: (bi, i, hp)),
        out_shape=jax.ShapeDtypeStruct((b, t, GROUP_WIDTH), F32),
        compiler_params=_params("parallel", "parallel", "arbitrary"),
        name="forgetting_attention",
    )(q, k, v, cq, ck)


def _mixout_kernel(osb_ref, ofx_ref, x_ref, gn_ref, w_ref, g_ref, b_ref, y_ref):
    def rms(o, g):
        return o * lax.rsqrt(jnp.mean(o * o, axis=-1, keepdims=True) + GN_EPS) * g

    gn = gn_ref[...]
    o = jnp.concatenate([rms(osb_ref[...], gn[:, :GROUP_WIDTH]), rms(ofx_ref[...], gn[:, GROUP_WIDTH:])], axis=1)
    mix = jnp.dot(o.astype(BF16), w_ref[...], preferred_element_type=F32)
    y_ref[...] = _layer_norm_rows(DN_ALPHA * x_ref[...] + mix, g_ref[...], b_ref[...])


def _mix_out(osb, ofx, x2d, w_gn, w_out, g, b, tm):
    m = x2d.shape[0]
    row = pl.BlockSpec((1, D_MODEL), lambda i: (0, 0))
    return pl.pallas_call(
        _mixout_kernel,
        grid=(m // tm,),
        in_specs=[
            pl.BlockSpec((tm, GROUP_WIDTH), lambda i: (i, 0)),
            pl.BlockSpec((tm, GROUP_WIDTH), lambda i: (i, 0)),
            pl.BlockSpec((tm, D_MODEL), lambda i: (i, 0)),
            row,
            pl.BlockSpec((D_MODEL, D_MODEL), lambda i: (0, 0)),
            row, row,
        ],
        out_specs=pl.BlockSpec((tm, D_MODEL), lambda i: (i, 0)),
        out_shape=jax.ShapeDtypeStruct((m, D_MODEL), F32),
        compiler_params=_params("parallel"),
        name="mix_out_ln1",
    )(osb, ofx, x2d, w_gn, w_out, g, b)


def _memkv_kernel(m_ref, wk_ref, wv_ref, k_ref, v_ref):
    mb = m_ref[...].astype(BF16)
    k_ref[...] = jnp.dot(mb, wk_ref[...], preferred_element_type=F32)
    v_ref[...] = jnp.dot(mb, wv_ref[...], preferred_element_type=F32)


def _mem_kv(mem2d, w_mk, w_mv, tm):
    m = mem2d.shape[0]
    wspec = pl.BlockSpec((D_MODEL, D_MODEL), lambda i: (0, 0))
    blk = pl.BlockSpec((tm, D_MODEL), lambda i: (i, 0))
    out = jax.ShapeDtypeStruct((m, D_MODEL), F32)
    return pl.pallas_call(
        _memkv_kernel,
        grid=(m // tm,),
        in_specs=[blk, wspec, wspec],
        out_specs=[blk, blk],
        out_shape=[out, out],
        compiler_params=_params("parallel"),
        name="mem_kv",
    )(mem2d, w_mk, w_mv)


def _memattn_kernel(x_ref, mk_ref, mv_ref, wq_ref, wo_ref, g_ref, b_ref, y_ref):
    x = x_ref[0]
    q = jnp.dot(x.astype(BF16), wq_ref[...], preferred_element_type=F32)
    qb = (q * (MEM_HEAD_DIM ** -0.5)).astype(BF16)
    mk = mk_ref[0].astype(BF16)
    mv = mv_ref[0].astype(BF16)
    outs = []
    for h in range(MEM_HEADS):
        sl = slice(h * MEM_HEAD_DIM, (h + 1) * MEM_HEAD_DIM)
        s = lax.dot_general(qb[:, sl], mk[:, sl], (((1,), (1,)), ((), ())), preferred_element_type=F32)
        p = jnp.exp(s - jnp.max(s, axis=1, keepdims=True))
        o = jnp.dot(p.astype(BF16), mv[:, sl], preferred_element_type=F32)
        outs.append(o / jnp.sum(p, axis=1, keepdims=True))
    o = jnp.concatenate(outs, axis=1).astype(BF16)
    att = jnp.dot(o, wo_ref[...], preferred_element_type=F32)
    y_ref[0] = _layer_norm_rows(DN_ALPHA * x + att, g_ref[...], b_ref[...])


def _mem_attention(x3d, mk, mv, w_mq, w_mo, g, b, tm):
    bsz, t, _ = x3d.shape
    n_mem = mk.shape[1]
    row = pl.BlockSpec((1, D_MODEL), lambda bi, i: (0, 0))
    wspec = pl.BlockSpec((D_MODEL, D_MODEL), lambda bi, i: (0, 0))
    return pl.pallas_call(
        _memattn_kernel,
        grid=(bsz, t // tm),
        in_specs=[
            pl.BlockSpec((1, tm, D_MODEL), lambda bi, i: (bi, i, 0)),
            pl.BlockSpec((1, n_mem, D_MODEL), lambda bi, i: (bi, 0, 0)),
            pl.BlockSpec((1, n_mem, D_MODEL), lambda bi, i: (bi, 0, 0)),
            wspec, wspec, row, row,
        ],
        out_specs=pl.BlockSpec((1, tm, D_MODEL), lambda bi, i: (bi, i, 0)),
        out_shape=jax.ShapeDtypeStruct((bsz, t, D_MODEL), F32),
        compiler_params=_params("parallel", "parallel"),
        name="mem_attention_ln2",
    )(x3d, mk, mv, w_mq, w_mo, g, b)


def _topk_rows(s, key, k, big):
    vals, keys = [], []
    for r in range(k):
        m = jnp.max(s, axis=0, keepdims=True)
        km = jnp.min(jnp.where(s == m, key, big), axis=0, keepdims=True)
        vals.append(m)
        keys.append(km)
        if r + 1 < k:
            s = jnp.where(key == km, -jnp.inf, s)
    return jnp.concatenate(vals, axis=0), jnp.concatenate(keys, axis=0)


HALF_EXPERTS = PEER_EXPERTS // 2
HALF_SHIFT = HALF_EXPERTS.bit_length() - 1
HIGH_MASK = -65536
ROUTE_HEADS_PER_STEP = 4


def _route_kernel(x_ref, w_ref, ka_ref, kb_ref, off_ref, sh_ref, g_ref, q_scr):
    tm = x_ref.shape[0]
    xb = x_ref[...].astype(BF16)
    for c in range(2 * PEER_HEADS):
        q_scr[c] = jnp.dot(xb, w_ref[:, c * PEER_HALF:(c + 1) * PEER_HALF],
                           preferred_element_type=F32).astype(BF16)
    key_io = lax.broadcasted_iota(I32, (PEER_NKEYS, tm), 0)
    n_cand = PEER_TOPK * PEER_TOPK
    nt = (((1,), (1,)), ((), ()))
    half = PEER_TOPK // 2
    cols = [PEER_TOPK, half] + [SUBLANES] * (half - 2)
    sub = lambda n: lax.broadcasted_iota(I32, (n, tm), 0)
    pos = jnp.concatenate([i * PEER_TOPK + sub(n) for i, n in enumerate(cols)] + [(half + sub(half)) * PEER_TOPK],
                          axis=0) * PEER_EXPERTS

    def head(h):
        sa = lax.dot_general(ka_ref[h], q_scr[2 * h], nt, preferred_element_type=F32)
        sb = lax.dot_general(kb_ref[h], q_scr[2 * h + 1], nt, preferred_element_type=F32)
        va, ia = _topk_rows(sa, key_io, PEER_TOPK, PEER_NKEYS)
        vb, ib = _topk_rows(sb, key_io, PEER_TOPK, PEER_NKEYS)
        cand = jnp.concatenate([va[i:i + 1] + vb[:n] for i, n in enumerate(cols)] + [va[half:] + vb[0:1]], axis=0)
        cidx = jnp.concatenate([ia[i:i + 1] * PEER_NKEYS + ib[:n] for i, n in enumerate(cols)]
                               + [ia[half:] * PEER_NKEYS + ib[0:1]], axis=0)
        top, tkey = _topk_rows(cand, pos + cidx, PEER_TOPK, n_cand * PEER_EXPERTS)
        e = jnp.exp(top - top[0:1])
        r0 = pl.multiple_of(h * PEER_TOPK, PEER_TOPK)
        off_ref[0, pl.ds(r0, PEER_TOPK), :] = (tkey & (HALF_EXPERTS - 1)) * SUBLANES
        sh_ref[0, pl.ds(r0, PEER_TOPK), :] = ((tkey >> HALF_SHIFT) & 1) * 16
        g_ref[0, pl.ds(r0, PEER_TOPK), :] = e / jnp.sum(e, axis=0, keepdims=True)

    def head_group(i, carry):
        for k in range(ROUTE_HEADS_PER_STEP):
            head(ROUTE_HEADS_PER_STEP * i + k)
        return carry

    lax.fori_loop(0, PEER_HEADS // ROUTE_HEADS_PER_STEP, head_group, 0)


def _peer_route(x2d, w_pq, keys_a, keys_b, tm):
    m = x2d.shape[0]
    nt = m // tm
    kspec = pl.BlockSpec((PEER_HEADS, PEER_NKEYS, PEER_HALF), lambda i: (0, 0, 0))
    ospec = pl.BlockSpec((1, PEER_PICKS, tm), lambda i: (i, 0, 0))
    return pl.pallas_call(
        _route_kernel,
        grid=(nt,),
        in_specs=[pl.BlockSpec((tm, D_MODEL), lambda i: (i, 0)),
                  pl.BlockSpec((D_MODEL, 2 * PEER_HEADS * PEER_HALF), lambda i: (0, 0)),
                  kspec, kspec],
        out_specs=[ospec, ospec, ospec],
        out_shape=[jax.ShapeDtypeStruct((nt, PEER_PICKS, tm), I32),
                   jax.ShapeDtypeStruct((nt, PEER_PICKS, tm), I32),
                   jax.ShapeDtypeStruct((nt, PEER_PICKS, tm), F32)],
        scratch_shapes=[pltpu.VMEM((2 * PEER_HEADS, tm, PEER_HALF), BF16)],
        compiler_params=_params("parallel"),
        name="peer_route",
    )(x2d, w_pq, keys_a, keys_b)


def _pack_table(t):
    bits = lax.bitcast_convert_type(t.astype(BF16), jnp.uint16).astype(jnp.uint32)
    word = (bits[:HALF_EXPERTS] << 16) | bits[HALF_EXPERTS:]
    return lax.bitcast_convert_type(word, I32).reshape(HALF_EXPERTS * SUBLANES, LANES)


def _table_spec():
    return pl.BlockSpec((HALF_EXPERTS * SUBLANES, LANES), lambda i: (0, 0), pipeline_mode=pl.Buffered(1))


def _table_row(tab_ref, off, shift):
    row = tab_ref[pl.ds(pl.multiple_of(off, SUBLANES), SUBLANES), :]
    return pltpu.bitcast((row << shift) & HIGH_MASK, F32)


def _rows_to_tiles(x_ref, tiles_ref):
    for s in range(SUBLANES):
        tiles_ref[:, s, :] = x_ref[:, s * LANES:(s + 1) * LANES]


def _tiles_to_rows(tiles_ref, y_ref):
    for s in range(SUBLANES):
        y_ref[:, s * LANES:(s + 1) * LANES] = tiles_ref[:, s, :]


BIT_REVERSED = (0, 4, 2, 6, 1, 5, 3, 7)


def _sublane_sums(prods, sub_io):
    def merge(a, b, h):
        low = (sub_io & h) == 0
        if 2 * h == SUBLANES:
            return jnp.where(low, a, b) + pltpu.roll(jnp.where(low, b, a), h, 0)
        return jnp.where(low, a, pltpu.roll(b, h, 0)) + jnp.where(low, pltpu.roll(a, SUBLANES - h, 0), b)

    p = [prods[BIT_REVERSED[k]] for k in range(SUBLANES)]
    t = [merge(p[2 * k], p[2 * k + 1], 4) for k in range(4)]
    u = [merge(t[2 * k], t[2 * k + 1], 2) for k in range(2)]
    return merge(u[0], u[1], 1)


REDUCE_UNROLL = 8


def _peer_in_kernel(off_s, sh_s, x_ref, sh_ref, g_ref, spread_ref, tab_ref, w_ref, part_ref, xt_ref):
    tm = x_ref.shape[0]
    sub_io = lax.broadcasted_iota(I32, (SUBLANES, LANES), 0)
    tok_io = lax.broadcasted_iota(I32, (PEER_PICKS, tm), 1)
    _rows_to_tiles(x_ref, xt_ref)

    def token(t, carry):
        x = xt_ref[t]
        for g8 in range(PEER_PICKS // SUBLANES):
            picks = [g8 * SUBLANES + s for s in range(SUBLANES)]
            prods = [x * _table_row(tab_ref, off_s[0, t, p], sh_s[0, t, p]) for p in picks]
            part_ref[t, g8 * SUBLANES:(g8 + 1) * SUBLANES, :] = _sublane_sums(prods, sub_io)
        return carry

    lax.fori_loop(0, tm, token, 0)

    def reduce(i, h_t):
        for k in range(REDUCE_UNROLL):
            t = i * REDUCE_UNROLL + k
            col = jnp.sum(part_ref[t], axis=1, keepdims=True)
            h_t = jnp.where(tok_io == t, col, h_t)
        return h_t

    h_t = lax.fori_loop(0, tm // REDUCE_UNROLL, reduce, jnp.zeros((PEER_PICKS, tm), F32))
    gelu = 0.5 * h_t * (1.0 + lax.erf(h_t * (2.0 ** -0.5)))
    w = (gelu * g_ref[0]).astype(BF16)
    tn = (((0,), (0,)), ((), ()))
    spread = spread_ref[...]
    w_rows = lax.dot_general(w, spread, tn, preferred_element_type=F32)
    low_half = lax.dot_general((sh_ref[0] >> 4).astype(BF16), spread, tn, preferred_element_type=F32)
    odd_row = (lax.broadcasted_iota(I32, (1, PACKED_ROWS * PEER_PICKS), 1) & 1).astype(F32)
    w_ref[...] = jnp.where(low_half + odd_row == 1.0, w_rows, 0.0)


OUT_TOKENS_PER_STEP = 8
PACKED_ROWS = 2 * SUBLANES
SPREAD_COLS = PACKED_ROWS * PEER_PICKS


def _peer_in(off, sh, x2d, gate, table, tm):
    nt = off.shape[0]
    m = x2d.shape[0]
    assert tm == LANES
    tspec = pl.BlockSpec((1, PEER_PICKS, tm), lambda i: (i, 0, 0))
    sspec = pl.BlockSpec((1, tm, PEER_PICKS), lambda i: (i, 0, 0), memory_space=pltpu.SMEM)
    pick = lax.broadcasted_iota(I32, (PEER_PICKS, SPREAD_COLS), 0)
    col = lax.broadcasted_iota(I32, (PEER_PICKS, SPREAD_COLS), 1)
    spread = (col // PACKED_ROWS == pick).astype(BF16)
    return pl.pallas_call(
        _peer_in_kernel,
        grid=(nt,),
        in_specs=[sspec, sspec,
                  pl.BlockSpec((tm, D_MODEL), lambda i: (i, 0)),
                  tspec, tspec,
                  pl.BlockSpec((PEER_PICKS, SPREAD_COLS), lambda i: (0, 0)),
                  _table_spec()],
        out_specs=pl.BlockSpec((tm, SPREAD_COLS), lambda i: (i, 0)),
        out_shape=jax.ShapeDtypeStruct((m, SPREAD_COLS), F32),
        scratch_shapes=[pltpu.VMEM((tm, PEER_PICKS, LANES), F32), pltpu.VMEM((tm, SUBLANES, LANES), F32)],
        compiler_params=_params("arbitrary"),
        name="peer_expert_in",
    )(off.transpose(0, 2, 1), sh.transpose(0, 2, 1), x2d, sh, gate, spread, table)


def _peer_out_kernel(off_s, w_ref, x_ref, g_ref, b_ref, tab_ref, y_ref, xt_ref, wt_ref):
    tm = x_ref.shape[0]
    _rows_to_tiles(x_ref, xt_ref)
    chunks = SPREAD_COLS // LANES
    for k in range(chunks):
        wt_ref[:, k, :] = w_ref[:, k * LANES:(k + 1) * LANES]
    lane = lax.broadcasted_iota(I32, (SUBLANES, LANES), 1)
    sub = lax.broadcasted_iota(I32, (SUBLANES, LANES), 0)
    own_chunk = (lane % PACKED_ROWS) // 2 == sub

    def token(t):
        tiles = [pltpu.bitcast(tab_ref[pl.ds(pl.multiple_of(off_s[0, t, p], SUBLANES), SUBLANES), :], BF16)
                 for p in range(PEER_PICKS)]
        wt = wt_ref[t]
        lhs = jnp.concatenate([jnp.where(own_chunk, jnp.broadcast_to(wt[k:k + 1, :], (SUBLANES, LANES)), 0.0)
                               for k in range(chunks)], axis=1).astype(BF16)
        out = jnp.dot(lhs, jnp.concatenate(tiles, axis=0), preferred_element_type=F32)
        xt_ref[t] = DN_ALPHA * xt_ref[t] + out

    def token_group(i, carry):
        for k in range(OUT_TOKENS_PER_STEP):
            token(i * OUT_TOKENS_PER_STEP + k)
        return carry

    lax.fori_loop(0, tm // OUT_TOKENS_PER_STEP, token_group, 0)
    _tiles_to_rows(xt_ref, y_ref)
    y_ref[...] = _layer_norm_rows(y_ref[...], g_ref[...], b_ref[...])


def _peer_out(off, w_rows, x2d, g, b, table, tm):
    nt = off.shape[0]
    m = x2d.shape[0]
    sspec = pl.BlockSpec((1, tm, PEER_PICKS), lambda i: (i, 0, 0), memory_space=pltpu.SMEM)
    vec = pl.BlockSpec((1, D_MODEL), lambda i: (0, 0))
    xspec = pl.BlockSpec((tm, D_MODEL), lambda i: (i, 0))
    return pl.pallas_call(
        _peer_out_kernel,
        grid=(nt,),
        in_specs=[sspec, pl.BlockSpec((tm, SPREAD_COLS), lambda i: (i, 0)), xspec, vec, vec, _table_spec()],
        out_specs=xspec,
        out_shape=jax.ShapeDtypeStruct((m, D_MODEL), F32),
        scratch_shapes=[pltpu.VMEM((tm, SUBLANES, LANES), F32),
                        pltpu.VMEM((tm, SPREAD_COLS // LANES, LANES), F32)],
        compiler_params=_params("arbitrary"),
        name="peer_expert_out",
    )(off.transpose(0, 2, 1), w_rows, x2d, g, b, table)


def _pick_tile(n, pref):
    t = pref
    while n % t:
        t //= 2
    return t


def _layer(x, past, mem_k, mem_v, wts):
    b, t, _ = x.shape
    m = b * t
    x2d = x.reshape(m, D_MODEL)
    tm = _pick_tile(m, 256)

    (qsb, ksb, vsb, qfx, kfx, vfx, ksbb, vsbb, kfxb, vfxb, lf) = _in_projection(
        x2d, wts["w_in_main"], wts["w_in_f"], wts["b_f"], tm)
    state = tuple(a.reshape(b, t, N_HEADS, HEAD_DIM) for a in (ksb, vsb, kfx, vfx)) + (lf.reshape(b, t, N_HEADS),)

    r3 = lambda a: a.reshape(b, t, GROUP_WIDTH)
    tq = min(QUERY_BLOCK, t)
    kblk = FOX_TILE // tq
    if past is None:
        p = 0
        k_sb, v_sb, k_fx, v_fx = r3(ksbb), r3(vsbb), r3(kfxb), r3(vfxb)
        lf_all = lf.reshape(b, t, N_HEADS)
    else:
        p = past[0].shape[1]
        pad = (-(p + t)) % kblk

        def cat(c, new):
            parts = [c.reshape(b, p, GROUP_WIDTH).astype(BF16), r3(new)]
            if pad:
                parts.append(jnp.zeros((b, pad, GROUP_WIDTH), BF16))
            return jnp.concatenate(parts, axis=1)

        k_sb, v_sb, k_fx, v_fx = cat(past[0], ksbb), cat(past[1], vsbb), cat(past[2], kfxb), cat(past[3], vfxb)
        parts = [past[4].astype(F32), lf.reshape(b, t, N_HEADS)]
        if pad:
            parts.append(jnp.zeros((b, pad, N_HEADS), F32))
        lf_all = jnp.concatenate(parts, axis=1)

    c_all = _forget_cumsum(lf_all)
    lk = c_all.shape[0]
    cq = c_all[p:p + t].reshape(t, b, N_HEADS).transpose(1, 0, 2)
    ck = c_all.T.reshape(b, N_HEADS // 2, 2, lk // kblk, kblk).transpose(0, 1, 3, 2, 4)

    assert kblk % tq == 0 and t % tq == 0 and p % tq == 0 and lk % kblk == 0 and kblk % KEY_BLOCK == 0
    o_sb = _stick_breaking_attention(r3(qsb), k_sb, v_sb, tq, p)
    o_fx = _forgetting_attention(r3(qfx), k_fx, v_fx, cq, ck, tq, p)

    x1 = _mix_out(o_sb.reshape(m, GROUP_WIDTH), o_fx.reshape(m, GROUP_WIDTH), x2d,
                  wts["w_gn"], wts["w_out"], wts["ln1_g"], wts["ln1_b"], tm)
    x2 = _mem_attention(x1.reshape(b, t, D_MODEL), mem_k, mem_v, wts["w_mq"], wts["w_mo"],
                        wts["ln2_g"], wts["ln2_b"], _pick_tile(t, 256))
    x2d2 = x2.reshape(m, D_MODEL)

    tr = LANES
    off, sh, gate = _peer_route(x2d2, wts["w_pq"], wts["keys_a"], wts["keys_b"], tr)
    w_rows = _peer_in(off, sh, x2d2, gate, wts["table_u"], tr)
    y = _peer_out(off, w_rows, x2d2, wts["ln3_g"], wts["ln3_b"], wts["table_v"], tr)
    return y.reshape(b, t, D_MODEL), state


def kernel(x_prompt, x_sample, mem_prompt, cache_sb_k, cache_sb_v, cache_fox_k, cache_fox_v, cache_fox_logf,
           cache_mem_k, cache_mem_v, w_in, b_f, w_gn, w_out, ln1_g, ln1_b, w_mq, w_mk, w_mv, w_mo, ln2_g, ln2_b,
           w_pq, peer_keys_a, peer_keys_b, peer_u, peer_v, ln3_g, ln3_b):
    depth = w_in.shape[0]
    hp, hs = x_prompt, x_sample
    bp = x_prompt.shape[0]
    n_mem = mem_prompt.shape[1]
    mix_cols = 6 * GROUP_WIDTH
    outs_p = [[] for _ in range(7)]
    outs_s = [[] for _ in range(5)]
    row = lambda a: a.reshape(1, D_MODEL)
    for l in range(depth):
        wts = {
            "w_in_main": w_in[l][:, :mix_cols].astype(BF16),
            "w_in_f": jnp.pad(w_in[l][:, mix_cols:], ((0, 0), (0, LANES - N_HEADS))).astype(BF16),
            "b_f": jnp.pad(b_f[l], (0, LANES - N_HEADS)).reshape(1, LANES),
            "w_gn": row(w_gn[l]), "w_out": w_out[l].astype(BF16),
            "ln1_g": row(ln1_g[l]), "ln1_b": row(ln1_b[l]),
            "w_mq": w_mq[l].astype(BF16), "w_mo": w_mo[l].astype(BF16),
            "ln2_g": row(ln2_g[l]), "ln2_b": row(ln2_b[l]),
            "w_pq": w_pq[l].astype(BF16),
            "keys_a": peer_keys_a[l].astype(BF16), "keys_b": peer_keys_b[l].astype(BF16),
            "table_u": _pack_table(peer_u[l]), "table_v": _pack_table(peer_v[l]),
            "ln3_g": row(ln3_g[l]), "ln3_b": row(ln3_b[l]),
        }
        mem2d = mem_prompt.reshape(bp * n_mem, D_MODEL)
        mk_p, mv_p = _mem_kv(mem2d, w_mk[l].astype(BF16), w_mv[l].astype(BF16), _pick_tile(bp * n_mem, 512))
        mk_p = mk_p.reshape(bp, n_mem, D_MODEL)
        mv_p = mv_p.reshape(bp, n_mem, D_MODEL)
        hp, st_p = _layer(hp, None, mk_p, mv_p, wts)
        bs = x_sample.shape[0]
        past = (cache_sb_k[l], cache_sb_v[l], cache_fox_k[l], cache_fox_v[l], cache_fox_logf[l])
        hs, st_s = _layer(hs, past, cache_mem_k[l].reshape(bs, -1, D_MODEL), cache_mem_v[l].reshape(bs, -1, D_MODEL), wts)
        for i in range(5):
            outs_p[i].append(st_p[i])
            outs_s[i].append(st_s[i])
        outs_p[5].append(mk_p.reshape(bp, n_mem, MEM_HEADS, MEM_HEAD_DIM))
        outs_p[6].append(mv_p.reshape(bp, n_mem, MEM_HEADS, MEM_HEAD_DIM))
    stack = lambda xs: jnp.stack(xs)
    return (hp, hs) + tuple(stack(o) for o in outs_p) + tuple(stack(o) for o in outs_s)
```

```python
import functools

import jax
import jax.numpy as jnp
from jax import lax
from jax.experimental import pallas as pl
from jax.experimental.pallas import tpu as pltpu

F32 = jnp.float32
BF16 = jnp.bfloat16
I32 = jnp.int32

D_MODEL = 1024
HEAD_DIM = 64
N_HEADS = 8
GROUP_WIDTH = N_HEADS * HEAD_DIM
MEM_HEADS = 4
MEM_HEAD_DIM = D_MODEL // MEM_HEADS
PEER_HEADS = 8
PEER_NKEYS = 128
PEER_TOPK = 16
PEER_HALF = 128
PEER_PICKS = PEER_HEADS * PEER_TOPK
PEER_EXPERTS = PEER_NKEYS * PEER_NKEYS
DN_ALPHA = 2.0 ** 0.25
LN_EPS = 1e-5
GN_EPS = 1e-6

LANES = 128
SUBLANES = 8
KEY_BLOCK = 128
QUERY_BLOCK = 256
FOX_TILE = 256 * 256
VMEM_LIMIT = 56 * 1024 * 1024


def _params(*sem):
    return pltpu.CompilerParams(dimension_semantics=sem, vmem_limit_bytes=VMEM_LIMIT)


def _log_sigmoid(x):
    return jnp.minimum(x, 0.0) - jnp.log1p(jnp.exp(-jnp.abs(x)))


def _layer_norm_rows(r, g, b):
    mu = jnp.mean(r, axis=-1, keepdims=True)
    d = r - mu
    var = jnp.mean(d * d, axis=-1, keepdims=True)
    return d * lax.rsqrt(var + LN_EPS) * g + b


def _inproj_kernel(x_ref, w_ref, wf_ref, bf_ref,
                   qsb_ref, ksb_ref, vsb_ref, qfx_ref, kfx_ref, vfx_ref,
                   ksbb_ref, vsbb_ref, kfxb_ref, vfxb_ref, lf_ref):
    xb = x_ref[...].astype(BF16)

    def proj(j):
        return jnp.dot(xb, w_ref[:, j * GROUP_WIDTH:(j + 1) * GROUP_WIDTH], preferred_element_type=F32)

    scale = HEAD_DIM ** -0.5
    qsb_ref[...] = (proj(0) * scale).astype(BF16)
    k = proj(1)
    ksb_ref[...] = k
    ksbb_ref[...] = k.astype(BF16)
    v = proj(2)
    vsb_ref[...] = v
    vsbb_ref[...] = v.astype(BF16)
    qfx_ref[...] = (proj(3) * scale).astype(BF16)
    k = proj(4)
    kfx_ref[...] = k
    kfxb_ref[...] = k.astype(BF16)
    v = proj(5)
    vfx_ref[...] = v
    vfxb_ref[...] = v.astype(BF16)
    f = jnp.dot(xb, wf_ref[...], preferred_element_type=F32) + bf_ref[...]
    lf_ref[...] = _log_sigmoid(f)[:, :N_HEADS]


def _in_projection(x2d, w_main, w_f, b_f, tm):
    m = x2d.shape[0]
    f32o = jax.ShapeDtypeStruct((m, GROUP_WIDTH), F32)
    bf16o = jax.ShapeDtypeStruct((m, GROUP_WIDTH), BF16)
    blk = pl.BlockSpec((tm, GROUP_WIDTH), lambda i: (i, 0))
    return pl.pallas_call(
        _inproj_kernel,
        grid=(m // tm,),
        in_specs=[
            pl.BlockSpec((tm, D_MODEL), lambda i: (i, 0)),
            pl.BlockSpec((D_MODEL, 6 * GROUP_WIDTH), lambda i: (0, 0)),
            pl.BlockSpec((D_MODEL, LANES), lambda i: (0, 0)),
            pl.BlockSpec((1, LANES), lambda i: (0, 0)),
        ],
        out_specs=[blk] * 10 + [pl.BlockSpec((tm, N_HEADS), lambda i: (i, 0))],
        out_shape=[bf16o, f32o, f32o, bf16o, f32o, f32o, bf16o, bf16o, bf16o, bf16o,
                   jax.ShapeDtypeStruct((m, N_HEADS), F32)],
        compiler_params=_params("parallel"),
        name="in_projection",
    )(x2d, w_main, w_f, b_f)


def _cumsum_kernel(lf_ref, tri_ref, c_ref):
    l, cols = lf_ref.shape

    def chunk(i, carry):
        r0 = pl.multiple_of(i * KEY_BLOCK, KEY_BLOCK)
        v = lf_ref[pl.ds(r0, KEY_BLOCK), :]
        hi = v.astype(BF16)
        r1 = v - hi.astype(F32)
        mid = r1.astype(BF16)
        lo = (r1 - mid.astype(F32)).astype(BF16)
        parts = jnp.concatenate([hi, mid, lo], axis=1)
        s = jnp.dot(tri_ref[...], parts, preferred_element_type=F32)
        c = s[:, :cols] + s[:, cols:2 * cols] + s[:, 2 * cols:] + carry
        c_ref[pl.ds(r0, KEY_BLOCK), :] = c
        return c[KEY_BLOCK - 1:, :]

    lax.fori_loop(0, l // KEY_BLOCK, chunk, jnp.zeros((1, cols), F32))


def _forget_cumsum(lf):
    b, l, _ = lf.shape
    cols = b * N_HEADS
    r = lax.broadcasted_iota(I32, (KEY_BLOCK, KEY_BLOCK), 0)
    c = lax.broadcasted_iota(I32, (KEY_BLOCK, KEY_BLOCK), 1)
    tri = (c <= r).astype(BF16)
    return pl.pallas_call(
        _cumsum_kernel,
        grid=(1,),
        in_specs=[pl.BlockSpec((l, cols), lambda i: (0, 0)),
                  pl.BlockSpec((KEY_BLOCK, KEY_BLOCK), lambda i: (0, 0))],
        out_specs=pl.BlockSpec((l, cols), lambda i: (0, 0)),
        out_shape=jax.ShapeDtypeStruct((l, cols), F32),
        compiler_params=_params("arbitrary"),
        name="forget_cumsum",
    )(lf.transpose(1, 0, 2).reshape(l, cols), tri)


def _head_masks(width):
    lane = lax.broadcasted_iota(I32, (1, width), 1)
    return lane < HEAD_DIM


def _sb_kernel(q_ref, k_ref, v_ref, tri_ref, o_ref, *, tq, past):
    i = pl.program_id(2)
    q = q_ref[0]
    first = _head_masks(LANES)
    zero_q = jnp.zeros_like(q)
    qh = (jnp.where(first, q, zero_q), jnp.where(first, zero_q, q))
    q_pos0 = past + i * tq
    diag = q_pos0 // KEY_BLOCK
    n_diag = max(1, tq // KEY_BLOCK)
    tri = tri_ref[...]

    def block(j, run, acc, masked):
        k0 = pl.multiple_of(j * KEY_BLOCK, KEY_BLOCK)
        kb = k_ref[0, pl.ds(k0, KEY_BLOCK), :]
        vb = v_ref[0, pl.ds(k0, KEY_BLOCK), :]
        if masked:
            kpos = k0 + lax.broadcasted_iota(I32, (tq, KEY_BLOCK), 1)
            qpos = q_pos0 + lax.broadcasted_iota(I32, (tq, KEY_BLOCK), 0)
            mask = kpos < qpos
        ws = []
        new_run = []
        for h in range(2):
            z = lax.dot_general(qh[h], kb, (((1,), (1,)), ((), ())), preferred_element_type=F32)
            sp = jnp.maximum(z, 0.0) + jnp.log(1.0 + jnp.exp(-jnp.abs(z)))
            log_beta = z - sp
            if masked:
                sp = jnp.where(mask, sp, 0.0)
            hi = sp.astype(BF16)
            lo = (sp - hi.astype(F32)).astype(BF16)
            c = jnp.dot(jnp.concatenate([hi, lo], axis=1), tri, preferred_element_type=F32)
            w = jnp.exp(log_beta + c[:, :KEY_BLOCK] + run[h])
            if masked:
                w = jnp.where(mask, w, 0.0)
            new_run.append(run[h] + c[:, KEY_BLOCK:])
            ws.append(w.astype(BF16))
        zero_v = jnp.zeros_like(vb)
        v2 = jnp.concatenate([jnp.where(first, vb, zero_v), jnp.where(first, zero_v, vb)], axis=0)
        acc = acc + jnp.dot(jnp.concatenate(ws, axis=1), v2, preferred_element_type=F32)
        return tuple(new_run), acc

    def alive(run):
        return (jnp.max(jnp.maximum(run[0], run[1])) > EXP_UNDERFLOW).astype(I32)

    zeros = jnp.zeros((tq, KEY_BLOCK), F32)
    run, acc = (zeros, zeros), jnp.zeros((tq, LANES), F32)
    for d in reversed(range(n_diag)):
        run, acc = block(diag + d, run, acc, True)

    def cond(state):
        it, live, _, _ = state
        return (it < diag) & (live > 0)

    def body(state):
        it, _, run, acc = state
        run, acc = block(diag - 1 - it, run, acc, False)
        return it + 1, alive(run), run, acc

    _, _, _, acc = lax.while_loop(cond, body, (jnp.int32(0), alive(run), run, acc))
    o_ref[0] = acc


EXP_UNDERFLOW = -105.0


def _cumsum_rhs():
    r = lax.broadcasted_iota(I32, (2 * KEY_BLOCK, 2 * KEY_BLOCK), 0) % KEY_BLOCK
    c = lax.broadcasted_iota(I32, (2 * KEY_BLOCK, 2 * KEY_BLOCK), 1)
    return -((c >= KEY_BLOCK) | (r > c)).astype(BF16)


def _stick_breaking_attention(q, k, v, tq, past):
    b, t, _ = q.shape
    lk = k.shape[1]
    pairs = GROUP_WIDTH // LANES
    return pl.pallas_call(
        functools.partial(_sb_kernel, tq=tq, past=past),
        grid=(b, pairs, t // tq),
        in_specs=[
            pl.BlockSpec((1, tq, LANES), lambda bi, hp, i: (bi, i, hp)),
            pl.BlockSpec((1, lk, LANES), lambda bi, hp, i: (bi, 0, hp)),
            pl.BlockSpec((1, lk, LANES), lambda bi, hp, i: (bi, 0, hp)),
            pl.BlockSpec((2 * KEY_BLOCK, 2 * KEY_BLOCK), lambda bi, hp, i: (0, 0)),
        ],
        out_specs=pl.BlockSpec((1, tq, LANES), lambda bi, hp, i: (bi, i, hp)),
        out_shape=jax.ShapeDtypeStruct((b, t, GROUP_WIDTH), F32),
        compiler_params=_params("parallel", "parallel", "arbitrary"),
        name="stick_breaking_attention",
    )(q, k, v, _cumsum_rhs())


def _fox_kernel(q_ref, k_ref, v_ref, cq_ref, ck_ref, o_ref, *, tq, past):
    i = pl.program_id(2)
    hp = pl.program_id(1)
    q = q_ref[0]
    first = _head_masks(LANES)
    zero_q = jnp.zeros_like(q)
    qh = (jnp.where(first, q, zero_q), jnp.where(first, zero_q, q))
    q_pos0 = past + i * tq
    kblk = ck_ref.shape[-1]
    diag = q_pos0 // kblk
    cq_all = cq_ref[0]
    head_lane = lax.broadcasted_iota(I32, (1, N_HEADS), 1)
    cq = [jnp.sum(jnp.where(head_lane == 2 * hp + h, cq_all, 0.0), axis=1, keepdims=True) for h in range(2)]

    def block(j, carry, masked):
        ms, ls, acc = carry
        k0 = pl.multiple_of(j * kblk, kblk)
        kb = k_ref[0, pl.ds(k0, kblk), :]
        vb = v_ref[0, pl.ds(k0, kblk), :]
        ck = ck_ref[0, 0, j]
        if masked:
            kpos = k0 + lax.broadcasted_iota(I32, (tq, kblk), 1)
            qpos = q_pos0 + lax.broadcasted_iota(I32, (tq, kblk), 0)
            mask = kpos <= qpos
        ps, new_m, new_l, scales = [], [], [], []
        for h in range(2):
            s = lax.dot_general(qh[h], kb, (((1,), (1,)), ((), ())), preferred_element_type=F32)
            s = s + cq[h] - ck[h:h + 1, :]
            if masked:
                s = jnp.where(mask, s, -jnp.inf)
            m = jnp.maximum(ms[h], jnp.max(s, axis=1, keepdims=True))
            p = jnp.exp(s - m)
            a = jnp.exp(ms[h] - m)
            new_m.append(m)
            new_l.append(a * ls[h] + jnp.sum(p, axis=1, keepdims=True))
            scales.append(a)
            ps.append(p.astype(BF16))
        zero_v = jnp.zeros_like(vb)
        v2 = jnp.concatenate([jnp.where(first, vb, zero_v), jnp.where(first, zero_v, vb)], axis=0)
        pv = jnp.dot(jnp.concatenate(ps, axis=1), v2, preferred_element_type=F32)
        acc = acc * jnp.where(first, scales[0], scales[1]) + pv
        return (tuple(new_m), tuple(new_l), acc)

    neg = jnp.full((tq, 1), -jnp.inf, F32)
    zero = jnp.zeros((tq, 1), F32)
    carry = block(diag, ((neg, neg), (zero, zero), jnp.zeros((tq, LANES), F32)), True)

    def body(it, carry):
        return block(diag - 1 - it, carry, False)

    _, ls, acc = lax.fori_loop(0, diag, body, carry)
    o_ref[0] = acc / jnp.where(first, ls[0], ls[1])


def _forgetting_attention(q, k, v, cq, ck, tq, past):
    b, t, _ = q.shape
    lk = k.shape[1]
    pairs = GROUP_WIDTH // LANES
    return pl.pallas_call(
        functools.partial(_fox_kernel, tq=tq, past=past),
        grid=(b, pairs, t // tq),
        in_specs=[
            pl.BlockSpec((1, tq, LANES), lambda bi, hp, i: (bi, i, hp)),
            pl.BlockSpec((1, lk, LANES), lambda bi, hp, i: (bi, 0, hp)),
            pl.BlockSpec((1, lk, LANES), lambda bi, hp, i: (bi, 0, hp)),
            pl.BlockSpec((1, tq, N_HEADS), lambda bi, hp, i: (bi, i, 0)),
            pl.BlockSpec((1, 1) + ck.shape[2:], lambda bi, hp, i: (bi, hp, 0, 0, 0)),
        ],
        out_specs=pl.BlockSpec((1, tq, LANES), lambda bi, hp, i: (bi, i, hp)),
        out_shape=jax.ShapeDtypeStruct((b, t, GROUP_WIDTH), F32),
        compiler_params=_params("parallel", "parallel", "arbitrary"),
        name="forgetting_attention",
    )(q, k, v, cq, ck)


def _mixout_kernel(osb_ref, ofx_ref, x_ref, gn_ref, w_ref, g_ref, b_ref, y_ref):
    def rms(o, g):
        return o * lax.rsqrt(jnp.mean(o * o, axis=-1, keepdims=True) + GN_EPS) * g

    gn = gn_ref[...]
    o = jnp.concatenate([rms(osb_ref[...], gn[:, :GROUP_WIDTH]), rms(ofx_ref[...], gn[:, GROUP_WIDTH:])], axis=1)
    mix = jnp.dot(o.astype(BF16), w_ref[...], preferred_element_type=F32)
    y_ref[...] = _layer_norm_rows(DN_ALPHA * x_ref[...] + mix, g_ref[...], b_ref[...])


def _mix_out(osb, ofx, x2d, w_gn, w_out, g, b, tm):
    m = x2d.shape[0]
    row = pl.BlockSpec((1, D_MODEL), lambda i: (0, 0))
    return pl.pallas_call(
        _mixout_kernel,
        grid=(m // tm,),
        in_specs=[
            pl.BlockSpec((tm, GROUP_WIDTH), lambda i: (i, 0)),
            pl.BlockSpec((tm, GROUP_WIDTH), lambda i: (i, 0)),
            pl.BlockSpec((tm, D_MODEL), lambda i: (i, 0)),
            row,
            pl.BlockSpec((D_MODEL, D_MODEL), lambda i: (0, 0)),
            row, row,
        ],
        out_specs=pl.BlockSpec((tm, D_MODEL), lambda i: (i, 0)),
        out_shape=jax.ShapeDtypeStruct((m, D_MODEL), F32),
        compiler_params=_params("parallel"),
        name="mix_out_ln1",
    )(osb, ofx, x2d, w_gn, w_out, g, b)


def _memkv_kernel(m_ref, wk_ref, wv_ref, k_ref, v_ref):
    mb = m_ref[...].astype(BF16)
    k_ref[...] = jnp.dot(mb, wk_ref[...], preferred_element_type=F32)
    v_ref[...] = jnp.dot(mb, wv_ref[...], preferred_element_type=F32)


def _mem_kv(mem2d, w_mk, w_mv, tm):
    m = mem2d.shape[0]
    wspec = pl.BlockSpec((D_MODEL, D_MODEL), lambda i: (0, 0))
    blk = pl.BlockSpec((tm, D_MODEL), lambda i: (i, 0))
    out = jax.ShapeDtypeStruct((m, D_MODEL), F32)
    return pl.pallas_call(
        _memkv_kernel,
        grid=(m // tm,),
        in_specs=[blk, wspec, wspec],
        out_specs=[blk, blk],
        out_shape=[out, out],
        compiler_params=_params("parallel"),
        name="mem_kv",
    )(mem2d, w_mk, w_mv)


def _memattn_kernel(x_ref, mk_ref, mv_ref, wq_ref, wo_ref, g_ref, b_ref, y_ref):
    x = x_ref[0]
    q = jnp.dot(x.astype(BF16), wq_ref[...], preferred_element_type=F32)
    qb = (q * (MEM_HEAD_DIM ** -0.5)).astype(BF16)
    mk = mk_ref[0].astype(BF16)
    mv = mv_ref[0].astype(BF16)
    outs = []
    for h in range(MEM_HEADS):
        sl = slice(h * MEM_HEAD_DIM, (h + 1) * MEM_HEAD_DIM)
        s = lax.dot_general(qb[:, sl], mk[:, sl], (((1,), (1,)), ((), ())), preferred_element_type=F32)
        p = jnp.exp(s - jnp.max(s, axis=1, keepdims=True))
        o = jnp.dot(p.astype(BF16), mv[:, sl], preferred_element_type=F32)
        outs.append(o / jnp.sum(p, axis=1, keepdims=True))
    o = jnp.concatenate(outs, axis=1).astype(BF16)
    att = jnp.dot(o, wo_ref[...], preferred_element_type=F32)
    y_ref[0] = _layer_norm_rows(DN_ALPHA * x + att, g_ref[...], b_ref[...])


def _mem_attention(x3d, mk, mv, w_mq, w_mo, g, b, tm):
    bsz, t, _ = x3d.shape
    n_mem = mk.shape[1]
    row = pl.BlockSpec((1, D_MODEL), lambda bi, i: (0, 0))
    wspec = pl.BlockSpec((D_MODEL, D_MODEL), lambda bi, i: (0, 0))
    return pl.pallas_call(
        _memattn_kernel,
        grid=(bsz, t // tm),
        in_specs=[
            pl.BlockSpec((1, tm, D_MODEL), lambda bi, i: (bi, i, 0)),
            pl.BlockSpec((1, n_mem, D_MODEL), lambda bi, i: (bi, 0, 0)),
            pl.BlockSpec((1, n_mem, D_MODEL), lambda bi, i: (bi, 0, 0)),
            wspec, wspec, row, row,
        ],
        out_specs=pl.BlockSpec((1, tm, D_MODEL), lambda bi, i: (bi, i, 0)),
        out_shape=jax.ShapeDtypeStruct((bsz, t, D_MODEL), F32),
        compiler_params=_params("parallel", "parallel"),
        name="mem_attention_ln2",
    )(x3d, mk, mv, w_mq, w_mo, g, b)


def _tree(op, xs):
    xs = list(xs)
    while len(xs) > 1:
        xs = [op(xs[i], xs[i + 1]) if i + 1 < len(xs) else xs[i] for i in range(0, len(xs), 2)]
    return xs[0]


def _all_sublanes(op, x):
    for shift in (4, 2, 1):
        x = op(x, pltpu.roll(x, shift, 0))
    return x


def _topk_slabs(slabs, keys, k, big):
    vals, kout = [], []
    for r in range(k):
        m = _all_sublanes(jnp.maximum, _tree(jnp.maximum, slabs))
        km = _all_sublanes(jnp.minimum, _tree(jnp.minimum, [jnp.where(s == m, kk, big) for s, kk in zip(slabs, keys)]))
        vals.append(m)
        kout.append(km)
        if r + 1 < k:
            slabs = [jnp.where(kk == km, -jnp.inf, s) for s, kk in zip(slabs, keys)]
    return vals, kout


def _pack_rows(rows, sub_io):
    out = rows[0]
    for r in range(1, SUBLANES):
        out = jnp.where(sub_io == r, rows[r], out)
    return out


HALF_EXPERTS = PEER_EXPERTS // 2
HALF_SHIFT = HALF_EXPERTS.bit_length() - 1
HIGH_MASK = -65536
ROUTE_HEADS_PER_STEP = 4


def _route_kernel(x_ref, w_ref, ka_ref, kb_ref, off_ref, sh_ref, g_ref, q_scr):
    tm = x_ref.shape[0]
    xb = x_ref[...].astype(BF16)
    for c in range(2 * PEER_HEADS):
        q_scr[c] = jnp.dot(xb, w_ref[:, c * PEER_HALF:(c + 1) * PEER_HALF],
                           preferred_element_type=F32).astype(BF16)
    assert PEER_TOPK == 2 * SUBLANES
    nt = (((1,), (1,)), ((), ()))
    sub_io = lax.broadcasted_iota(I32, (SUBLANES, tm), 0)
    sub_f = sub_io.astype(F32)
    n_slabs = PEER_NKEYS // SUBLANES
    key_slabs = [sub_f + float(SUBLANES * i) for i in range(n_slabs)]
    experts = float(PEER_EXPERTS)
    big = float(PEER_TOPK * PEER_TOPK) * experts

    def head(h):
        sa = lax.dot_general(ka_ref[h], q_scr[2 * h], nt, preferred_element_type=F32)
        sb = lax.dot_general(kb_ref[h], q_scr[2 * h + 1], nt, preferred_element_type=F32)
        slabs = lambda s: [s[SUBLANES * i:SUBLANES * (i + 1)] for i in range(n_slabs)]
        va, ia = _topk_slabs(slabs(sa), key_slabs, PEER_TOPK, float(PEER_NKEYS))
        vb, ib = _topk_slabs(slabs(sb), key_slabs, PEER_TOPK, float(PEER_NKEYS))
        va_hi, ia_hi = _pack_rows(va[SUBLANES:], sub_io), _pack_rows(ia[SUBLANES:], sub_io)
        vb_lo, ib_lo = _pack_rows(vb[:SUBLANES], sub_io), _pack_rows(ib[:SUBLANES], sub_io)
        vb_hi, ib_hi = _pack_rows(vb[SUBLANES:], sub_io), _pack_rows(ib[SUBLANES:], sub_io)
        cand = [va[0] + vb_lo, va[0] + vb_hi]
        ckey = [sub_f * experts + (ia[0] * PEER_NKEYS + ib_lo),
                (sub_f + SUBLANES) * experts + (ia[0] * PEER_NKEYS + ib_hi)]
        for i in range(1, SUBLANES):
            cand.append(va[i] + vb_lo)
            ckey.append((sub_f + float(i * PEER_TOPK)) * experts + (ia[i] * PEER_NKEYS + ib_lo))
        cand.append(va_hi + vb[0])
        ckey.append((sub_f + SUBLANES) * (PEER_TOPK * experts) + (ia_hi * PEER_NKEYS + ib[0]))
        top, tkey = _topk_slabs(cand, ckey, PEER_TOPK, big)
        r0 = pl.multiple_of(h * PEER_TOPK, SUBLANES)
        es = [jnp.exp(_pack_rows(top[SUBLANES * j:SUBLANES * (j + 1)], sub_io) - top[0]) for j in range(2)]
        denom = _all_sublanes(jnp.add, es[0] + es[1])
        for j in range(2):
            expert = _pack_rows(tkey[SUBLANES * j:SUBLANES * (j + 1)], sub_io).astype(I32) & (PEER_EXPERTS - 1)
            rows = pl.ds(pl.multiple_of(r0 + SUBLANES * j, SUBLANES), SUBLANES)
            off_ref[0, rows, :] = (expert & (HALF_EXPERTS - 1)) * SUBLANES
            sh_ref[0, rows, :] = (expert >> HALF_SHIFT) * 16
            g_ref[0, rows, :] = es[j] / denom

    def head_group(i, carry):
        for k in range(ROUTE_HEADS_PER_STEP):
            head(ROUTE_HEADS_PER_STEP * i + k)
        return carry

    lax.fori_loop(0, PEER_HEADS // ROUTE_HEADS_PER_STEP, head_group, 0)


def _peer_route(x2d, w_pq, keys_a, keys_b, tm):
    m = x2d.shape[0]
    nt = m // tm
    kspec = pl.BlockSpec((PEER_HEADS, PEER_NKEYS, PEER_HALF), lambda i: (0, 0, 0))
    ospec = pl.BlockSpec((1, PEER_PICKS, tm), lambda i: (i, 0, 0))
    return pl.pallas_call(
        _route_kernel,
        grid=(nt,),
        in_specs=[pl.BlockSpec((tm, D_MODEL), lambda i: (i, 0)),
                  pl.BlockSpec((D_MODEL, 2 * PEER_HEADS * PEER_HALF), lambda i: (0, 0)),
                  kspec, kspec],
        out_specs=[ospec, ospec, ospec],
        out_shape=[jax.ShapeDtypeStruct((nt, PEER_PICKS, tm), I32),
                   jax.ShapeDtypeStruct((nt, PEER_PICKS, tm), I32),
                   jax.ShapeDtypeStruct((nt, PEER_PICKS, tm), F32)],
        scratch_shapes=[pltpu.VMEM((2 * PEER_HEADS, tm, PEER_HALF), BF16)],
        compiler_params=_params("parallel"),
        name="peer_route",
    )(x2d, w_pq, keys_a, keys_b)


def _pack_table(t):
    bits = lax.bitcast_convert_type(t.astype(BF16), jnp.uint16).astype(jnp.uint32)
    word = (bits[:HALF_EXPERTS] << 16) | bits[HALF_EXPERTS:]
    return lax.bitcast_convert_type(word, I32).reshape(HALF_EXPERTS * SUBLANES, LANES)


def _table_spec():
    return pl.BlockSpec((HALF_EXPERTS * SUBLANES, LANES), lambda i: (0, 0), pipeline_mode=pl.Buffered(1))


def _table_row(tab_ref, off, shift):
    row = tab_ref[pl.ds(pl.multiple_of(off, SUBLANES), SUBLANES), :]
    return pltpu.bitcast((row << shift) & HIGH_MASK, F32)


def _rows_to_tiles(x_ref, tiles_ref):
    for s in range(SUBLANES):
        tiles_ref[:, s, :] = x_ref[:, s * LANES:(s + 1) * LANES]


def _tiles_to_rows(tiles_ref, y_ref):
    for s in range(SUBLANES):
        y_ref[:, s * LANES:(s + 1) * LANES] = tiles_ref[:, s, :]


BIT_REVERSED = (0, 4, 2, 6, 1, 5, 3, 7)


def _sublane_sums(prods, sub_io):
    def merge(a, b, h):
        low = (sub_io & h) == 0
        if 2 * h == SUBLANES:
            return jnp.where(low, a, b) + pltpu.roll(jnp.where(low, b, a), h, 0)
        return jnp.where(low, a, pltpu.roll(b, h, 0)) + jnp.where(low, pltpu.roll(a, SUBLANES - h, 0), b)

    p = [prods[BIT_REVERSED[k]] for k in range(SUBLANES)]
    t = [merge(p[2 * k], p[2 * k + 1], 4) for k in range(4)]
    u = [merge(t[2 * k], t[2 * k + 1], 2) for k in range(2)]
    return merge(u[0], u[1], 1)


REDUCE_UNROLL = 8


def _peer_in_kernel(off_s, sh_s, x_ref, sh_ref, g_ref, spread_ref, tab_ref, w_ref, part_ref, xt_ref):
    tm = x_ref.shape[0]
    sub_io = lax.broadcasted_iota(I32, (SUBLANES, LANES), 0)
    tok_io = lax.broadcasted_iota(I32, (PEER_PICKS, tm), 1)
    _rows_to_tiles(x_ref, xt_ref)

    def token(t, carry):
        x = xt_ref[t]
        for g8 in range(PEER_PICKS // SUBLANES):
            picks = [g8 * SUBLANES + s for s in range(SUBLANES)]
            prods = [x * _table_row(tab_ref, off_s[0, t, p], sh_s[0, t, p]) for p in picks]
            part_ref[t, g8 * SUBLANES:(g8 + 1) * SUBLANES, :] = _sublane_sums(prods, sub_io)
        return carry

    lax.fori_loop(0, tm, token, 0)

    def reduce(i, h_t):
        for k in range(REDUCE_UNROLL):
            t = i * REDUCE_UNROLL + k
            col = jnp.sum(part_ref[t], axis=1, keepdims=True)
            h_t = jnp.where(tok_io == t, col, h_t)
        return h_t

    h_t = lax.fori_loop(0, tm // REDUCE_UNROLL, reduce, jnp.zeros((PEER_PICKS, tm), F32))
    gelu = 0.5 * h_t * (1.0 + lax.erf(h_t * (2.0 ** -0.5)))
    w = (gelu * g_ref[0]).astype(BF16)
    tn = (((0,), (0,)), ((), ()))
    spread = spread_ref[...]
    w_rows = lax.dot_general(w, spread, tn, preferred_element_type=F32)
    low_half = lax.dot_general((sh_ref[0] >> 4).astype(BF16), spread, tn, preferred_element_type=F32)
    odd_row = (lax.broadcasted_iota(I32, (1, PACKED_ROWS * PEER_PICKS), 1) & 1).astype(F32)
    w_ref[...] = jnp.where(low_half + odd_row == 1.0, w_rows, 0.0)


OUT_TOKENS_PER_STEP = 8
PACKED_ROWS = 2 * SUBLANES
SPREAD_COLS = PACKED_ROWS * PEER_PICKS


def _peer_in(off, sh, x2d, gate, table, tm):
    nt = off.shape[0]
    m = x2d.shape[0]
    assert tm == LANES
    tspec = pl.BlockSpec((1, PEER_PICKS, tm), lambda i: (i, 0, 0))
    sspec = pl.BlockSpec((1, tm, PEER_PICKS), lambda i: (i, 0, 0), memory_space=pltpu.SMEM)
    pick = lax.broadcasted_iota(I32, (PEER_PICKS, SPREAD_COLS), 0)
    col = lax.broadcasted_iota(I32, (PEER_PICKS, SPREAD_COLS), 1)
    spread = (col // PACKED_ROWS == pick).astype(BF16)
    return pl.pallas_call(
        _peer_in_kernel,
        grid=(nt,),
        in_specs=[sspec, sspec,
                  pl.BlockSpec((tm, D_MODEL), lambda i: (i, 0)),
                  tspec, tspec,
                  pl.BlockSpec((PEER_PICKS, SPREAD_COLS), lambda i: (0, 0)),
                  _table_spec()],
        out_specs=pl.BlockSpec((tm, SPREAD_COLS), lambda i: (i, 0)),
        out_shape=jax.ShapeDtypeStruct((m, SPREAD_COLS), F32),
        scratch_shapes=[pltpu.VMEM((tm, PEER_PICKS, LANES), F32), pltpu.VMEM((tm, SUBLANES, LANES), F32)],
        compiler_params=_params("arbitrary"),
        name="peer_expert_in",
    )(off.transpose(0, 2, 1), sh.transpose(0, 2, 1), x2d, sh, gate, spread, table)


def _peer_out_kernel(off_s, w_ref, x_ref, g_ref, b_ref, tab_ref, y_ref, xt_ref, wt_ref):
    tm = x_ref.shape[0]
    _rows_to_tiles(x_ref, xt_ref)
    chunks = SPREAD_COLS // LANES
    for k in range(chunks):
        wt_ref[:, k, :] = w_ref[:, k * LANES:(k + 1) * LANES]
    lane = lax.broadcasted_iota(I32, (SUBLANES, LANES), 1)
    sub = lax.broadcasted_iota(I32, (SUBLANES, LANES), 0)
    own_chunk = (lane % PACKED_ROWS) // 2 == sub

    def token(t):
        tiles = [pltpu.bitcast(tab_ref[pl.ds(pl.multiple_of(off_s[0, t, p], SUBLANES), SUBLANES), :], BF16)
                 for p in range(PEER_PICKS)]
        wt = wt_ref[t]
        lhs = jnp.concatenate([jnp.where(own_chunk, jnp.broadcast_to(wt[k:k + 1, :], (SUBLANES, LANES)), 0.0)
                               for k in range(chunks)], axis=1).astype(BF16)
        out = jnp.dot(lhs, jnp.concatenate(tiles, axis=0), preferred_element_type=F32)
        xt_ref[t] = DN_ALPHA * xt_ref[t] + out

    def token_group(i, carry):
        for k in range(OUT_TOKENS_PER_STEP):
            token(i * OUT_TOKENS_PER_STEP + k)
        return carry

    lax.fori_loop(0, tm // OUT_TOKENS_PER_STEP, token_group, 0)
    _tiles_to_rows(xt_ref, y_ref)
    y_ref[...] = _layer_norm_rows(y_ref[...], g_ref[...], b_ref[...])


def _peer_out(off, w_rows, x2d, g, b, table, tm):
    nt = off.shape[0]
    m = x2d.shape[0]
    sspec = pl.BlockSpec((1, tm, PEER_PICKS), lambda i: (i, 0, 0), memory_space=pltpu.SMEM)
    vec = pl.BlockSpec((1, D_MODEL), lambda i: (0, 0))
    xspec = pl.BlockSpec((tm, D_MODEL), lambda i: (i, 0))
    return pl.pallas_call(
        _peer_out_kernel,
        grid=(nt,),
        in_specs=[sspec, pl.BlockSpec((tm, SPREAD_COLS), lambda i: (i, 0)), xspec, vec, vec, _table_spec()],
        out_specs=xspec,
        out_shape=jax.ShapeDtypeStruct((m, D_MODEL), F32),
        scratch_shapes=[pltpu.VMEM((tm, SUBLANES, LANES), F32),
                        pltpu.VMEM((tm, SPREAD_COLS // LANES, LANES), F32)],
        compiler_params=_params("arbitrary"),
        name="peer_expert_out",
    )(off.transpose(0, 2, 1), w_rows, x2d, g, b, table)


def _pick_tile(n, pref):
    t = pref
    while n % t:
        t //= 2
    return t


def _layer(x, past, mem_k, mem_v, wts):
    b, t, _ = x.shape
    m = b * t
    x2d = x.reshape(m, D_MODEL)
    tm = _pick_tile(m, 256)

    (qsb, ksb, vsb, qfx, kfx, vfx, ksbb, vsbb, kfxb, vfxb, lf) = _in_projection(
        x2d, wts["w_in_main"], wts["w_in_f"], wts["b_f"], tm)
    state = tuple(a.reshape(b, t, N_HEADS, HEAD_DIM) for a in (ksb, vsb, kfx, vfx)) + (lf.reshape(b, t, N_HEADS),)

    r3 = lambda a: a.reshape(b, t, GROUP_WIDTH)
    tq = min(QUERY_BLOCK, t)
    kblk = FOX_TILE // tq
    if past is None:
        p = 0
        k_sb, v_sb, k_fx, v_fx = r3(ksbb), r3(vsbb), r3(kfxb), r3(vfxb)
        lf_all = lf.reshape(b, t, N_HEADS)
    else:
        p = past[0].shape[1]
        pad = (-(p + t)) % kblk

        def cat(c, new):
            parts = [c.reshape(b, p, GROUP_WIDTH).astype(BF16), r3(new)]
            if pad:
                parts.append(jnp.zeros((b, pad, GROUP_WIDTH), BF16))
            return jnp.concatenate(parts, axis=1)

        k_sb, v_sb, k_fx, v_fx = cat(past[0], ksbb), cat(past[1], vsbb), cat(past[2], kfxb), cat(past[3], vfxb)
        parts = [past[4].astype(F32), lf.reshape(b, t, N_HEADS)]
        if pad:
            parts.append(jnp.zeros((b, pad, N_HEADS), F32))
        lf_all = jnp.concatenate(parts, axis=1)

    c_all = _forget_cumsum(lf_all)
    lk = c_all.shape[0]
    cq = c_all[p:p + t].reshape(t, b, N_HEADS).transpose(1, 0, 2)
    ck = c_all.T.reshape(b, N_HEADS // 2, 2, lk // kblk, kblk).transpose(0, 1, 3, 2, 4)

    assert kblk % tq == 0 and t % tq == 0 and p % tq == 0 and lk % kblk == 0 and kblk % KEY_BLOCK == 0
    o_sb = _stick_breaking_attention(r3(qsb), k_sb, v_sb, tq, p)
    o_fx = _forgetting_attention(r3(qfx), k_fx, v_fx, cq, ck, tq, p)

    x1 = _mix_out(o_sb.reshape(m, GROUP_WIDTH), o_fx.reshape(m, GROUP_WIDTH), x2d,
                  wts["w_gn"], wts["w_out"], wts["ln1_g"], wts["ln1_b"], tm)
    x2 = _mem_attention(x1.reshape(b, t, D_MODEL), mem_k, mem_v, wts["w_mq"], wts["w_mo"],
                        wts["ln2_g"], wts["ln2_b"], _pick_tile(t, 256))
    x2d2 = x2.reshape(m, D_MODEL)

    tr = LANES
    off, sh, gate = _peer_route(x2d2, wts["w_pq"], wts["keys_a"], wts["keys_b"], tr)
    w_rows = _peer_in(off, sh, x2d2, gate, wts["table_u"], tr)
    y = _peer_out(off, w_rows, x2d2, wts["ln3_g"], wts["ln3_b"], wts["table_v"], tr)
    return y.reshape(b, t, D_MODEL), state


def kernel(x_prompt, x_sample, mem_prompt, cache_sb_k, cache_sb_v, cache_fox_k, cache_fox_v, cache_fox_logf,
           cache_mem_k, cache_mem_v, w_in, b_f, w_gn, w_out, ln1_g, ln1_b, w_mq, w_mk, w_mv, w_mo, ln2_g, ln2_b,
           w_pq, peer_keys_a, peer_keys_b, peer_u, peer_v, ln3_g, ln3_b):
    depth = w_in.shape[0]
    hp, hs = x_prompt, x_sample
    bp = x_prompt.shape[0]
    n_mem = mem_prompt.shape[1]
    mix_cols = 6 * GROUP_WIDTH
    outs_p = [[] for _ in range(7)]
    outs_s = [[] for _ in range(5)]
    row = lambda a: a.reshape(1, D_MODEL)
    for l in range(depth):
        wts = {
            "w_in_main": w_in[l][:, :mix_cols].astype(BF16),
            "w_in_f": jnp.pad(w_in[l][:, mix_cols:], ((0, 0), (0, LANES - N_HEADS))).astype(BF16),
            "b_f": jnp.pad(b_f[l], (0, LANES - N_HEADS)).reshape(1, LANES),
            "w_gn": row(w_gn[l]), "w_out": w_out[l].astype(BF16),
            "ln1_g": row(ln1_g[l]), "ln1_b": row(ln1_b[l]),
            "w_mq": w_mq[l].astype(BF16), "w_mo": w_mo[l].astype(BF16),
            "ln2_g": row(ln2_g[l]), "ln2_b": row(ln2_b[l]),
            "w_pq": w_pq[l].astype(BF16),
            "keys_a": peer_keys_a[l].astype(BF16), "keys_b": peer_keys_b[l].astype(BF16),
            "table_u": _pack_table(peer_u[l]), "table_v": _pack_table(peer_v[l]),
            "ln3_g": row(ln3_g[l]), "ln3_b": row(ln3_b[l]),
        }
        mem2d = mem_prompt.reshape(bp * n_mem, D_MODEL)
        mk_p, mv_p = _mem_kv(mem2d, w_mk[l].astype(BF16), w_mv[l].astype(BF16), _pick_tile(bp * n_mem, 512))
        mk_p = mk_p.reshape(bp, n_mem, D_MODEL)
        mv_p = mv_p.reshape(bp, n_mem, D_MODEL)
        hp, st_p = _layer(hp, None, mk_p, mv_p, wts)
        bs = x_sample.shape[0]
        past = (cache_sb_k[l], cache_sb_v[l], cache_fox_k[l], cache_fox_v[l], cache_fox_logf[l])
        hs, st_s = _layer(hs, past, cache_mem_k[l].reshape(bs, -1, D_MODEL), cache_mem_v[l].reshape(bs, -1, D_MODEL), wts)
        for i in range(5):
            outs_p[i].append(st_p[i])
            outs_s[i].append(st_s[i])
        outs_p[5].append(mk_p.reshape(bp, n_mem, MEM_HEADS, MEM_HEAD_DIM))
        outs_p[6].append(mv_p.reshape(bp, n_mem, MEM_HEADS, MEM_HEAD_DIM))
    stack = lambda xs: jnp.stack(xs)
    return (hp, hs) + tuple(stack(o) for o in outs_p) + tuple(stack(o) for o in outs_s)
```

```python
import functools

import jax
import jax.numpy as jnp
from jax import lax
from jax.experimental import pallas as pl
from jax.experimental.pallas import tpu as pltpu

F32 = jnp.float32
BF16 = jnp.bfloat16
I32 = jnp.int32

D_MODEL = 1024
HEAD_DIM = 64
N_HEADS = 8
GROUP_WIDTH = N_HEADS * HEAD_DIM
MEM_HEADS = 4
MEM_HEAD_DIM = D_MODEL // MEM_HEADS
PEER_HEADS = 8
PEER_NKEYS = 128
PEER_TOPK = 16
PEER_HALF = 128
PEER_PICKS = PEER_HEADS * PEER_TOPK
PEER_EXPERTS = PEER_NKEYS * PEER_NKEYS
DN_ALPHA = 2.0 ** 0.25
LN_EPS = 1e-5
GN_EPS = 1e-6

LANES = 128
SUBLANES = 8
KEY_BLOCK = 128
QUERY_BLOCK = 256
FOX_TILE = 256 * 256
VMEM_LIMIT = 56 * 1024 * 1024


def _params(*sem):
    return pltpu.CompilerParams(dimension_semantics=sem, vmem_limit_bytes=VMEM_LIMIT)


def _log_sigmoid(x):
    return jnp.minimum(x, 0.0) - jnp.log1p(jnp.exp(-jnp.abs(x)))


def _layer_norm_rows(r, g, b):
    mu = jnp.mean(r, axis=-1, keepdims=True)
    d = r - mu
    var = jnp.mean(d * d, axis=-1, keepdims=True)
    return d * lax.rsqrt(var + LN_EPS) * g + b


def _inproj_kernel(x_ref, w_ref, wf_ref, bf_ref,
                   qsb_ref, ksb_ref, vsb_ref, qfx_ref, kfx_ref, vfx_ref,
                   ksbb_ref, vsbb_ref, kfxb_ref, vfxb_ref, lf_ref):
    xb = x_ref[...].astype(BF16)

    def proj(j):
        return jnp.dot(xb, w_ref[:, j * GROUP_WIDTH:(j + 1) * GROUP_WIDTH], preferred_element_type=F32)

    scale = HEAD_DIM ** -0.5
    qsb_ref[...] = (proj(0) * scale).astype(BF16)
    k = proj(1)
    ksb_ref[...] = k
    ksbb_ref[...] = k.astype(BF16)
    v = proj(2)
    vsb_ref[...] = v
    vsbb_ref[...] = v.astype(BF16)
    qfx_ref[...] = (proj(3) * scale).astype(BF16)
    k = proj(4)
    kfx_ref[...] = k
    kfxb_ref[...] = k.astype(BF16)
    v = proj(5)
    vfx_ref[...] = v
    vfxb_ref[...] = v.astype(BF16)
    f = jnp.dot(xb, wf_ref[...], preferred_element_type=F32) + bf_ref[...]
    lf_ref[...] = _log_sigmoid(f)[:, :N_HEADS]


def _in_projection(x2d, w_main, w_f, b_f, tm):
    m = x2d.shape[0]
    f32o = jax.ShapeDtypeStruct((m, GROUP_WIDTH), F32)
    bf16o = jax.ShapeDtypeStruct((m, GROUP_WIDTH), BF16)
    blk = pl.BlockSpec((tm, GROUP_WIDTH), lambda i: (i, 0))
    return pl.pallas_call(
        _inproj_kernel,
        grid=(m // tm,),
        in_specs=[
            pl.BlockSpec((tm, D_MODEL), lambda i: (i, 0)),
            pl.BlockSpec((D_MODEL, 6 * GROUP_WIDTH), lambda i: (0, 0)),
            pl.BlockSpec((D_MODEL, LANES), lambda i: (0, 0)),
            pl.BlockSpec((1, LANES), lambda i: (0, 0)),
        ],
        out_specs=[blk] * 10 + [pl.BlockSpec((tm, N_HEADS), lambda i: (i, 0))],
        out_shape=[bf16o, f32o, f32o, bf16o, f32o, f32o, bf16o, bf16o, bf16o, bf16o,
                   jax.ShapeDtypeStruct((m, N_HEADS), F32)],
        compiler_params=_params("parallel"),
        name="in_projection",
    )(x2d, w_main, w_f, b_f)


def _cumsum_kernel(lf_ref, tri_ref, c_ref):
    l, cols = lf_ref.shape

    def chunk(i, carry):
        r0 = pl.multiple_of(i * KEY_BLOCK, KEY_BLOCK)
        v = lf_ref[pl.ds(r0, KEY_BLOCK), :]
        hi = v.astype(BF16)
        r1 = v - hi.astype(F32)
        mid = r1.astype(BF16)
        lo = (r1 - mid.astype(F32)).astype(BF16)
        parts = jnp.concatenate([hi, mid, lo], axis=1)
        s = jnp.dot(tri_ref[...], parts, preferred_element_type=F32)
        c = s[:, :cols] + s[:, cols:2 * cols] + s[:, 2 * cols:] + carry
        c_ref[pl.ds(r0, KEY_BLOCK), :] = c
        return c[KEY_BLOCK - 1:, :]

    lax.fori_loop(0, l // KEY_BLOCK, chunk, jnp.zeros((1, cols), F32))


def _forget_cumsum(lf):
    b, l, _ = lf.shape
    cols = b * N_HEADS
    r = lax.broadcasted_iota(I32, (KEY_BLOCK, KEY_BLOCK), 0)
    c = lax.broadcasted_iota(I32, (KEY_BLOCK, KEY_BLOCK), 1)
    tri = (c <= r).astype(BF16)
    return pl.pallas_call(
        _cumsum_kernel,
        grid=(1,),
        in_specs=[pl.BlockSpec((l, cols), lambda i: (0, 0)),
                  pl.BlockSpec((KEY_BLOCK, KEY_BLOCK), lambda i: (0, 0))],
        out_specs=pl.BlockSpec((l, cols), lambda i: (0, 0)),
        out_shape=jax.ShapeDtypeStruct((l, cols), F32),
        compiler_params=_params("arbitrary"),
        name="forget_cumsum",
    )(lf.transpose(1, 0, 2).reshape(l, cols), tri)


def _head_masks(width):
    lane = lax.broadcasted_iota(I32, (1, width), 1)
    return lane < HEAD_DIM


def _sb_kernel(q_ref, k_ref, v_ref, tri_ref, o_ref, *, tq, past):
    i = pl.program_id(2)
    q = q_ref[0]
    first = _head_masks(LANES)
    zero_q = jnp.zeros_like(q)
    qh = (jnp.where(first, q, zero_q), jnp.where(first, zero_q, q))
    q_pos0 = past + i * tq
    diag = q_pos0 // KEY_BLOCK
    n_diag = max(1, tq // KEY_BLOCK)
    tri = tri_ref[...]

    def block(j, run, acc, masked):
        k0 = pl.multiple_of(j * KEY_BLOCK, KEY_BLOCK)
        kb = k_ref[0, pl.ds(k0, KEY_BLOCK), :]
        vb = v_ref[0, pl.ds(k0, KEY_BLOCK), :]
        if masked:
            kpos = k0 + lax.broadcasted_iota(I32, (tq, KEY_BLOCK), 1)
            qpos = q_pos0 + lax.broadcasted_iota(I32, (tq, KEY_BLOCK), 0)
            mask = kpos < qpos
        ws = []
        new_run = []
        for h in range(2):
            z = lax.dot_general(qh[h], kb, (((1,), (1,)), ((), ())), preferred_element_type=F32)
            sp = jnp.maximum(z, 0.0) + jnp.log(1.0 + jnp.exp(-jnp.abs(z)))
            log_beta = z - sp
            if masked:
                sp = jnp.where(mask, sp, 0.0)
            hi = sp.astype(BF16)
            lo = (sp - hi.astype(F32)).astype(BF16)
            c = jnp.dot(jnp.concatenate([hi, lo], axis=1), tri, preferred_element_type=F32)
            w = jnp.exp(log_beta + c[:, :KEY_BLOCK] + run[h])
            if masked:
                w = jnp.where(mask, w, 0.0)
            new_run.append(run[h] + c[:, KEY_BLOCK:])
            ws.append(w.astype(BF16))
        zero_v = jnp.zeros_like(vb)
        v2 = jnp.concatenate([jnp.where(first, vb, zero_v), jnp.where(first, zero_v, vb)], axis=0)
        acc = acc + jnp.dot(jnp.concatenate(ws, axis=1), v2, preferred_element_type=F32)
        return tuple(new_run), acc

    def alive(run):
        return (jnp.max(jnp.maximum(run[0], run[1])) > EXP_UNDERFLOW).astype(I32)

    zeros = jnp.zeros((tq, KEY_BLOCK), F32)
    run, acc = (zeros, zeros), jnp.zeros((tq, LANES), F32)
    for d in reversed(range(n_diag)):
        run, acc = block(diag + d, run, acc, True)

    def cond(state):
        it, live, _, _ = state
        return (it < diag) & (live > 0)

    def body(state):
        it, _, run, acc = state
        run, acc = block(diag - 1 - it, run, acc, False)
        return it + 1, alive(run), run, acc

    _, _, _, acc = lax.while_loop(cond, body, (jnp.int32(0), alive(run), run, acc))
    o_ref[0] = acc


EXP_UNDERFLOW = -105.0


def _cumsum_rhs():
    r = lax.broadcasted_iota(I32, (2 * KEY_BLOCK, 2 * KEY_BLOCK), 0) % KEY_BLOCK
    c = lax.broadcasted_iota(I32, (2 * KEY_BLOCK, 2 * KEY_BLOCK), 1)
    return -((c >= KEY_BLOCK) | (r > c)).astype(BF16)


def _stick_breaking_attention(q, k, v, tq, past):
    b, t, _ = q.shape
    lk = k.shape[1]
    pairs = GROUP_WIDTH // LANES
    return pl.pallas_call(
        functools.partial(_sb_kernel, tq=tq, past=past),
        grid=(b, pairs, t // tq),
        in_specs=[
            pl.BlockSpec((1, tq, LANES), lambda bi, hp, i: (bi, i, hp)),
            pl.BlockSpec((1, lk, LANES), lambda bi, hp, i: (bi, 0, hp)),
            pl.BlockSpec((1, lk, LANES), lambda bi, hp, i: (bi, 0, hp)),
            pl.BlockSpec((2 * KEY_BLOCK, 2 * KEY_BLOCK), lambda bi, hp, i: (0, 0)),
        ],
        out_specs=pl.BlockSpec((1, tq, LANES), lambda bi, hp, i: (bi, i, hp)),
        out_shape=jax.ShapeDtypeStruct((b, t, GROUP_WIDTH), F32),
        compiler_params=_params("parallel", "parallel", "arbitrary"),
        name="stick_breaking_attention",
    )(q, k, v, _cumsum_rhs())


def _fox_kernel(q_ref, k_ref, v_ref, cq_ref, ck_ref, o_ref, *, tq, past):
    i = pl.program_id(2)
    hp = pl.program_id(1)
    q = q_ref[0]
    first = _head_masks(LANES)
    zero_q = jnp.zeros_like(q)
    qh = (jnp.where(first, q, zero_q), jnp.where(first, zero_q, q))
    q_pos0 = past + i * tq
    kblk = ck_ref.shape[-1]
    diag = q_pos0 // kblk
    cq_all = cq_ref[0]
    head_lane = lax.broadcasted_iota(I32, (1, N_HEADS), 1)
    cq = [jnp.sum(jnp.where(head_lane == 2 * hp + h, cq_all, 0.0), axis=1, keepdims=True) for h in range(2)]

    def block(j, carry, masked):
        ms, ls, acc = carry
        k0 = pl.multiple_of(j * kblk, kblk)
        kb = k_ref[0, pl.ds(k0, kblk), :]
        vb = v_ref[0, pl.ds(k0, kblk), :]
        ck = ck_ref[0, 0, j]
        if masked:
            kpos = k0 + lax.broadcasted_iota(I32, (tq, kblk), 1)
            qpos = q_pos0 + lax.broadcasted_iota(I32, (tq, kblk), 0)
            mask = kpos <= qpos
        ps, new_m, new_l, scales = [], [], [], []
        for h in range(2):
            s = lax.dot_general(qh[h], kb, (((1,), (1,)), ((), ())), preferred_element_type=F32)
            s = s + cq[h] - ck[h:h + 1, :]
            if masked:
                s = jnp.where(mask, s, -jnp.inf)
            m = jnp.maximum(ms[h], jnp.max(s, axis=1, keepdims=True))
            p = jnp.exp(s - m)
            a = jnp.exp(ms[h] - m)
            new_m.append(m)
            new_l.append(a * ls[h] + jnp.sum(p, axis=1, keepdims=True))
            scales.append(a)
            ps.append(p.astype(BF16))
        zero_v = jnp.zeros_like(vb)
        v2 = jnp.concatenate([jnp.where(first, vb, zero_v), jnp.where(first, zero_v, vb)], axis=0)
        pv = jnp.dot(jnp.concatenate(ps, axis=1), v2, preferred_element_type=F32)
        acc = acc * jnp.where(first, scales[0], scales[1]) + pv
        return (tuple(new_m), tuple(new_l), acc)

    neg = jnp.full((tq, 1), -jnp.inf, F32)
    zero = jnp.zeros((tq, 1), F32)
    carry = block(diag, ((neg, neg), (zero, zero), jnp.zeros((tq, LANES), F32)), True)

    def body(it, carry):
        return block(diag - 1 - it, carry, False)

    _, ls, acc = lax.fori_loop(0, diag, body, carry)
    o_ref[0] = acc / jnp.where(first, ls[0], ls[1])


def _forgetting_attention(q, k, v, cq, ck, tq, past):
    b, t, _ = q.shape
    lk = k.shape[1]
    pairs = GROUP_WIDTH // LANES
    return pl.pallas_call(
        functools.partial(_fox_kernel, tq=tq, past=past),
        grid=(b, pairs, t // tq),
        in_specs=[
            pl.BlockSpec((1, tq, LANES), lambda bi, hp, i: (bi, i, hp)),
            pl.BlockSpec((1, lk, LANES), lambda bi, hp, i: (bi, 0, hp)),
            pl.BlockSpec((1, lk, LANES), lambda bi, hp, i: (bi, 0, hp)),
            pl.BlockSpec((1, tq, N_HEADS), lambda bi, hp, i: (bi, i, 0)),
            pl.BlockSpec((1, 1) + ck.shape[2:], lambda bi, hp, i: (bi, hp, 0, 0, 0)),
        ],
        out_specs=pl.BlockSpec((1, tq, LANES), lambda bi, hp, i: (bi, i, hp)),
        out_shape=jax.ShapeDtypeStruct((b, t, GROUP_WIDTH), F32),
        compiler_params=_params("parallel", "parallel", "arbitrary"),
        name="forgetting_attention",
    )(q, k, v, cq, ck)


def _mixout_kernel(osb_ref, ofx_ref, x_ref, gn_ref, w_ref, g_ref, b_ref, y_ref):
    def rms(o, g):
        return o * lax.rsqrt(jnp.mean(o * o, axis=-1, keepdims=True) + GN_EPS) * g

    gn = gn_ref[...]
    o = jnp.concatenate([rms(osb_ref[...], gn[:, :GROUP_WIDTH]), rms(ofx_ref[...], gn[:, GROUP_WIDTH:])], axis=1)
    mix = jnp.dot(o.astype(BF16), w_ref[...], preferred_element_type=F32)
    y_ref[...] = _layer_norm_rows(DN_ALPHA * x_ref[...] + mix, g_ref[...], b_ref[...])


def _mix_out(osb, ofx, x2d, w_gn, w_out, g, b, tm):
    m = x2d.shape[0]
    row = pl.BlockSpec((1, D_MODEL), lambda i: (0, 0))
    return pl.pallas_call(
        _mixout_kernel,
        grid=(m // tm,),
        in_specs=[
            pl.BlockSpec((tm, GROUP_WIDTH), lambda i: (i, 0)),
            pl.BlockSpec((tm, GROUP_WIDTH), lambda i: (i, 0)),
            pl.BlockSpec((tm, D_MODEL), lambda i: (i, 0)),
            row,
            pl.BlockSpec((D_MODEL, D_MODEL), lambda i: (0, 0)),
            row, row,
        ],
        out_specs=pl.BlockSpec((tm, D_MODEL), lambda i: (i, 0)),
        out_shape=jax.ShapeDtypeStruct((m, D_MODEL), F32),
        compiler_params=_params("parallel"),
        name="mix_out_ln1",
    )(osb, ofx, x2d, w_gn, w_out, g, b)


def _memkv_kernel(m_ref, wk_ref, wv_ref, k_ref, v_ref):
    mb = m_ref[...].astype(BF16)
    k_ref[...] = jnp.dot(mb, wk_ref[...], preferred_element_type=F32)
    v_ref[...] = jnp.dot(mb, wv_ref[...], preferred_element_type=F32)


def _mem_kv(mem2d, w_mk, w_mv, tm):
    m = mem2d.shape[0]
    wspec = pl.BlockSpec((D_MODEL, D_MODEL), lambda i: (0, 0))
    blk = pl.BlockSpec((tm, D_MODEL), lambda i: (i, 0))
    out = jax.ShapeDtypeStruct((m, D_MODEL), F32)
    return pl.pallas_call(
        _memkv_kernel,
        grid=(m // tm,),
        in_specs=[blk, wspec, wspec],
        out_specs=[blk, blk],
        out_shape=[out, out],
        compiler_params=_params("parallel"),
        name="mem_kv",
    )(mem2d, w_mk, w_mv)


def _memattn_kernel(x_ref, mk_ref, mv_ref, wq_ref, wo_ref, g_ref, b_ref, y_ref):
    x = x_ref[0]
    q = jnp.dot(x.astype(BF16), wq_ref[...], preferred_element_type=F32)
    qb = (q * (MEM_HEAD_DIM ** -0.5)).astype(BF16)
    mk = mk_ref[0].astype(BF16)
    mv = mv_ref[0].astype(BF16)
    outs = []
    for h in range(MEM_HEADS):
        sl = slice(h * MEM_HEAD_DIM, (h + 1) * MEM_HEAD_DIM)
        s = lax.dot_general(qb[:, sl], mk[:, sl], (((1,), (1,)), ((), ())), preferred_element_type=F32)
        p = jnp.exp(s - jnp.max(s, axis=1, keepdims=True))
        o = jnp.dot(p.astype(BF16), mv[:, sl], preferred_element_type=F32)
        outs.append(o / jnp.sum(p, axis=1, keepdims=True))
    o = jnp.concatenate(outs, axis=1).astype(BF16)
    att = jnp.dot(o, wo_ref[...], preferred_element_type=F32)
    y_ref[0] = _layer_norm_rows(DN_ALPHA * x + att, g_ref[...], b_ref[...])


def _mem_attention(x3d, mk, mv, w_mq, w_mo, g, b, tm):
    bsz, t, _ = x3d.shape
    n_mem = mk.shape[1]
    row = pl.BlockSpec((1, D_MODEL), lambda bi, i: (0, 0))
    wspec = pl.BlockSpec((D_MODEL, D_MODEL), lambda bi, i: (0, 0))
    return pl.pallas_call(
        _memattn_kernel,
        grid=(bsz, t // tm),
        in_specs=[
            pl.BlockSpec((1, tm, D_MODEL), lambda bi, i: (bi, i, 0)),
            pl.BlockSpec((1, n_mem, D_MODEL), lambda bi, i: (bi, 0, 0)),
            pl.BlockSpec((1, n_mem, D_MODEL), lambda bi, i: (bi, 0, 0)),
            wspec, wspec, row, row,
        ],
        out_specs=pl.BlockSpec((1, tm, D_MODEL), lambda bi, i: (bi, i, 0)),
        out_shape=jax.ShapeDtypeStruct((bsz, t, D_MODEL), F32),
        compiler_params=_params("parallel", "parallel"),
        name="mem_attention_ln2",
    )(x3d, mk, mv, w_mq, w_mo, g, b)


def _tree(op, xs):
    xs = list(xs)
    while len(xs) > 1:
        xs = [op(xs[i], xs[i + 1]) if i + 1 < len(xs) else xs[i] for i in range(0, len(xs), 2)]
    return xs[0]


def _all_sublanes(op, x):
    for shift in (4, 2, 1):
        x = op(x, pltpu.roll(x, shift, 0))
    return x


def _topk_slabs(slabs, keys, k, big):
    vals, kout = [], []
    for r in range(k):
        m = _all_sublanes(jnp.maximum, _tree(jnp.maximum, slabs))
        km = _all_sublanes(jnp.minimum, _tree(jnp.minimum, [jnp.where(s == m, kk, big) for s, kk in zip(slabs, keys)]))
        vals.append(m)
        kout.append(km)
        if r + 1 < k:
            slabs = [jnp.where(kk == km, -jnp.inf, s) for s, kk in zip(slabs, keys)]
    return vals, kout


def _pack_rows(rows, sub_io):
    out = rows[0]
    for r in range(1, SUBLANES):
        out = jnp.where(sub_io == r, rows[r], out)
    return out


HALF_EXPERTS = PEER_EXPERTS // 2
HALF_SHIFT = HALF_EXPERTS.bit_length() - 1
HIGH_MASK = -65536
ROUTE_HEADS_PER_STEP = 8


def _route_kernel(x_ref, w_ref, ka_ref, kb_ref, off_ref, sh_ref, g_ref, q_scr):
    tm = x_ref.shape[0]
    xb = x_ref[...].astype(BF16)
    for c in range(2 * PEER_HEADS):
        q_scr[c] = jnp.dot(xb, w_ref[:, c * PEER_HALF:(c + 1) * PEER_HALF],
                           preferred_element_type=F32).astype(BF16)
    assert PEER_TOPK == 2 * SUBLANES
    nt = (((1,), (1,)), ((), ()))
    sub_io = lax.broadcasted_iota(I32, (SUBLANES, tm), 0)
    sub_f = sub_io.astype(F32)
    n_slabs = PEER_NKEYS // SUBLANES
    key_slabs = [sub_f + float(SUBLANES * i) for i in range(n_slabs)]
    experts = float(PEER_EXPERTS)
    big = float(PEER_TOPK * PEER_TOPK) * experts

    def head(h):
        sa = lax.dot_general(ka_ref[h], q_scr[2 * h], nt, preferred_element_type=F32)
        sb = lax.dot_general(kb_ref[h], q_scr[2 * h + 1], nt, preferred_element_type=F32)
        slabs = lambda s: [s[SUBLANES * i:SUBLANES * (i + 1)] for i in range(n_slabs)]
        va, ia = _topk_slabs(slabs(sa), key_slabs, PEER_TOPK, float(PEER_NKEYS))
        vb, ib = _topk_slabs(slabs(sb), key_slabs, PEER_TOPK, float(PEER_NKEYS))
        va_hi, ia_hi = _pack_rows(va[SUBLANES:], sub_io), _pack_rows(ia[SUBLANES:], sub_io)
        vb_lo, ib_lo = _pack_rows(vb[:SUBLANES], sub_io), _pack_rows(ib[:SUBLANES], sub_io)
        vb_hi, ib_hi = _pack_rows(vb[SUBLANES:], sub_io), _pack_rows(ib[SUBLANES:], sub_io)
        cand = [va[0] + vb_lo, va[0] + vb_hi]
        ckey = [sub_f * experts + (ia[0] * PEER_NKEYS + ib_lo),
                (sub_f + SUBLANES) * experts + (ia[0] * PEER_NKEYS + ib_hi)]
        for i in range(1, SUBLANES):
            cand.append(va[i] + vb_lo)
            ckey.append((sub_f + float(i * PEER_TOPK)) * experts + (ia[i] * PEER_NKEYS + ib_lo))
        cand.append(va_hi + vb[0])
        ckey.append((sub_f + SUBLANES) * (PEER_TOPK * experts) + (ia_hi * PEER_NKEYS + ib[0]))
        top, tkey = _topk_slabs(cand, ckey, PEER_TOPK, big)
        r0 = pl.multiple_of(h * PEER_TOPK, SUBLANES)
        es = [jnp.exp(_pack_rows(top[SUBLANES * j:SUBLANES * (j + 1)], sub_io) - top[0]) for j in range(2)]
        denom = _all_sublanes(jnp.add, es[0] + es[1])
        for j in range(2):
            expert = _pack_rows(tkey[SUBLANES * j:SUBLANES * (j + 1)], sub_io).astype(I32) & (PEER_EXPERTS - 1)
            rows = pl.ds(pl.multiple_of(r0 + SUBLANES * j, SUBLANES), SUBLANES)
            off_ref[0, rows, :] = (expert & (HALF_EXPERTS - 1)) * SUBLANES
            sh_ref[0, rows, :] = (expert >> HALF_SHIFT) * 16
            g_ref[0, rows, :] = es[j] / denom

    def head_group(i, carry):
        for k in range(ROUTE_HEADS_PER_STEP):
            head(ROUTE_HEADS_PER_STEP * i + k)
        return carry

    lax.fori_loop(0, PEER_HEADS // ROUTE_HEADS_PER_STEP, head_group, 0)


def _peer_route(x2d, w_pq, keys_a, keys_b, tm):
    m = x2d.shape[0]
    nt = m // tm
    kspec = pl.BlockSpec((PEER_HEADS, PEER_NKEYS, PEER_HALF), lambda i: (0, 0, 0))
    ospec = pl.BlockSpec((1, PEER_PICKS, tm), lambda i: (i, 0, 0))
    return pl.pallas_call(
        _route_kernel,
        grid=(nt,),
        in_specs=[pl.BlockSpec((tm, D_MODEL), lambda i: (i, 0)),
                  pl.BlockSpec((D_MODEL, 2 * PEER_HEADS * PEER_HALF), lambda i: (0, 0)),
                  kspec, kspec],
        out_specs=[ospec, ospec, ospec],
        out_shape=[jax.ShapeDtypeStruct((nt, PEER_PICKS, tm), I32),
                   jax.ShapeDtypeStruct((nt, PEER_PICKS, tm), I32),
                   jax.ShapeDtypeStruct((nt, PEER_PICKS, tm), F32)],
        scratch_shapes=[pltpu.VMEM((2 * PEER_HEADS, tm, PEER_HALF), BF16)],
        compiler_params=_params("parallel"),
        name="peer_route",
    )(x2d, w_pq, keys_a, keys_b)


def _pack_table(t):
    bits = lax.bitcast_convert_type(t.astype(BF16), jnp.uint16).astype(jnp.uint32)
    word = (bits[:HALF_EXPERTS] << 16) | bits[HALF_EXPERTS:]
    return lax.bitcast_convert_type(word, I32).reshape(HALF_EXPERTS * SUBLANES, LANES)


def _table_spec():
    return pl.BlockSpec((HALF_EXPERTS * SUBLANES, LANES), lambda i: (0, 0), pipeline_mode=pl.Buffered(1))


def _table_row(tab_ref, off, shift):
    row = tab_ref[pl.ds(pl.multiple_of(off, SUBLANES), SUBLANES), :]
    return pltpu.bitcast((row << shift) & HIGH_MASK, F32)


def _rows_to_tiles(x_ref, tiles_ref):
    for s in range(SUBLANES):
        tiles_ref[:, s, :] = x_ref[:, s * LANES:(s + 1) * LANES]


def _tiles_to_rows(tiles_ref, y_ref):
    for s in range(SUBLANES):
        y_ref[:, s * LANES:(s + 1) * LANES] = tiles_ref[:, s, :]


BIT_REVERSED = (0, 4, 2, 6, 1, 5, 3, 7)


def _sublane_sums(prods, sub_io):
    def merge(a, b, h):
        low = (sub_io & h) == 0
        if 2 * h == SUBLANES:
            return jnp.where(low, a, b) + pltpu.roll(jnp.where(low, b, a), h, 0)
        return jnp.where(low, a, pltpu.roll(b, h, 0)) + jnp.where(low, pltpu.roll(a, SUBLANES - h, 0), b)

    p = [prods[BIT_REVERSED[k]] for k in range(SUBLANES)]
    t = [merge(p[2 * k], p[2 * k + 1], 4) for k in range(4)]
    u = [merge(t[2 * k], t[2 * k + 1], 2) for k in range(2)]
    return merge(u[0], u[1], 1)


def _peer_in_kernel(off_s, sh_s, x_ref, sh_ref, g_ref, spread_ref, tab_ref, w_ref, part_ref, xt_ref):
    tm = x_ref.shape[0]
    sub_io = lax.broadcasted_iota(I32, (SUBLANES, LANES), 0)
    tok_io = lax.broadcasted_iota(I32, (PEER_PICKS, tm), 1)
    _rows_to_tiles(x_ref, xt_ref)

    def gather(t):
        x = xt_ref[t]
        for g8 in range(PEER_PICKS // SUBLANES):
            picks = [g8 * SUBLANES + s for s in range(SUBLANES)]
            prods = [x * _table_row(tab_ref, off_s[0, t, p], sh_s[0, t, p]) for p in picks]
            part_ref[t, g8 * SUBLANES:(g8 + 1) * SUBLANES, :] = _sublane_sums(prods, sub_io)

    def fold(t, h_t):
        col = jnp.sum(part_ref[t], axis=1, keepdims=True)
        return jnp.where(tok_io == t, col, h_t)

    def step(t, h_t):
        h_t = fold(t - 1, h_t)
        gather(t)
        return h_t

    gather(0)
    h_t = lax.fori_loop(1, tm, step, jnp.zeros((PEER_PICKS, tm), F32))
    h_t = fold(tm - 1, h_t)
    gelu = 0.5 * h_t * (1.0 + lax.erf(h_t * (2.0 ** -0.5)))
    w = (gelu * g_ref[0]).astype(BF16)
    tn = (((0,), (0,)), ((), ()))
    spread = spread_ref[...]
    w_rows = lax.dot_general(w, spread, tn, preferred_element_type=F32)
    low_half = lax.dot_general((sh_ref[0] >> 4).astype(BF16), spread, tn, preferred_element_type=F32)
    odd_row = (lax.broadcasted_iota(I32, (1, PACKED_ROWS * PEER_PICKS), 1) & 1).astype(F32)
    w_ref[...] = jnp.where(low_half + odd_row == 1.0, w_rows, 0.0)


OUT_TOKENS_PER_STEP = 8
PACKED_ROWS = 2 * SUBLANES
SPREAD_COLS = PACKED_ROWS * PEER_PICKS


def _peer_in(off, sh, x2d, gate, table, tm):
    nt = off.shape[0]
    m = x2d.shape[0]
    assert tm == LANES
    tspec = pl.BlockSpec((1, PEER_PICKS, tm), lambda i: (i, 0, 0))
    sspec = pl.BlockSpec((1, tm, PEER_PICKS), lambda i: (i, 0, 0), memory_space=pltpu.SMEM)
    pick = lax.broadcasted_iota(I32, (PEER_PICKS, SPREAD_COLS), 0)
    col = lax.broadcasted_iota(I32, (PEER_PICKS, SPREAD_COLS), 1)
    spread = (col // PACKED_ROWS == pick).astype(BF16)
    return pl.pallas_call(
        _peer_in_kernel,
        grid=(nt,),
        in_specs=[sspec, sspec,
                  pl.BlockSpec((tm, D_MODEL), lambda i: (i, 0)),
                  tspec, tspec,
                  pl.BlockSpec((PEER_PICKS, SPREAD_COLS), lambda i: (0, 0)),
                  _table_spec()],
        out_specs=pl.BlockSpec((tm, SPREAD_COLS), lambda i: (i, 0)),
        out_shape=jax.ShapeDtypeStruct((m, SPREAD_COLS), F32),
        scratch_shapes=[pltpu.VMEM((tm, PEER_PICKS, LANES), F32), pltpu.VMEM((tm, SUBLANES, LANES), F32)],
        compiler_params=_params("arbitrary"),
        name="peer_expert_in",
    )(off.transpose(0, 2, 1), sh.transpose(0, 2, 1), x2d, sh, gate, spread, table)


def _peer_out_kernel(off_s, w_ref, x_ref, g_ref, b_ref, tab_ref, y_ref, xt_ref, wt_ref):
    tm = x_ref.shape[0]
    _rows_to_tiles(x_ref, xt_ref)
    chunks = SPREAD_COLS // LANES
    for k in range(chunks):
        wt_ref[:, k, :] = w_ref[:, k * LANES:(k + 1) * LANES]
    lane = lax.broadcasted_iota(I32, (SUBLANES, LANES), 1)
    sub = lax.broadcasted_iota(I32, (SUBLANES, LANES), 0)
    own_chunk = (lane % PACKED_ROWS) // 2 == sub

    def token(t):
        tiles = [pltpu.bitcast(tab_ref[pl.ds(pl.multiple_of(off_s[0, t, p], SUBLANES), SUBLANES), :], BF16)
                 for p in range(PEER_PICKS)]
        wt = wt_ref[t]
        lhs = jnp.concatenate([jnp.where(own_chunk, jnp.broadcast_to(wt[k:k + 1, :], (SUBLANES, LANES)), 0.0)
                               for k in range(chunks)], axis=1).astype(BF16)
        out = jnp.dot(lhs, jnp.concatenate(tiles, axis=0), preferred_element_type=F32)
        xt_ref[t] = DN_ALPHA * xt_ref[t] + out

    def token_group(i, carry):
        for k in range(OUT_TOKENS_PER_STEP):
            token(i * OUT_TOKENS_PER_STEP + k)
        return carry

    lax.fori_loop(0, tm // OUT_TOKENS_PER_STEP, token_group, 0)
    _tiles_to_rows(xt_ref, y_ref)
    y_ref[...] = _layer_norm_rows(y_ref[...], g_ref[...], b_ref[...])


def _peer_out(off, w_rows, x2d, g, b, table, tm):
    nt = off.shape[0]
    m = x2d.shape[0]
    sspec = pl.BlockSpec((1, tm, PEER_PICKS), lambda i: (i, 0, 0), memory_space=pltpu.SMEM)
    vec = pl.BlockSpec((1, D_MODEL), lambda i: (0, 0))
    xspec = pl.BlockSpec((tm, D_MODEL), lambda i: (i, 0))
    return pl.pallas_call(
        _peer_out_kernel,
        grid=(nt,),
        in_specs=[sspec, pl.BlockSpec((tm, SPREAD_COLS), lambda i: (i, 0)), xspec, vec, vec, _table_spec()],
        out_specs=xspec,
        out_shape=jax.ShapeDtypeStruct((m, D_MODEL), F32),
        scratch_shapes=[pltpu.VMEM((tm, SUBLANES, LANES), F32),
                        pltpu.VMEM((tm, SPREAD_COLS // LANES, LANES), F32)],
        compiler_params=_params("arbitrary"),
        name="peer_expert_out",
    )(off.transpose(0, 2, 1), w_rows, x2d, g, b, table)


def _pick_tile(n, pref):
    t = pref
    while n % t:
        t //= 2
    return t


def _layer(x, past, mem_k, mem_v, wts):
    b, t, _ = x.shape
    m = b * t
    x2d = x.reshape(m, D_MODEL)
    tm = _pick_tile(m, 256)

    (qsb, ksb, vsb, qfx, kfx, vfx, ksbb, vsbb, kfxb, vfxb, lf) = _in_projection(
        x2d, wts["w_in_main"], wts["w_in_f"], wts["b_f"], tm)
    state = tuple(a.reshape(b, t, N_HEADS, HEAD_DIM) for a in (ksb, vsb, kfx, vfx)) + (lf.reshape(b, t, N_HEADS),)

    r3 = lambda a: a.reshape(b, t, GROUP_WIDTH)
    tq = min(QUERY_BLOCK, t)
    kblk = FOX_TILE // tq
    if past is None:
        p = 0
        k_sb, v_sb, k_fx, v_fx = r3(ksbb), r3(vsbb), r3(kfxb), r3(vfxb)
        lf_all = lf.reshape(b, t, N_HEADS)
    else:
        p = past[0].shape[1]
        pad = (-(p + t)) % kblk

        def cat(c, new):
            parts = [c.reshape(b, p, GROUP_WIDTH).astype(BF16), r3(new)]
            if pad:
                parts.append(jnp.zeros((b, pad, GROUP_WIDTH), BF16))
            return jnp.concatenate(parts, axis=1)

        k_sb, v_sb, k_fx, v_fx = cat(past[0], ksbb), cat(past[1], vsbb), cat(past[2], kfxb), cat(past[3], vfxb)
        parts = [past[4].astype(F32), lf.reshape(b, t, N_HEADS)]
        if pad:
            parts.append(jnp.zeros((b, pad, N_HEADS), F32))
        lf_all = jnp.concatenate(parts, axis=1)

    c_all = _forget_cumsum(lf_all)
    lk = c_all.shape[0]
    cq = c_all[p:p + t].reshape(t, b, N_HEADS).transpose(1, 0, 2)
    ck = c_all.T.reshape(b, N_HEADS // 2, 2, lk // kblk, kblk).transpose(0, 1, 3, 2, 4)

    assert kblk % tq == 0 and t % tq == 0 and p % tq == 0 and lk % kblk == 0 and kblk % KEY_BLOCK == 0
    o_sb = _stick_breaking_attention(r3(qsb), k_sb, v_sb, tq, p)
    o_fx = _forgetting_attention(r3(qfx), k_fx, v_fx, cq, ck, tq, p)

    x1 = _mix_out(o_sb.reshape(m, GROUP_WIDTH), o_fx.reshape(m, GROUP_WIDTH), x2d,
                  wts["w_gn"], wts["w_out"], wts["ln1_g"], wts["ln1_b"], tm)
    x2 = _mem_attention(x1.reshape(b, t, D_MODEL), mem_k, mem_v, wts["w_mq"], wts["w_mo"],
                        wts["ln2_g"], wts["ln2_b"], _pick_tile(t, 256))
    x2d2 = x2.reshape(m, D_MODEL)

    tr = LANES
    off, sh, gate = _peer_route(x2d2, wts["w_pq"], wts["keys_a"], wts["keys_b"], tr)
    w_rows = _peer_in(off, sh, x2d2, gate, wts["table_u"], tr)
    y = _peer_out(off, w_rows, x2d2, wts["ln3_g"], wts["ln3_b"], wts["table_v"], tr)
    return y.reshape(b, t, D_MODEL), state


def kernel(x_prompt, x_sample, mem_prompt, cache_sb_k, cache_sb_v, cache_fox_k, cache_fox_v, cache_fox_logf,
           cache_mem_k, cache_mem_v, w_in, b_f, w_gn, w_out, ln1_g, ln1_b, w_mq, w_mk, w_mv, w_mo, ln2_g, ln2_b,
           w_pq, peer_keys_a, peer_keys_b, peer_u, peer_v, ln3_g, ln3_b):
    depth = w_in.shape[0]
    hp, hs = x_prompt, x_sample
    bp = x_prompt.shape[0]
    n_mem = mem_prompt.shape[1]
    mix_cols = 6 * GROUP_WIDTH
    outs_p = [[] for _ in range(7)]
    outs_s = [[] for _ in range(5)]
    row = lambda a: a.reshape(1, D_MODEL)
    for l in range(depth):
        wts = {
            "w_in_main": w_in[l][:, :mix_cols].astype(BF16),
            "w_in_f": jnp.pad(w_in[l][:, mix_cols:], ((0, 0), (0, LANES - N_HEADS))).astype(BF16),
            "b_f": jnp.pad(b_f[l], (0, LANES - N_HEADS)).reshape(1, LANES),
            "w_gn": row(w_gn[l]), "w_out": w_out[l].astype(BF16),
            "ln1_g": row(ln1_g[l]), "ln1_b": row(ln1_b[l]),
            "w_mq": w_mq[l].astype(BF16), "w_mo": w_mo[l].astype(BF16),
            "ln2_g": row(ln2_g[l]), "ln2_b": row(ln2_b[l]),
            "w_pq": w_pq[l].astype(BF16),
            "keys_a": peer_keys_a[l].astype(BF16), "keys_b": peer_keys_b[l].astype(BF16),
            "table_u": _pack_table(peer_u[l]), "table_v": _pack_table(peer_v[l]),
            "ln3_g": row(ln3_g[l]), "ln3_b": row(ln3_b[l]),
        }
        mem2d = mem_prompt.reshape(bp * n_mem, D_MODEL)
        mk_p, mv_p = _mem_kv(mem2d, w_mk[l].astype(BF16), w_mv[l].astype(BF16), _pick_tile(bp * n_mem, 512))
        mk_p = mk_p.reshape(bp, n_mem, D_MODEL)
        mv_p = mv_p.reshape(bp, n_mem, D_MODEL)
        hp, st_p = _layer(hp, None, mk_p, mv_p, wts)
        bs = x_sample.shape[0]
        past = (cache_sb_k[l], cache_sb_v[l], cache_fox_k[l], cache_fox_v[l], cache_fox_logf[l])
        hs, st_s = _layer(hs, past, cache_mem_k[l].reshape(bs, -1, D_MODEL), cache_mem_v[l].reshape(bs, -1, D_MODEL), wts)
        for i in range(5):
            outs_p[i].append(st_p[i])
            outs_s[i].append(st_s[i])
        outs_p[5].append(mk_p.reshape(bp, n_mem, MEM_HEADS, MEM_HEAD_DIM))
        outs_p[6].append(mv_p.reshape(bp, n_mem, MEM_HEADS, MEM_HEAD_DIM))
    stack = lambda xs: jnp.stack(xs)
    return (hp, hs) + tuple(stack(o) for o in outs_p) + tuple(stack(o) for o in outs_s)
```

```python
import functools

import jax
import jax.numpy as jnp
from jax import lax
from jax.experimental import pallas as pl
from jax.experimental.pallas import tpu as pltpu

F32 = jnp.float32
BF16 = jnp.bfloat16
I32 = jnp.int32

D_MODEL = 1024
HEAD_DIM = 64
N_HEADS = 8
GROUP_WIDTH = N_HEADS * HEAD_DIM
MEM_HEADS = 4
MEM_HEAD_DIM = D_MODEL // MEM_HEADS
PEER_HEADS = 8
PEER_NKEYS = 128
PEER_TOPK = 16
PEER_HALF = 128
PEER_PICKS = PEER_HEADS * PEER_TOPK
PEER_EXPERTS = PEER_NKEYS * PEER_NKEYS
DN_ALPHA = 2.0 ** 0.25
LN_EPS = 1e-5
GN_EPS = 1e-6

LANES = 128
SUBLANES = 8
KEY_BLOCK = 128
QUERY_BLOCK = 256
FOX_TILE = 256 * 256
VMEM_LIMIT = 56 * 1024 * 1024


def _params(*sem):
    return pltpu.CompilerParams(dimension_semantics=sem, vmem_limit_bytes=VMEM_LIMIT)


def _log_sigmoid(x):
    return jnp.minimum(x, 0.0) - jnp.log1p(jnp.exp(-jnp.abs(x)))


def _layer_norm_rows(r, g, b):
    mu = jnp.mean(r, axis=-1, keepdims=True)
    d = r - mu
    var = jnp.mean(d * d, axis=-1, keepdims=True)
    return d * lax.rsqrt(var + LN_EPS) * g + b


def _inproj_kernel(x_ref, w_ref, wf_ref, bf_ref,
                   qsb_ref, ksb_ref, vsb_ref, qfx_ref, kfx_ref, vfx_ref,
                   ksbb_ref, vsbb_ref, kfxb_ref, vfxb_ref, lf_ref):
    xb = x_ref[...].astype(BF16)

    def proj(j):
        return jnp.dot(xb, w_ref[:, j * GROUP_WIDTH:(j + 1) * GROUP_WIDTH], preferred_element_type=F32)

    scale = HEAD_DIM ** -0.5
    qsb_ref[...] = (proj(0) * scale).astype(BF16)
    k = proj(1)
    ksb_ref[...] = k
    ksbb_ref[...] = k.astype(BF16)
    v = proj(2)
    vsb_ref[...] = v
    vsbb_ref[...] = v.astype(BF16)
    qfx_ref[...] = (proj(3) * scale).astype(BF16)
    k = proj(4)
    kfx_ref[...] = k
    kfxb_ref[...] = k.astype(BF16)
    v = proj(5)
    vfx_ref[...] = v
    vfxb_ref[...] = v.astype(BF16)
    f = jnp.dot(xb, wf_ref[...], preferred_element_type=F32) + bf_ref[...]
    lf_ref[...] = _log_sigmoid(f)[:, :N_HEADS]


def _in_projection(x2d, w_main, w_f, b_f, tm):
    m = x2d.shape[0]
    f32o = jax.ShapeDtypeStruct((m, GROUP_WIDTH), F32)
    bf16o = jax.ShapeDtypeStruct((m, GROUP_WIDTH), BF16)
    blk = pl.BlockSpec((tm, GROUP_WIDTH), lambda i: (i, 0))
    return pl.pallas_call(
        _inproj_kernel,
        grid=(m // tm,),
        in_specs=[
            pl.BlockSpec((tm, D_MODEL), lambda i: (i, 0)),
            pl.BlockSpec((D_MODEL, 6 * GROUP_WIDTH), lambda i: (0, 0)),
            pl.BlockSpec((D_MODEL, LANES), lambda i: (0, 0)),
            pl.BlockSpec((1, LANES), lambda i: (0, 0)),
        ],
        out_specs=[blk] * 10 + [pl.BlockSpec((tm, N_HEADS), lambda i: (i, 0))],
        out_shape=[bf16o, f32o, f32o, bf16o, f32o, f32o, bf16o, bf16o, bf16o, bf16o,
                   jax.ShapeDtypeStruct((m, N_HEADS), F32)],
        compiler_params=_params("parallel"),
        name="in_projection",
    )(x2d, w_main, w_f, b_f)


def _cumsum_kernel(lf_ref, tri_ref, c_ref):
    l, cols = lf_ref.shape

    def chunk(i, carry):
        r0 = pl.multiple_of(i * KEY_BLOCK, KEY_BLOCK)
        v = lf_ref[pl.ds(r0, KEY_BLOCK), :]
        hi = v.astype(BF16)
        r1 = v - hi.astype(F32)
        mid = r1.astype(BF16)
        lo = (r1 - mid.astype(F32)).astype(BF16)
        parts = jnp.concatenate([hi, mid, lo], axis=1)
        s = jnp.dot(tri_ref[...], parts, preferred_element_type=F32)
        c = s[:, :cols] + s[:, cols:2 * cols] + s[:, 2 * cols:] + carry
        c_ref[pl.ds(r0, KEY_BLOCK), :] = c
        return c[KEY_BLOCK - 1:, :]

    lax.fori_loop(0, l // KEY_BLOCK, chunk, jnp.zeros((1, cols), F32))


def _forget_cumsum(lf):
    b, l, _ = lf.shape
    cols = b * N_HEADS
    r = lax.broadcasted_iota(I32, (KEY_BLOCK, KEY_BLOCK), 0)
    c = lax.broadcasted_iota(I32, (KEY_BLOCK, KEY_BLOCK), 1)
    tri = (c <= r).astype(BF16)
    return pl.pallas_call(
        _cumsum_kernel,
        grid=(1,),
        in_specs=[pl.BlockSpec((l, cols), lambda i: (0, 0)),
                  pl.BlockSpec((KEY_BLOCK, KEY_BLOCK), lambda i: (0, 0))],
        out_specs=pl.BlockSpec((l, cols), lambda i: (0, 0)),
        out_shape=jax.ShapeDtypeStruct((l, cols), F32),
        compiler_params=_params("arbitrary"),
        name="forget_cumsum",
    )(lf.transpose(1, 0, 2).reshape(l, cols), tri)


def _head_masks(width):
    lane = lax.broadcasted_iota(I32, (1, width), 1)
    return lane < HEAD_DIM


def _sb_kernel(q_ref, k_ref, v_ref, kd_ref, vd_ref, tri_ref, o_ref, *, tq, past):
    i = pl.program_id(2)
    q = q_ref[0]
    first = _head_masks(LANES)
    zero_q = jnp.zeros_like(q)
    qh = (jnp.where(first, q, zero_q), jnp.where(first, zero_q, q))
    q_pos0 = past + i * tq
    diag = q_pos0 // KEY_BLOCK
    n_diag = kd_ref.shape[1] // KEY_BLOCK
    tri = tri_ref[...]

    def block(j, run, acc, masked):
        if masked:
            kb = kd_ref[0, j * KEY_BLOCK:(j + 1) * KEY_BLOCK, :]
            vb = vd_ref[0, j * KEY_BLOCK:(j + 1) * KEY_BLOCK, :]
            kpos = j * KEY_BLOCK + lax.broadcasted_iota(I32, (tq, KEY_BLOCK), 1)
            qpos = lax.broadcasted_iota(I32, (tq, KEY_BLOCK), 0)
            mask = kpos < qpos
        else:
            k0 = pl.multiple_of(j * KEY_BLOCK, KEY_BLOCK)
            kb = k_ref[0, pl.ds(k0, KEY_BLOCK), :].astype(BF16)
            vb = v_ref[0, pl.ds(k0, KEY_BLOCK), :].astype(BF16)
        ws = []
        new_run = []
        for h in range(2):
            z = lax.dot_general(qh[h], kb, (((1,), (1,)), ((), ())), preferred_element_type=F32)
            sp = jnp.maximum(z, 0.0) + jnp.log(1.0 + jnp.exp(-jnp.abs(z)))
            log_beta = z - sp
            if masked:
                sp = jnp.where(mask, sp, 0.0)
            hi = sp.astype(BF16)
            lo = (sp - hi.astype(F32)).astype(BF16)
            c = jnp.dot(jnp.concatenate([hi, lo], axis=1), tri, preferred_element_type=F32)
            w = jnp.exp(log_beta + c[:, :KEY_BLOCK] + run[h])
            if masked:
                w = jnp.where(mask, w, 0.0)
            new_run.append(run[h] + c[:, KEY_BLOCK:])
            ws.append(w.astype(BF16))
        zero_v = jnp.zeros_like(vb)
        v2 = jnp.concatenate([jnp.where(first, vb, zero_v), jnp.where(first, zero_v, vb)], axis=0)
        acc = acc + jnp.dot(jnp.concatenate(ws, axis=1), v2, preferred_element_type=F32)
        return tuple(new_run), acc

    def alive(run):
        return (jnp.max(jnp.maximum(run[0], run[1])) > EXP_UNDERFLOW).astype(I32)

    zeros = jnp.zeros((tq, KEY_BLOCK), F32)
    run, acc = (zeros, zeros), jnp.zeros((tq, LANES), F32)
    for d in reversed(range(n_diag)):
        run, acc = block(d, run, acc, True)

    def cond(state):
        it, live, _, _ = state
        return (it < diag) & (live > 0)

    def body(state):
        it, _, run, acc = state
        run, acc = block(diag - 1 - it, run, acc, False)
        return it + 1, alive(run), run, acc

    _, _, _, acc = lax.while_loop(cond, body, (jnp.int32(0), alive(run), run, acc))
    o_ref[0] = acc


EXP_UNDERFLOW = -105.0


def _cumsum_rhs():
    r = lax.broadcasted_iota(I32, (2 * KEY_BLOCK, 2 * KEY_BLOCK), 0) % KEY_BLOCK
    c = lax.broadcasted_iota(I32, (2 * KEY_BLOCK, 2 * KEY_BLOCK), 1)
    return -((c >= KEY_BLOCK) | (r > c)).astype(BF16)


def _stick_breaking_attention(q, k, v, kd, vd, tq, wd, past):
    b, t, _ = q.shape
    lk = k.shape[1]
    pairs = GROUP_WIDTH // LANES
    main = pl.BlockSpec((1, lk, LANES), lambda bi, hp, i: (bi, 0, hp))
    diag = pl.BlockSpec((1, wd, LANES), lambda bi, hp, i: (bi, i, hp))
    return pl.pallas_call(
        functools.partial(_sb_kernel, tq=tq, past=past),
        grid=(b, pairs, t // tq),
        in_specs=[
            pl.BlockSpec((1, tq, LANES), lambda bi, hp, i: (bi, i, hp)),
            main, main, diag, diag,
            pl.BlockSpec((2 * KEY_BLOCK, 2 * KEY_BLOCK), lambda bi, hp, i: (0, 0)),
        ],
        out_specs=pl.BlockSpec((1, tq, LANES), lambda bi, hp, i: (bi, i, hp)),
        out_shape=jax.ShapeDtypeStruct((b, t, GROUP_WIDTH), F32),
        compiler_params=_params("parallel", "parallel", "arbitrary"),
        name="stick_breaking_attention",
    )(q, k, v, kd, vd, _cumsum_rhs())


def _fox_kernel(q_ref, k_ref, v_ref, kd_ref, vd_ref, cq_ref, ck_ref, ckd_ref, o_ref, *, tq, past):
    i = pl.program_id(2)
    hp = pl.program_id(1)
    q = q_ref[0]
    first = _head_masks(LANES)
    zero_q = jnp.zeros_like(q)
    qh = (jnp.where(first, q, zero_q), jnp.where(first, zero_q, q))
    kblk = ck_ref.shape[-1]
    wd = kd_ref.shape[1]
    diag = (past + i * tq) // kblk
    cq_all = cq_ref[0]
    head_lane = lax.broadcasted_iota(I32, (1, N_HEADS), 1)
    cq = [jnp.sum(jnp.where(head_lane == 2 * hp + h, cq_all, 0.0), axis=1, keepdims=True) for h in range(2)]

    def block(j, carry, masked):
        ms, ls, acc = carry
        if masked:
            kb, vb, ck = kd_ref[0], vd_ref[0], ckd_ref[0, 0, 0]
            mask = lax.broadcasted_iota(I32, (tq, wd), 1) <= lax.broadcasted_iota(I32, (tq, wd), 0)
        else:
            k0 = pl.multiple_of(j * kblk, kblk)
            kb = k_ref[0, pl.ds(k0, kblk), :].astype(BF16)
            vb = v_ref[0, pl.ds(k0, kblk), :].astype(BF16)
            ck = ck_ref[0, 0, j]
        ps, new_m, new_l, scales = [], [], [], []
        for h in range(2):
            s = lax.dot_general(qh[h], kb, (((1,), (1,)), ((), ())), preferred_element_type=F32)
            s = s + cq[h] - ck[h:h + 1, :]
            if masked:
                s = jnp.where(mask, s, -jnp.inf)
            m = jnp.maximum(ms[h], jnp.max(s, axis=1, keepdims=True))
            p = jnp.exp(s - m)
            a = jnp.exp(ms[h] - m)
            new_m.append(m)
            new_l.append(a * ls[h] + jnp.sum(p, axis=1, keepdims=True))
            scales.append(a)
            ps.append(p.astype(BF16))
        zero_v = jnp.zeros_like(vb)
        v2 = jnp.concatenate([jnp.where(first, vb, zero_v), jnp.where(first, zero_v, vb)], axis=0)
        pv = jnp.dot(jnp.concatenate(ps, axis=1), v2, preferred_element_type=F32)
        acc = acc * jnp.where(first, scales[0], scales[1]) + pv
        return (tuple(new_m), tuple(new_l), acc)

    neg = jnp.full((tq, 1), -jnp.inf, F32)
    zero = jnp.zeros((tq, 1), F32)
    carry = block(0, ((neg, neg), (zero, zero), jnp.zeros((tq, LANES), F32)), True)

    def body(it, carry):
        return block(diag - 1 - it, carry, False)

    _, ls, acc = lax.fori_loop(0, diag, body, carry)
    o_ref[0] = acc / jnp.where(first, ls[0], ls[1])


def _forgetting_attention(q, k, v, kd, vd, cq, ck, ckd, tq, wd, past):
    b, t, _ = q.shape
    lk = k.shape[1]
    pairs = GROUP_WIDTH // LANES
    main = pl.BlockSpec((1, lk, LANES), lambda bi, hp, i: (bi, 0, hp))
    diag = pl.BlockSpec((1, wd, LANES), lambda bi, hp, i: (bi, i, hp))
    return pl.pallas_call(
        functools.partial(_fox_kernel, tq=tq, past=past),
        grid=(b, pairs, t // tq),
        in_specs=[
            pl.BlockSpec((1, tq, LANES), lambda bi, hp, i: (bi, i, hp)),
            main, main, diag, diag,
            pl.BlockSpec((1, tq, N_HEADS), lambda bi, hp, i: (bi, i, 0)),
            pl.BlockSpec((1, 1) + ck.shape[2:], lambda bi, hp, i: (bi, hp, 0, 0, 0)),
            pl.BlockSpec((1, 1, 1, 2, wd), lambda bi, hp, i: (bi, hp, i, 0, 0)),
        ],
        out_specs=pl.BlockSpec((1, tq, LANES), lambda bi, hp, i: (bi, i, hp)),
        out_shape=jax.ShapeDtypeStruct((b, t, GROUP_WIDTH), F32),
        compiler_params=_params("parallel", "parallel", "arbitrary"),
        name="forgetting_attention",
    )(q, k, v, kd, vd, cq, ck, ckd)


def _mixout_kernel(osb_ref, ofx_ref, x_ref, gn_ref, w_ref, g_ref, b_ref, y_ref):
    def rms(o, g):
        return o * lax.rsqrt(jnp.mean(o * o, axis=-1, keepdims=True) + GN_EPS) * g

    gn = gn_ref[...]
    o = jnp.concatenate([rms(osb_ref[...], gn[:, :GROUP_WIDTH]), rms(ofx_ref[...], gn[:, GROUP_WIDTH:])], axis=1)
    mix = jnp.dot(o.astype(BF16), w_ref[...], preferred_element_type=F32)
    y_ref[...] = _layer_norm_rows(DN_ALPHA * x_ref[...] + mix, g_ref[...], b_ref[...])


def _mix_out(osb, ofx, x2d, w_gn, w_out, g, b, tm):
    m = x2d.shape[0]
    row = pl.BlockSpec((1, D_MODEL), lambda i: (0, 0))
    return pl.pallas_call(
        _mixout_kernel,
        grid=(m // tm,),
        in_specs=[
            pl.BlockSpec((tm, GROUP_WIDTH), lambda i: (i, 0)),
            pl.BlockSpec((tm, GROUP_WIDTH), lambda i: (i, 0)),
            pl.BlockSpec((tm, D_MODEL), lambda i: (i, 0)),
            row,
            pl.BlockSpec((D_MODEL, D_MODEL), lambda i: (0, 0)),
            row, row,
        ],
        out_specs=pl.BlockSpec((tm, D_MODEL), lambda i: (i, 0)),
        out_shape=jax.ShapeDtypeStruct((m, D_MODEL), F32),
        compiler_params=_params("parallel"),
        name="mix_out_ln1",
    )(osb, ofx, x2d, w_gn, w_out, g, b)


def _memkv_kernel(m_ref, wk_ref, wv_ref, k_ref, v_ref):
    mb = m_ref[...].astype(BF16)
    k_ref[...] = jnp.dot(mb, wk_ref[...], preferred_element_type=F32)
    v_ref[...] = jnp.dot(mb, wv_ref[...], preferred_element_type=F32)


def _mem_kv(mem2d, w_mk, w_mv, tm):
    m = mem2d.shape[0]
    wspec = pl.BlockSpec((D_MODEL, D_MODEL), lambda i: (0, 0))
    blk = pl.BlockSpec((tm, D_MODEL), lambda i: (i, 0))
    out = jax.ShapeDtypeStruct((m, D_MODEL), F32)
    return pl.pallas_call(
        _memkv_kernel,
        grid=(m // tm,),
        in_specs=[blk, wspec, wspec],
        out_specs=[blk, blk],
        out_shape=[out, out],
        compiler_params=_params("parallel"),
        name="mem_kv",
    )(mem2d, w_mk, w_mv)


def _memattn_kernel(x_ref, mk_ref, mv_ref, wq_ref, wo_ref, g_ref, b_ref, y_ref):
    x = x_ref[0]
    q = jnp.dot(x.astype(BF16), wq_ref[...], preferred_element_type=F32)
    qb = (q * (MEM_HEAD_DIM ** -0.5)).astype(BF16)
    mk = mk_ref[0].astype(BF16)
    mv = mv_ref[0].astype(BF16)
    outs = []
    for h in range(MEM_HEADS):
        sl = slice(h * MEM_HEAD_DIM, (h + 1) * MEM_HEAD_DIM)
        s = lax.dot_general(qb[:, sl], mk[:, sl], (((1,), (1,)), ((), ())), preferred_element_type=F32)
        p = jnp.exp(s - jnp.max(s, axis=1, keepdims=True))
        o = jnp.dot(p.astype(BF16), mv[:, sl], preferred_element_type=F32)
        outs.append(o / jnp.sum(p, axis=1, keepdims=True))
    o = jnp.concatenate(outs, axis=1).astype(BF16)
    att = jnp.dot(o, wo_ref[...], preferred_element_type=F32)
    y_ref[0] = _layer_norm_rows(DN_ALPHA * x + att, g_ref[...], b_ref[...])


def _mem_attention(x3d, mk, mv, w_mq, w_mo, g, b, tm):
    bsz, t, _ = x3d.shape
    n_mem = mk.shape[1]
    row = pl.BlockSpec((1, D_MODEL), lambda bi, i: (0, 0))
    wspec = pl.BlockSpec((D_MODEL, D_MODEL), lambda bi, i: (0, 0))
    return pl.pallas_call(
        _memattn_kernel,
        grid=(bsz, t // tm),
        in_specs=[
            pl.BlockSpec((1, tm, D_MODEL), lambda bi, i: (bi, i, 0)),
            pl.BlockSpec((1, n_mem, D_MODEL), lambda bi, i: (bi, 0, 0)),
            pl.BlockSpec((1, n_mem, D_MODEL), lambda bi, i: (bi, 0, 0)),
            wspec, wspec, row, row,
        ],
        out_specs=pl.BlockSpec((1, tm, D_MODEL), lambda bi, i: (bi, i, 0)),
        out_shape=jax.ShapeDtypeStruct((bsz, t, D_MODEL), F32),
        compiler_params=_params("parallel", "parallel"),
        name="mem_attention_ln2",
    )(x3d, mk, mv, w_mq, w_mo, g, b)


def _tree(op, xs):
    xs = list(xs)
    while len(xs) > 1:
        xs = [op(xs[i], xs[i + 1]) if i + 1 < len(xs) else xs[i] for i in range(0, len(xs), 2)]
    return xs[0]


def _all_sublanes(op, x):
    for shift in (4, 2, 1):
        x = op(x, pltpu.roll(x, shift, 0))
    return x


def _topk_slabs(slabs, keys, k, big):
    vals, kout = [], []
    for r in range(k):
        m = _all_sublanes(jnp.maximum, _tree(jnp.maximum, slabs))
        km = _all_sublanes(jnp.minimum, _tree(jnp.minimum, [jnp.where(s == m, kk, big) for s, kk in zip(slabs, keys)]))
        vals.append(m)
        kout.append(km)
        if r + 1 < k:
            slabs = [jnp.where(kk == km, -jnp.inf, s) for s, kk in zip(slabs, keys)]
    return vals, kout


def _pack_rows(rows, sub_io):
    out = rows[0]
    for r in range(1, SUBLANES):
        out = jnp.where(sub_io == r, rows[r], out)
    return out


HALF_EXPERTS = PEER_EXPERTS // 2
HALF_SHIFT = HALF_EXPERTS.bit_length() - 1
HIGH_MASK = -65536
ROUTE_HEADS_PER_STEP = 8


def _route_kernel(x_ref, w_ref, ka_ref, kb_ref, off_ref, sh_ref, g_ref, q_scr):
    tm = x_ref.shape[0]
    xb = x_ref[...].astype(BF16)
    for c in range(2 * PEER_HEADS):
        q_scr[c] = jnp.dot(xb, w_ref[:, c * PEER_HALF:(c + 1) * PEER_HALF],
                           preferred_element_type=F32).astype(BF16)
    assert PEER_TOPK == 2 * SUBLANES
    nt = (((1,), (1,)), ((), ()))
    sub_io = lax.broadcasted_iota(I32, (SUBLANES, tm), 0)
    sub_f = sub_io.astype(F32)
    n_slabs = PEER_NKEYS // SUBLANES
    key_slabs = [sub_f + float(SUBLANES * i) for i in range(n_slabs)]
    experts = float(PEER_EXPERTS)
    big = float(PEER_TOPK * PEER_TOPK) * experts

    def head(h):
        sa = lax.dot_general(ka_ref[h], q_scr[2 * h], nt, preferred_element_type=F32)
        sb = lax.dot_general(kb_ref[h], q_scr[2 * h + 1], nt, preferred_element_type=F32)
        slabs = lambda s: [s[SUBLANES * i:SUBLANES * (i + 1)] for i in range(n_slabs)]
        va, ia = _topk_slabs(slabs(sa), key_slabs, PEER_TOPK, float(PEER_NKEYS))
        vb, ib = _topk_slabs(slabs(sb), key_slabs, PEER_TOPK, float(PEER_NKEYS))
        va_hi, ia_hi = _pack_rows(va[SUBLANES:], sub_io), _pack_rows(ia[SUBLANES:], sub_io)
        vb_lo, ib_lo = _pack_rows(vb[:SUBLANES], sub_io), _pack_rows(ib[:SUBLANES], sub_io)
        vb_hi, ib_hi = _pack_rows(vb[SUBLANES:], sub_io), _pack_rows(ib[SUBLANES:], sub_io)
        cand = [va[0] + vb_lo, va[0] + vb_hi]
        ckey = [sub_f * experts + (ia[0] * PEER_NKEYS + ib_lo),
                (sub_f + SUBLANES) * experts + (ia[0] * PEER_NKEYS + ib_hi)]
        for i in range(1, SUBLANES):
            cand.append(va[i] + vb_lo)
            ckey.append((sub_f + float(i * PEER_TOPK)) * experts + (ia[i] * PEER_NKEYS + ib_lo))
        cand.append(va_hi + vb[0])
        ckey.append((sub_f + SUBLANES) * (PEER_TOPK * experts) + (ia_hi * PEER_NKEYS + ib[0]))
        top, tkey = _topk_slabs(cand, ckey, PEER_TOPK, big)
        r0 = pl.multiple_of(h * PEER_TOPK, SUBLANES)
        es = [jnp.exp(_pack_rows(top[SUBLANES * j:SUBLANES * (j + 1)], sub_io) - top[0]) for j in range(2)]
        denom = _all_sublanes(jnp.add, es[0] + es[1])
        for j in range(2):
            expert = _pack_rows(tkey[SUBLANES * j:SUBLANES * (j + 1)], sub_io).astype(I32) & (PEER_EXPERTS - 1)
            rows = pl.ds(pl.multiple_of(r0 + SUBLANES * j, SUBLANES), SUBLANES)
            off_ref[0, rows, :] = (expert & (HALF_EXPERTS - 1)) * SUBLANES
            sh_ref[0, rows, :] = (expert >> HALF_SHIFT) * 16
            g_ref[0, rows, :] = es[j] / denom

    def head_group(i, carry):
        for k in range(ROUTE_HEADS_PER_STEP):
            head(ROUTE_HEADS_PER_STEP * i + k)
        return carry

    lax.fori_loop(0, PEER_HEADS // ROUTE_HEADS_PER_STEP, head_group, 0)


def _peer_route(x2d, w_pq, keys_a, keys_b, tm):
    m = x2d.shape[0]
    nt = m // tm
    kspec = pl.BlockSpec((PEER_HEADS, PEER_NKEYS, PEER_HALF), lambda i: (0, 0, 0))
    ospec = pl.BlockSpec((1, PEER_PICKS, tm), lambda i: (i, 0, 0))
    return pl.pallas_call(
        _route_kernel,
        grid=(nt,),
        in_specs=[pl.BlockSpec((tm, D_MODEL), lambda i: (i, 0)),
                  pl.BlockSpec((D_MODEL, 2 * PEER_HEADS * PEER_HALF), lambda i: (0, 0)),
                  kspec, kspec],
        out_specs=[ospec, ospec, ospec],
        out_shape=[jax.ShapeDtypeStruct((nt, PEER_PICKS, tm), I32),
                   jax.ShapeDtypeStruct((nt, PEER_PICKS, tm), I32),
                   jax.ShapeDtypeStruct((nt, PEER_PICKS, tm), F32)],
        scratch_shapes=[pltpu.VMEM((2 * PEER_HEADS, tm, PEER_HALF), BF16)],
        compiler_params=_params("parallel"),
        name="peer_route",
    )(x2d, w_pq, keys_a, keys_b)


def _pack_table(t):
    bits = lax.bitcast_convert_type(t.astype(BF16), jnp.uint16).astype(jnp.uint32)
    word = (bits[:HALF_EXPERTS] << 16) | bits[HALF_EXPERTS:]
    return lax.bitcast_convert_type(word, I32).reshape(HALF_EXPERTS * SUBLANES, LANES)


def _table_spec():
    return pl.BlockSpec((HALF_EXPERTS * SUBLANES, LANES), lambda i: (0, 0), pipeline_mode=pl.Buffered(1))


def _table_row(tab_ref, off, shift):
    row = tab_ref[pl.ds(pl.multiple_of(off, SUBLANES), SUBLANES), :]
    return pltpu.bitcast((row << shift) & HIGH_MASK, F32)


def _rows_to_tiles(x_ref, tiles_ref):
    for s in range(SUBLANES):
        tiles_ref[:, s, :] = x_ref[:, s * LANES:(s + 1) * LANES]


def _tiles_to_rows(tiles_ref, y_ref):
    for s in range(SUBLANES):
        y_ref[:, s * LANES:(s + 1) * LANES] = tiles_ref[:, s, :]


BIT_REVERSED = (0, 4, 2, 6, 1, 5, 3, 7)


def _sublane_sums(prods, sub_io):
    def merge(a, b, h):
        low = (sub_io & h) == 0
        if 2 * h == SUBLANES:
            return jnp.where(low, a, b) + pltpu.roll(jnp.where(low, b, a), h, 0)
        return jnp.where(low, a, pltpu.roll(b, h, 0)) + jnp.where(low, pltpu.roll(a, SUBLANES - h, 0), b)

    p = [prods[BIT_REVERSED[k]] for k in range(SUBLANES)]
    t = [merge(p[2 * k], p[2 * k + 1], 4) for k in range(4)]
    u = [merge(t[2 * k], t[2 * k + 1], 2) for k in range(2)]
    return merge(u[0], u[1], 1)


def _peer_in_kernel(off_s, sh_s, x_ref, sh_ref, g_ref, spread_ref, tab_ref, w_ref, part_ref, xt_ref):
    tm = x_ref.shape[0]
    sub_io = lax.broadcasted_iota(I32, (SUBLANES, LANES), 0)
    tok_io = lax.broadcasted_iota(I32, (PEER_PICKS, tm), 1)
    _rows_to_tiles(x_ref, xt_ref)

    def gather(t):
        x = xt_ref[t]
        for g8 in range(PEER_PICKS // SUBLANES):
            picks = [g8 * SUBLANES + s for s in range(SUBLANES)]
            prods = [x * _table_row(tab_ref, off_s[0, t, p], sh_s[0, t, p]) for p in picks]
            part_ref[t, g8 * SUBLANES:(g8 + 1) * SUBLANES, :] = _sublane_sums(prods, sub_io)

    def fold(t, h_t):
        col = jnp.sum(part_ref[t], axis=1, keepdims=True)
        return jnp.where(tok_io == t, col, h_t)

    def step(t, h_t):
        h_t = fold(t - 1, h_t)
        gather(t)
        return h_t

    gather(0)
    h_t = lax.fori_loop(1, tm, step, jnp.zeros((PEER_PICKS, tm), F32))
    h_t = fold(tm - 1, h_t)
    gelu = 0.5 * h_t * (1.0 + lax.erf(h_t * (2.0 ** -0.5)))
    w = (gelu * g_ref[0]).astype(BF16)
    tn = (((0,), (0,)), ((), ()))
    spread = spread_ref[...]
    w_rows = lax.dot_general(w, spread, tn, preferred_element_type=F32)
    low_half = lax.dot_general((sh_ref[0] >> 4).astype(BF16), spread, tn, preferred_element_type=F32)
    odd_row = (lax.broadcasted_iota(I32, (1, PACKED_ROWS * PEER_PICKS), 1) & 1).astype(F32)
    w_ref[...] = jnp.where(low_half + odd_row == 1.0, w_rows, 0.0)


OUT_TOKENS_PER_STEP = 16
PACKED_ROWS = 2 * SUBLANES
SPREAD_COLS = PACKED_ROWS * PEER_PICKS


def _peer_in(off, sh, x2d, gate, table, tm):
    nt = off.shape[0]
    m = x2d.shape[0]
    assert tm == LANES
    tspec = pl.BlockSpec((1, PEER_PICKS, tm), lambda i: (i, 0, 0))
    sspec = pl.BlockSpec((1, tm, PEER_PICKS), lambda i: (i, 0, 0), memory_space=pltpu.SMEM)
    pick = lax.broadcasted_iota(I32, (PEER_PICKS, SPREAD_COLS), 0)
    col = lax.broadcasted_iota(I32, (PEER_PICKS, SPREAD_COLS), 1)
    spread = (col // PACKED_ROWS == pick).astype(BF16)
    return pl.pallas_call(
        _peer_in_kernel,
        grid=(nt,),
        in_specs=[sspec, sspec,
                  pl.BlockSpec((tm, D_MODEL), lambda i: (i, 0)),
                  tspec, tspec,
                  pl.BlockSpec((PEER_PICKS, SPREAD_COLS), lambda i: (0, 0)),
                  _table_spec()],
        out_specs=pl.BlockSpec((tm, SPREAD_COLS), lambda i: (i, 0)),
        out_shape=jax.ShapeDtypeStruct((m, SPREAD_COLS), F32),
        scratch_shapes=[pltpu.VMEM((tm, PEER_PICKS, LANES), F32), pltpu.VMEM((tm, SUBLANES, LANES), F32)],
        compiler_params=_params("arbitrary"),
        name="peer_expert_in",
    )(off.transpose(0, 2, 1), sh.transpose(0, 2, 1), x2d, sh, gate, spread, table)


def _peer_out_kernel(off_s, w_ref, x_ref, g_ref, b_ref, tab_ref, y_ref, xt_ref, wt_ref):
    tm = x_ref.shape[0]
    _rows_to_tiles(x_ref, xt_ref)
    chunks = SPREAD_COLS // LANES
    for k in range(chunks):
        wt_ref[:, k, :] = w_ref[:, k * LANES:(k + 1) * LANES]
    lane = lax.broadcasted_iota(I32, (SUBLANES, LANES), 1)
    sub = lax.broadcasted_iota(I32, (SUBLANES, LANES), 0)
    own_chunk = (lane % PACKED_ROWS) // 2 == sub

    def token(t):
        tiles = [pltpu.bitcast(tab_ref[pl.ds(pl.multiple_of(off_s[0, t, p], SUBLANES), SUBLANES), :], BF16)
                 for p in range(PEER_PICKS)]
        wt = wt_ref[t]
        lhs = jnp.concatenate([jnp.where(own_chunk, jnp.broadcast_to(wt[k:k + 1, :], (SUBLANES, LANES)), 0.0)
                               for k in range(chunks)], axis=1).astype(BF16)
        out = jnp.dot(lhs, jnp.concatenate(tiles, axis=0), preferred_element_type=F32)
        xt_ref[t] = DN_ALPHA * xt_ref[t] + out

    def token_group(i, carry):
        for k in range(OUT_TOKENS_PER_STEP):
            token(i * OUT_TOKENS_PER_STEP + k)
        return carry

    lax.fori_loop(0, tm // OUT_TOKENS_PER_STEP, token_group, 0)
    _tiles_to_rows(xt_ref, y_ref)
    y_ref[...] = _layer_norm_rows(y_ref[...], g_ref[...], b_ref[...])


def _peer_out(off, w_rows, x2d, g, b, table, tm):
    nt = off.shape[0]
    m = x2d.shape[0]
    sspec = pl.BlockSpec((1, tm, PEER_PICKS), lambda i: (i, 0, 0), memory_space=pltpu.SMEM)
    vec = pl.BlockSpec((1, D_MODEL), lambda i: (0, 0))
    xspec = pl.BlockSpec((tm, D_MODEL), lambda i: (i, 0))
    return pl.pallas_call(
        _peer_out_kernel,
        grid=(nt,),
        in_specs=[sspec, pl.BlockSpec((tm, SPREAD_COLS), lambda i: (i, 0)), xspec, vec, vec, _table_spec()],
        out_specs=xspec,
        out_shape=jax.ShapeDtypeStruct((m, D_MODEL), F32),
        scratch_shapes=[pltpu.VMEM((tm, SUBLANES, LANES), F32),
                        pltpu.VMEM((tm, SPREAD_COLS // LANES, LANES), F32)],
        compiler_params=_params("arbitrary"),
        name="peer_expert_out",
    )(off.transpose(0, 2, 1), w_rows, x2d, g, b, table)


def _pick_tile(n, pref):
    t = pref
    while n % t:
        t //= 2
    return t


def _layer(x, past, mem_k, mem_v, wts):
    b, t, _ = x.shape
    m = b * t
    x2d = x.reshape(m, D_MODEL)
    tm = _pick_tile(m, 256)

    (qsb, ksb, vsb, qfx, kfx, vfx, ksbb, vsbb, kfxb, vfxb, lf) = _in_projection(
        x2d, wts["w_in_main"], wts["w_in_f"], wts["b_f"], tm)
    state = tuple(a.reshape(b, t, N_HEADS, HEAD_DIM) for a in (ksb, vsb, kfx, vfx)) + (lf.reshape(b, t, N_HEADS),)

    r3 = lambda a: a.reshape(b, t, GROUP_WIDTH)
    tq = min(QUERY_BLOCK, t)
    kblk = FOX_TILE // tq
    new_kv = [r3(ksbb), r3(vsbb), r3(kfxb), r3(vfxb)]
    lf_new = lf.reshape(b, t, N_HEADS)
    if past is None:
        p, wd = 0, tq
        main_kv, diag_kv, lf_all = new_kv, new_kv, lf_new
    else:
        p = past[0].shape[1]
        wd = -(-t // KEY_BLOCK) * KEY_BLOCK
        assert t == tq
        main_kv = [c.reshape(b, p, GROUP_WIDTH) for c in past[:4]]
        diag_kv = [jnp.pad(a, ((0, 0), (0, wd - t), (0, 0))) for a in new_kv]
        lf_all = jnp.concatenate([past[4].astype(F32), jnp.pad(lf_new, ((0, 0), (0, wd - t), (0, 0)))], axis=1)

    c_all = _forget_cumsum(lf_all)
    lk = main_kv[0].shape[1]
    heads = lambda c, n, w: c.T.reshape(b, N_HEADS // 2, 2, n, w).transpose(0, 1, 3, 2, 4)
    cq = c_all[p:p + t].reshape(t, b, N_HEADS).transpose(1, 0, 2)
    ck = heads(c_all[:lk], lk // kblk, kblk)
    ckd = heads(c_all[p:p + (t // tq) * wd], t // tq, wd)

    assert t % tq == 0 and lk % kblk == 0 and kblk % KEY_BLOCK == 0 and wd % KEY_BLOCK == 0
    assert p % kblk == 0 and (past is not None or tq % kblk == 0)
    o_sb = _stick_breaking_attention(r3(qsb), main_kv[0], main_kv[1], diag_kv[0], diag_kv[1], tq, wd, p)
    o_fx = _forgetting_attention(r3(qfx), main_kv[2], main_kv[3], diag_kv[2], diag_kv[3], cq, ck, ckd, tq, wd, p)

    x1 = _mix_out(o_sb.reshape(m, GROUP_WIDTH), o_fx.reshape(m, GROUP_WIDTH), x2d,
                  wts["w_gn"], wts["w_out"], wts["ln1_g"], wts["ln1_b"], tm)
    x2 = _mem_attention(x1.reshape(b, t, D_MODEL), mem_k, mem_v, wts["w_mq"], wts["w_mo"],
                        wts["ln2_g"], wts["ln2_b"], _pick_tile(t, 256))
    x2d2 = x2.reshape(m, D_MODEL)

    tr = LANES
    off, sh, gate = _peer_route(x2d2, wts["w_pq"], wts["keys_a"], wts["keys_b"], tr)
    w_rows = _peer_in(off, sh, x2d2, gate, wts["table_u"], tr)
    y = _peer_out(off, w_rows, x2d2, wts["ln3_g"], wts["ln3_b"], wts["table_v"], tr)
    return y.reshape(b, t, D_MODEL), state


def kernel(x_prompt, x_sample, mem_prompt, cache_sb_k, cache_sb_v, cache_fox_k, cache_fox_v, cache_fox_logf,
           cache_mem_k, cache_mem_v, w_in, b_f, w_gn, w_out, ln1_g, ln1_b, w_mq, w_mk, w_mv, w_mo, ln2_g, ln2_b,
           w_pq, peer_keys_a, peer_keys_b, peer_u, peer_v, ln3_g, ln3_b):
    depth = w_in.shape[0]
    hp, hs = x_prompt, x_sample
    bp = x_prompt.shape[0]
    n_mem = mem_prompt.shape[1]
    mix_cols = 6 * GROUP_WIDTH
    outs_p = [[] for _ in range(7)]
    outs_s = [[] for _ in range(5)]
    row = lambda a: a.reshape(1, D_MODEL)
    for l in range(depth):
        wts = {
            "w_in_main": w_in[l][:, :mix_cols].astype(BF16),
            "w_in_f": jnp.pad(w_in[l][:, mix_cols:], ((0, 0), (0, LANES - N_HEADS))).astype(BF16),
            "b_f": jnp.pad(b_f[l], (0, LANES - N_HEADS)).reshape(1, LANES),
            "w_gn": row(w_gn[l]), "w_out": w_out[l].astype(BF16),
            "ln1_g": row(ln1_g[l]), "ln1_b": row(ln1_b[l]),
            "w_mq": w_mq[l].astype(BF16), "w_mo": w_mo[l].astype(BF16),
            "ln2_g": row(ln2_g[l]), "ln2_b": row(ln2_b[l]),
            "w_pq": w_pq[l].astype(BF16),
            "keys_a": peer_keys_a[l].astype(BF16), "keys_b": peer_keys_b[l].astype(BF16),
            "table_u": _pack_table(peer_u[l]), "table_v": _pack_table(peer_v[l]),
            "ln3_g": row(ln3_g[l]), "ln3_b": row(ln3_b[l]),
        }
        mem2d = mem_prompt.reshape(bp * n_mem, D_MODEL)
        mk_p, mv_p = _mem_kv(mem2d, w_mk[l].astype(BF16), w_mv[l].astype(BF16), _pick_tile(bp * n_mem, 512))
        mk_p = mk_p.reshape(bp, n_mem, D_MODEL)
        mv_p = mv_p.reshape(bp, n_mem, D_MODEL)
        hp, st_p = _layer(hp, None, mk_p, mv_p, wts)
        bs = x_sample.shape[0]
        past = (cache_sb_k[l], cache_sb_v[l], cache_fox_k[l], cache_fox_v[l], cache_fox_logf[l])
        hs, st_s = _layer(hs, past, cache_mem_k[l].reshape(bs, -1, D_MODEL), cache_mem_v[l].reshape(bs, -1, D_MODEL), wts)
        for i in range(5):
            outs_p[i].append(st_p[i])
            outs_s[i].append(st_s[i])
        outs_p[5].append(mk_p.reshape(bp, n_mem, MEM_HEADS, MEM_HEAD_DIM))
        outs_p[6].append(mv_p.reshape(bp, n_mem, MEM_HEADS, MEM_HEAD_DIM))
    stack = lambda xs: jnp.stack(xs)
    return (hp, hs) + tuple(stack(o) for o in outs_p) + tuple(stack(o) for o in outs_s)
```

```python
import functools

import jax
import jax.numpy as jnp
from jax import lax
from jax.experimental import pallas as pl
from jax.experimental.pallas import tpu as pltpu

F32 = jnp.float32
BF16 = jnp.bfloat16
I32 = jnp.int32

D_MODEL = 1024
HEAD_DIM = 64
N_HEADS = 8
GROUP_WIDTH = N_HEADS * HEAD_DIM
MEM_HEADS = 4
MEM_HEAD_DIM = D_MODEL // MEM_HEADS
PEER_HEADS = 8
PEER_NKEYS = 128
PEER_TOPK = 16
PEER_HALF = 128
PEER_PICKS = PEER_HEADS * PEER_TOPK
PEER_EXPERTS = PEER_NKEYS * PEER_NKEYS
DN_ALPHA = 2.0 ** 0.25
LN_EPS = 1e-5
GN_EPS = 1e-6

LANES = 128
SUBLANES = 8
KEY_BLOCK = 128
QUERY_BLOCK = 256
FOX_TILE = 256 * 256
VMEM_LIMIT = 56 * 1024 * 1024


def _params(*sem):
    return pltpu.CompilerParams(dimension_semantics=sem, vmem_limit_bytes=VMEM_LIMIT)


def _log_sigmoid(x):
    return jnp.minimum(x, 0.0) - jnp.log1p(jnp.exp(-jnp.abs(x)))


def _layer_norm_rows(r, g, b):
    mu = jnp.mean(r, axis=-1, keepdims=True)
    d = r - mu
    var = jnp.mean(d * d, axis=-1, keepdims=True)
    return d * lax.rsqrt(var + LN_EPS) * g + b


def _inproj_kernel(x_ref, w_ref, wf_ref, bf_ref,
                   qsb_ref, ksb_ref, vsb_ref, qfx_ref, kfx_ref, vfx_ref,
                   ksbb_ref, vsbb_ref, kfxb_ref, vfxb_ref, lf_ref):
    xb = x_ref[...].astype(BF16)

    def proj(j):
        return jnp.dot(xb, w_ref[:, j * GROUP_WIDTH:(j + 1) * GROUP_WIDTH], preferred_element_type=F32)

    scale = HEAD_DIM ** -0.5
    qsb_ref[...] = (proj(0) * scale).astype(BF16)
    k = proj(1)
    ksb_ref[...] = k
    ksbb_ref[...] = k.astype(BF16)
    v = proj(2)
    vsb_ref[...] = v
    vsbb_ref[...] = v.astype(BF16)
    qfx_ref[...] = (proj(3) * scale).astype(BF16)
    k = proj(4)
    kfx_ref[...] = k
    kfxb_ref[...] = k.astype(BF16)
    v = proj(5)
    vfx_ref[...] = v
    vfxb_ref[...] = v.astype(BF16)
    f = jnp.dot(xb, wf_ref[...], preferred_element_type=F32) + bf_ref[...]
    lf_ref[...] = _log_sigmoid(f)[:, :N_HEADS]


def _in_projection(x2d, w_main, w_f, b_f, tm):
    m = x2d.shape[0]
    f32o = jax.ShapeDtypeStruct((m, GROUP_WIDTH), F32)
    bf16o = jax.ShapeDtypeStruct((m, GROUP_WIDTH), BF16)
    blk = pl.BlockSpec((tm, GROUP_WIDTH), lambda i: (i, 0))
    return pl.pallas_call(
        _inproj_kernel,
        grid=(m // tm,),
        in_specs=[
            pl.BlockSpec((tm, D_MODEL), lambda i: (i, 0)),
            pl.BlockSpec((D_MODEL, 6 * GROUP_WIDTH), lambda i: (0, 0)),
            pl.BlockSpec((D_MODEL, LANES), lambda i: (0, 0)),
            pl.BlockSpec((1, LANES), lambda i: (0, 0)),
        ],
        out_specs=[blk] * 10 + [pl.BlockSpec((tm, N_HEADS), lambda i: (i, 0))],
        out_shape=[bf16o, f32o, f32o, bf16o, f32o, f32o, bf16o, bf16o, bf16o, bf16o,
                   jax.ShapeDtypeStruct((m, N_HEADS), F32)],
        compiler_params=_params("parallel"),
        name="in_projection",
    )(x2d, w_main, w_f, b_f)


def _cumsum_kernel(lf_ref, tri_ref, c_ref):
    l, cols = lf_ref.shape

    def chunk(i, carry):
        r0 = pl.multiple_of(i * KEY_BLOCK, KEY_BLOCK)
        v = lf_ref[pl.ds(r0, KEY_BLOCK), :]
        hi = v.astype(BF16)
        r1 = v - hi.astype(F32)
        mid = r1.astype(BF16)
        lo = (r1 - mid.astype(F32)).astype(BF16)
        parts = jnp.concatenate([hi, mid, lo], axis=1)
        s = jnp.dot(tri_ref[...], parts, preferred_element_type=F32)
        c = s[:, :cols] + s[:, cols:2 * cols] + s[:, 2 * cols:] + carry
        c_ref[pl.ds(r0, KEY_BLOCK), :] = c
        return c[KEY_BLOCK - 1:, :]

    lax.fori_loop(0, l // KEY_BLOCK, chunk, jnp.zeros((1, cols), F32))


def _forget_cumsum(lf):
    b, l, _ = lf.shape
    cols = b * N_HEADS
    r = lax.broadcasted_iota(I32, (KEY_BLOCK, KEY_BLOCK), 0)
    c = lax.broadcasted_iota(I32, (KEY_BLOCK, KEY_BLOCK), 1)
    tri = (c <= r).astype(BF16)
    return pl.pallas_call(
        _cumsum_kernel,
        grid=(1,),
        in_specs=[pl.BlockSpec((l, cols), lambda i: (0, 0)),
                  pl.BlockSpec((KEY_BLOCK, KEY_BLOCK), lambda i: (0, 0))],
        out_specs=pl.BlockSpec((l, cols), lambda i: (0, 0)),
        out_shape=jax.ShapeDtypeStruct((l, cols), F32),
        compiler_params=_params("arbitrary"),
        name="forget_cumsum",
    )(lf.transpose(1, 0, 2).reshape(l, cols), tri)


def _head_masks(width):
    lane = lax.broadcasted_iota(I32, (1, width), 1)
    return lane < HEAD_DIM


def _sb_kernel(q_ref, k_ref, v_ref, kd_ref, vd_ref, tri_ref, o_ref, *, tq, past):
    i = pl.program_id(2)
    q = q_ref[0]
    first = _head_masks(LANES)
    zero_q = jnp.zeros_like(q)
    qh = (jnp.where(first, q, zero_q), jnp.where(first, zero_q, q))
    q_pos0 = past + i * tq
    diag = q_pos0 // KEY_BLOCK
    n_diag = kd_ref.shape[1] // KEY_BLOCK
    tri = tri_ref[...]

    def block(j, run, acc, masked):
        if masked:
            kb = kd_ref[0, j * KEY_BLOCK:(j + 1) * KEY_BLOCK, :]
            vb = vd_ref[0, j * KEY_BLOCK:(j + 1) * KEY_BLOCK, :]
            kpos = j * KEY_BLOCK + lax.broadcasted_iota(I32, (tq, KEY_BLOCK), 1)
            qpos = lax.broadcasted_iota(I32, (tq, KEY_BLOCK), 0)
            mask = kpos < qpos
        else:
            k0 = pl.multiple_of(j * KEY_BLOCK, KEY_BLOCK)
            kb = k_ref[0, pl.ds(k0, KEY_BLOCK), :].astype(BF16)
            vb = v_ref[0, pl.ds(k0, KEY_BLOCK), :].astype(BF16)
        ws = []
        new_run = []
        for h in range(2):
            z = lax.dot_general(qh[h], kb, (((1,), (1,)), ((), ())), preferred_element_type=F32)
            sp = jnp.maximum(z, 0.0) + jnp.log(1.0 + jnp.exp(-jnp.abs(z)))
            log_beta = z - sp
            if masked:
                sp = jnp.where(mask, sp, 0.0)
            hi = sp.astype(BF16)
            lo = (sp - hi.astype(F32)).astype(BF16)
            c = jnp.dot(jnp.concatenate([hi, lo], axis=1), tri, preferred_element_type=F32)
            w = jnp.exp(log_beta + c[:, :KEY_BLOCK] + run[h])
            if masked:
                w = jnp.where(mask, w, 0.0)
            new_run.append(run[h] + c[:, KEY_BLOCK:])
            ws.append(w.astype(BF16))
        zero_v = jnp.zeros_like(vb)
        v2 = jnp.concatenate([jnp.where(first, vb, zero_v), jnp.where(first, zero_v, vb)], axis=0)
        acc = acc + jnp.dot(jnp.concatenate(ws, axis=1), v2, preferred_element_type=F32)
        return tuple(new_run), acc

    def alive(run):
        return (jnp.max(jnp.maximum(run[0], run[1])) > EXP_UNDERFLOW).astype(I32)

    zeros = jnp.zeros((tq, KEY_BLOCK), F32)
    run, acc = (zeros, zeros), jnp.zeros((tq, LANES), F32)
    for d in reversed(range(n_diag)):
        run, acc = block(d, run, acc, True)

    def cond(state):
        it, live, _, _ = state
        return (it < diag) & (live > 0)

    def body(state):
        it, _, run, acc = state
        run, acc = block(diag - 1 - it, run, acc, False)
        return it + 1, alive(run), run, acc

    _, _, _, acc = lax.while_loop(cond, body, (jnp.int32(0), alive(run), run, acc))
    o_ref[0] = acc


EXP_UNDERFLOW = -105.0


def _cumsum_rhs():
    r = lax.broadcasted_iota(I32, (2 * KEY_BLOCK, 2 * KEY_BLOCK), 0) % KEY_BLOCK
    c = lax.broadcasted_iota(I32, (2 * KEY_BLOCK, 2 * KEY_BLOCK), 1)
    return -((c >= KEY_BLOCK) | (r > c)).astype(BF16)


def _stick_breaking_attention(q, k, v, kd, vd, tq, wd, past):
    b, t, _ = q.shape
    lk = k.shape[1]
    pairs = GROUP_WIDTH // LANES
    main = pl.BlockSpec((1, lk, LANES), lambda bi, hp, i: (bi, 0, hp))
    diag = pl.BlockSpec((1, wd, LANES), lambda bi, hp, i: (bi, i, hp))
    return pl.pallas_call(
        functools.partial(_sb_kernel, tq=tq, past=past),
        grid=(b, pairs, t // tq),
        in_specs=[
            pl.BlockSpec((1, tq, LANES), lambda bi, hp, i: (bi, i, hp)),
            main, main, diag, diag,
            pl.BlockSpec((2 * KEY_BLOCK, 2 * KEY_BLOCK), lambda bi, hp, i: (0, 0)),
        ],
        out_specs=pl.BlockSpec((1, tq, LANES), lambda bi, hp, i: (bi, i, hp)),
        out_shape=jax.ShapeDtypeStruct((b, t, GROUP_WIDTH), F32),
        compiler_params=_params("parallel", "parallel", "arbitrary"),
        name="stick_breaking_attention",
    )(q, k, v, kd, vd, _cumsum_rhs())


def _fox_kernel(q_ref, k_ref, v_ref, kd_ref, vd_ref, cq_ref, ck_ref, ckd_ref, o_ref, *, tq, past):
    i = pl.program_id(2)
    hp = pl.program_id(1)
    q = q_ref[0]
    first = _head_masks(LANES)
    zero_q = jnp.zeros_like(q)
    qh = (jnp.where(first, q, zero_q), jnp.where(first, zero_q, q))
    kblk = ck_ref.shape[-1]
    wd = kd_ref.shape[1]
    diag = (past + i * tq) // kblk
    cq_all = cq_ref[0]
    head_lane = lax.broadcasted_iota(I32, (1, N_HEADS), 1)
    cq = [jnp.sum(jnp.where(head_lane == 2 * hp + h, cq_all, 0.0), axis=1, keepdims=True) for h in range(2)]

    def block(j, carry, masked):
        ms, ls, acc = carry
        if masked:
            kb, vb, ck = kd_ref[0], vd_ref[0], ckd_ref[0, 0, 0]
            mask = lax.broadcasted_iota(I32, (tq, wd), 1) <= lax.broadcasted_iota(I32, (tq, wd), 0)
        else:
            k0 = pl.multiple_of(j * kblk, kblk)
            kb = k_ref[0, pl.ds(k0, kblk), :].astype(BF16)
            vb = v_ref[0, pl.ds(k0, kblk), :].astype(BF16)
            ck = ck_ref[0, 0, j]
        ps, new_m, new_l, scales = [], [], [], []
        for h in range(2):
            s = lax.dot_general(qh[h], kb, (((1,), (1,)), ((), ())), preferred_element_type=F32)
            s = s + cq[h] - ck[h:h + 1, :]
            if masked:
                s = jnp.where(mask, s, -jnp.inf)
            m = jnp.maximum(ms[h], jnp.max(s, axis=1, keepdims=True))
            p = jnp.exp(s - m)
            a = jnp.exp(ms[h] - m)
            new_m.append(m)
            new_l.append(a * ls[h] + jnp.sum(p, axis=1, keepdims=True))
            scales.append(a)
            ps.append(p.astype(BF16))
        zero_v = jnp.zeros_like(vb)
        v2 = jnp.concatenate([jnp.where(first, vb, zero_v), jnp.where(first, zero_v, vb)], axis=0)
        pv = jnp.dot(jnp.concatenate(ps, axis=1), v2, preferred_element_type=F32)
        acc = acc * jnp.where(first, scales[0], scales[1]) + pv
        return (tuple(new_m), tuple(new_l), acc)

    neg = jnp.full((tq, 1), -jnp.inf, F32)
    zero = jnp.zeros((tq, 1), F32)
    carry = block(0, ((neg, neg), (zero, zero), jnp.zeros((tq, LANES), F32)), True)

    def body(it, carry):
        return block(diag - 1 - it, carry, False)

    _, ls, acc = lax.fori_loop(0, diag, body, carry)
    o_ref[0] = acc / jnp.where(first, ls[0], ls[1])


def _forgetting_attention(q, k, v, kd, vd, cq, ck, ckd, tq, wd, past):
    b, t, _ = q.shape
    lk = k.shape[1]
    pairs = GROUP_WIDTH // LANES
    main = pl.BlockSpec((1, lk, LANES), lambda bi, hp, i: (bi, 0, hp))
    diag = pl.BlockSpec((1, wd, LANES), lambda bi, hp, i: (bi, i, hp))
    return pl.pallas_call(
        functools.partial(_fox_kernel, tq=tq, past=past),
        grid=(b, pairs, t // tq),
        in_specs=[
            pl.BlockSpec((1, tq, LANES), lambda bi, hp, i: (bi, i, hp)),
            main, main, diag, diag,
            pl.BlockSpec((1, tq, N_HEADS), lambda bi, hp, i: (bi, i, 0)),
            pl.BlockSpec((1, 1) + ck.shape[2:], lambda bi, hp, i: (bi, hp, 0, 0, 0)),
            pl.BlockSpec((1, 1, 1, 2, wd), lambda bi, hp, i: (bi, hp, i, 0, 0)),
        ],
        out_specs=pl.BlockSpec((1, tq, LANES), lambda bi, hp, i: (bi, i, hp)),
        out_shape=jax.ShapeDtypeStruct((b, t, GROUP_WIDTH), F32),
        compiler_params=_params("parallel", "parallel", "arbitrary"),
        name="forgetting_attention",
    )(q, k, v, kd, vd, cq, ck, ckd)


def _mixout_kernel(osb_ref, ofx_ref, x_ref, gn_ref, w_ref, g_ref, b_ref, y_ref):
    def rms(o, g):
        return o * lax.rsqrt(jnp.mean(o * o, axis=-1, keepdims=True) + GN_EPS) * g

    gn = gn_ref[...]
    o = jnp.concatenate([rms(osb_ref[...], gn[:, :GROUP_WIDTH]), rms(ofx_ref[...], gn[:, GROUP_WIDTH:])], axis=1)
    mix = jnp.dot(o.astype(BF16), w_ref[...], preferred_element_type=F32)
    y_ref[...] = _layer_norm_rows(DN_ALPHA * x_ref[...] + mix, g_ref[...], b_ref[...])


def _mix_out(osb, ofx, x2d, w_gn, w_out, g, b, tm):
    m = x2d.shape[0]
    row = pl.BlockSpec((1, D_MODEL), lambda i: (0, 0))
    return pl.pallas_call(
        _mixout_kernel,
        grid=(m // tm,),
        in_specs=[
            pl.BlockSpec((tm, GROUP_WIDTH), lambda i: (i, 0)),
            pl.BlockSpec((tm, GROUP_WIDTH), lambda i: (i, 0)),
            pl.BlockSpec((tm, D_MODEL), lambda i: (i, 0)),
            row,
            pl.BlockSpec((D_MODEL, D_MODEL), lambda i: (0, 0)),
            row, row,
        ],
        out_specs=pl.BlockSpec((tm, D_MODEL), lambda i: (i, 0)),
        out_shape=jax.ShapeDtypeStruct((m, D_MODEL), F32),
        compiler_params=_params("parallel"),
        name="mix_out_ln1",
    )(osb, ofx, x2d, w_gn, w_out, g, b)


def _memkv_kernel(m_ref, wk_ref, wv_ref, k_ref, v_ref):
    mb = m_ref[...].astype(BF16)
    k_ref[...] = jnp.dot(mb, wk_ref[...], preferred_element_type=F32)
    v_ref[...] = jnp.dot(mb, wv_ref[...], preferred_element_type=F32)


def _mem_kv(mem2d, w_mk, w_mv, tm):
    m = mem2d.shape[0]
    wspec = pl.BlockSpec((D_MODEL, D_MODEL), lambda i: (0, 0))
    blk = pl.BlockSpec((tm, D_MODEL), lambda i: (i, 0))
    out = jax.ShapeDtypeStruct((m, D_MODEL), F32)
    return pl.pallas_call(
        _memkv_kernel,
        grid=(m // tm,),
        in_specs=[blk, wspec, wspec],
        out_specs=[blk, blk],
        out_shape=[out, out],
        compiler_params=_params("parallel"),
        name="mem_kv",
    )(mem2d, w_mk, w_mv)


def _memattn_kernel(x_ref, mk_ref, mv_ref, wq_ref, wo_ref, g_ref, b_ref, y_ref):
    x = x_ref[0]
    q = jnp.dot(x.astype(BF16), wq_ref[...], preferred_element_type=F32)
    qb = (q * (MEM_HEAD_DIM ** -0.5)).astype(BF16)
    mk = mk_ref[0].astype(BF16)
    mv = mv_ref[0].astype(BF16)
    outs = []
    for h in range(MEM_HEADS):
        sl = slice(h * MEM_HEAD_DIM, (h + 1) * MEM_HEAD_DIM)
        s = lax.dot_general(qb[:, sl], mk[:, sl], (((1,), (1,)), ((), ())), preferred_element_type=F32)
        p = jnp.exp(s - jnp.max(s, axis=1, keepdims=True))
        o = jnp.dot(p.astype(BF16), mv[:, sl], preferred_element_type=F32)
        outs.append(o / jnp.sum(p, axis=1, keepdims=True))
    o = jnp.concatenate(outs, axis=1).astype(BF16)
    att = jnp.dot(o, wo_ref[...], preferred_element_type=F32)
    y_ref[0] = _layer_norm_rows(DN_ALPHA * x + att, g_ref[...], b_ref[...])


def _mem_attention(x3d, mk, mv, w_mq, w_mo, g, b, tm):
    bsz, t, _ = x3d.shape
    n_mem = mk.shape[1]
    row = pl.BlockSpec((1, D_MODEL), lambda bi, i: (0, 0))
    wspec = pl.BlockSpec((D_MODEL, D_MODEL), lambda bi, i: (0, 0))
    return pl.pallas_call(
        _memattn_kernel,
        grid=(bsz, t // tm),
        in_specs=[
            pl.BlockSpec((1, tm, D_MODEL), lambda bi, i: (bi, i, 0)),
            pl.BlockSpec((1, n_mem, D_MODEL), lambda bi, i: (bi, 0, 0)),
            pl.BlockSpec((1, n_mem, D_MODEL), lambda bi, i: (bi, 0, 0)),
            wspec, wspec, row, row,
        ],
        out_specs=pl.BlockSpec((1, tm, D_MODEL), lambda bi, i: (bi, i, 0)),
        out_shape=jax.ShapeDtypeStruct((bsz, t, D_MODEL), F32),
        compiler_params=_params("parallel", "parallel"),
        name="mem_attention_ln2",
    )(x3d, mk, mv, w_mq, w_mo, g, b)


def _tree(op, xs):
    xs = list(xs)
    while len(xs) > 1:
        xs = [op(xs[i], xs[i + 1]) if i + 1 < len(xs) else xs[i] for i in range(0, len(xs), 2)]
    return xs[0]


def _all_sublanes(op, x):
    for shift in (4, 2, 1):
        x = op(x, pltpu.roll(x, shift, 0))
    return x


def _topk_slabs(slabs, keys, k, big):
    vals, kout = [], []
    for r in range(k):
        m = _all_sublanes(jnp.maximum, _tree(jnp.maximum, slabs))
        km = _all_sublanes(jnp.minimum, _tree(jnp.minimum, [jnp.where(s == m, kk, big) for s, kk in zip(slabs, keys)]))
        vals.append(m)
        kout.append(km)
        if r + 1 < k:
            slabs = [jnp.where(kk == km, -jnp.inf, s) for s, kk in zip(slabs, keys)]
    return vals, kout


def _pack_rows(rows, sub_io):
    out = rows[0]
    for r in range(1, SUBLANES):
        out = jnp.where(sub_io == r, rows[r], out)
    return out


HALF_EXPERTS = PEER_EXPERTS // 2
HALF_SHIFT = HALF_EXPERTS.bit_length() - 1
HIGH_MASK = -65536
ROUTE_HEADS_PER_STEP = 8


def _route_kernel(x_ref, w_ref, ka_ref, kb_ref, off_ref, sh_ref, g_ref, q_scr):
    tm = x_ref.shape[0]
    xb = x_ref[...].astype(BF16)
    for c in range(2 * PEER_HEADS):
        q_scr[c] = jnp.dot(xb, w_ref[:, c * PEER_HALF:(c + 1) * PEER_HALF],
                           preferred_element_type=F32).astype(BF16)
    assert PEER_TOPK == 2 * SUBLANES
    nt = (((1,), (1,)), ((), ()))
    sub_io = lax.broadcasted_iota(I32, (SUBLANES, tm), 0)
    sub_f = sub_io.astype(F32)
    n_slabs = PEER_NKEYS // SUBLANES
    key_slabs = [sub_f + float(SUBLANES * i) for i in range(n_slabs)]
    experts = float(PEER_EXPERTS)
    big = float(PEER_TOPK * PEER_TOPK) * experts

    def head(h):
        sa = lax.dot_general(ka_ref[h], q_scr[2 * h], nt, preferred_element_type=F32)
        sb = lax.dot_general(kb_ref[h], q_scr[2 * h + 1], nt, preferred_element_type=F32)
        slabs = lambda s: [s[SUBLANES * i:SUBLANES * (i + 1)] for i in range(n_slabs)]
        va, ia = _topk_slabs(slabs(sa), key_slabs, PEER_TOPK, float(PEER_NKEYS))
        vb, ib = _topk_slabs(slabs(sb), key_slabs, PEER_TOPK, float(PEER_NKEYS))
        va_hi, ia_hi = _pack_rows(va[SUBLANES:], sub_io), _pack_rows(ia[SUBLANES:], sub_io)
        vb_lo, ib_lo = _pack_rows(vb[:SUBLANES], sub_io), _pack_rows(ib[:SUBLANES], sub_io)
        vb_hi, ib_hi = _pack_rows(vb[SUBLANES:], sub_io), _pack_rows(ib[SUBLANES:], sub_io)
        cand = [va[0] + vb_lo, va[0] + vb_hi]
        ckey = [sub_f * experts + (ia[0] * PEER_NKEYS + ib_lo),
                (sub_f + SUBLANES) * experts + (ia[0] * PEER_NKEYS + ib_hi)]
        for i in range(1, SUBLANES):
            cand.append(va[i] + vb_lo)
            ckey.append((sub_f + float(i * PEER_TOPK)) * experts + (ia[i] * PEER_NKEYS + ib_lo))
        cand.append(va_hi + vb[0])
        ckey.append((sub_f + SUBLANES) * (PEER_TOPK * experts) + (ia_hi * PEER_NKEYS + ib[0]))
        top, tkey = _topk_slabs(cand, ckey, PEER_TOPK, big)
        r0 = pl.multiple_of(h * PEER_TOPK, SUBLANES)
        es = [jnp.exp(_pack_rows(top[SUBLANES * j:SUBLANES * (j + 1)], sub_io) - top[0]) for j in range(2)]
        denom = _all_sublanes(jnp.add, es[0] + es[1])
        for j in range(2):
            expert = _pack_rows(tkey[SUBLANES * j:SUBLANES * (j + 1)], sub_io).astype(I32) & (PEER_EXPERTS - 1)
            rows = pl.ds(pl.multiple_of(r0 + SUBLANES * j, SUBLANES), SUBLANES)
            off_ref[0, rows, :] = (expert & (HALF_EXPERTS - 1)) * SUBLANES
            sh_ref[0, rows, :] = (expert >> HALF_SHIFT) * 16
            g_ref[0, rows, :] = es[j] / denom

    def head_group(i, carry):
        for k in range(ROUTE_HEADS_PER_STEP):
            head(ROUTE_HEADS_PER_STEP * i + k)
        return carry

    lax.fori_loop(0, PEER_HEADS // ROUTE_HEADS_PER_STEP, head_group, 0)


def _peer_route(x2d, w_pq, keys_a, keys_b, tm):
    m = x2d.shape[0]
    nt = m // tm
    kspec = pl.BlockSpec((PEER_HEADS, PEER_NKEYS, PEER_HALF), lambda i: (0, 0, 0))
    ospec = pl.BlockSpec((1, PEER_PICKS, tm), lambda i: (i, 0, 0))
    return pl.pallas_call(
        _route_kernel,
        grid=(nt,),
        in_specs=[pl.BlockSpec((tm, D_MODEL), lambda i: (i, 0)),
                  pl.BlockSpec((D_MODEL, 2 * PEER_HEADS * PEER_HALF), lambda i: (0, 0)),
                  kspec, kspec],
        out_specs=[ospec, ospec, ospec],
        out_shape=[jax.ShapeDtypeStruct((nt, PEER_PICKS, tm), I32),
                   jax.ShapeDtypeStruct((nt, PEER_PICKS, tm), I32),
                   jax.ShapeDtypeStruct((nt, PEER_PICKS, tm), F32)],
        scratch_shapes=[pltpu.VMEM((2 * PEER_HEADS, tm, PEER_HALF), BF16)],
        compiler_params=_params("parallel"),
        name="peer_route",
    )(x2d, w_pq, keys_a, keys_b)


def _pack_table(t):
    bits = lax.bitcast_convert_type(t.astype(BF16), jnp.uint16).astype(jnp.uint32)
    word = (bits[:HALF_EXPERTS] << 16) | bits[HALF_EXPERTS:]
    return lax.bitcast_convert_type(word, I32).reshape(HALF_EXPERTS * SUBLANES, LANES)


def _table_spec():
    return pl.BlockSpec((HALF_EXPERTS * SUBLANES, LANES), lambda i: (0, 0), pipeline_mode=pl.Buffered(1))


def _table_row(tab_ref, off, shift):
    row = tab_ref[pl.ds(pl.multiple_of(off, SUBLANES), SUBLANES), :]
    return pltpu.bitcast((row << shift) & HIGH_MASK, F32)


def _rows_to_tiles(x_ref, tiles_ref):
    for s in range(SUBLANES):
        tiles_ref[:, s, :] = x_ref[:, s * LANES:(s + 1) * LANES]


def _tiles_to_rows(tiles_ref, y_ref):
    for s in range(SUBLANES):
        y_ref[:, s * LANES:(s + 1) * LANES] = tiles_ref[:, s, :]


BIT_REVERSED = (0, 4, 2, 6, 1, 5, 3, 7)


def _sublane_sums(prods, sub_io):
    def merge(a, b, h):
        low = (sub_io & h) == 0
        if 2 * h == SUBLANES:
            return jnp.where(low, a, b) + pltpu.roll(jnp.where(low, b, a), h, 0)
        return jnp.where(low, a, pltpu.roll(b, h, 0)) + jnp.where(low, pltpu.roll(a, SUBLANES - h, 0), b)

    p = [prods[BIT_REVERSED[k]] for k in range(SUBLANES)]
    t = [merge(p[2 * k], p[2 * k + 1], 4) for k in range(4)]
    u = [merge(t[2 * k], t[2 * k + 1], 2) for k in range(2)]
    return merge(u[0], u[1], 1)


def _peer_in_kernel(off_s, sh_s, x_ref, sh_ref, g_ref, spread_ref, tab_ref, w_ref, part_ref, xt_ref):
    tm = x_ref.shape[0]
    sub_io = lax.broadcasted_iota(I32, (SUBLANES, LANES), 0)
    tok_io = lax.broadcasted_iota(I32, (PEER_PICKS, tm), 1)
    _rows_to_tiles(x_ref, xt_ref)

    def gather(t):
        x = xt_ref[t]
        for g8 in range(PEER_PICKS // SUBLANES):
            picks = [g8 * SUBLANES + s for s in range(SUBLANES)]
            prods = [x * _table_row(tab_ref, off_s[0, t, p], sh_s[0, t, p]) for p in picks]
            part_ref[t, g8 * SUBLANES:(g8 + 1) * SUBLANES, :] = _sublane_sums(prods, sub_io)

    def fold(t, h_t):
        col = jnp.sum(part_ref[t], axis=1, keepdims=True)
        return jnp.where(tok_io == t, col, h_t)

    def step(t, h_t):
        h_t = fold(t - 1, h_t)
        gather(t)
        return h_t

    gather(0)
    h_t = lax.fori_loop(1, tm, step, jnp.zeros((PEER_PICKS, tm), F32))
    h_t = fold(tm - 1, h_t)
    gelu = 0.5 * h_t * (1.0 + lax.erf(h_t * (2.0 ** -0.5)))
    w = (gelu * g_ref[0]).astype(BF16)
    tn = (((0,), (0,)), ((), ()))
    spread = spread_ref[...]
    w_rows = lax.dot_general(w, spread, tn, preferred_element_type=F32)
    low_half = lax.dot_general((sh_ref[0] >> 4).astype(BF16), spread, tn, preferred_element_type=F32)
    odd_row = (lax.broadcasted_iota(I32, (1, PACKED_ROWS * PEER_PICKS), 1) & 1).astype(F32)
    w_ref[...] = jnp.where(low_half + odd_row == 1.0, w_rows, 0.0)


OUT_TOKENS_PER_STEP = 16
PACKED_ROWS = 2 * SUBLANES
SPREAD_COLS = PACKED_ROWS * PEER_PICKS


def _peer_in(off, sh, x2d, gate, table, tm):
    nt = off.shape[0]
    m = x2d.shape[0]
    assert tm == LANES
    tspec = pl.BlockSpec((1, PEER_PICKS, tm), lambda i: (i, 0, 0))
    sspec = pl.BlockSpec((1, tm, PEER_PICKS), lambda i: (i, 0, 0), memory_space=pltpu.SMEM)
    pick = lax.broadcasted_iota(I32, (PEER_PICKS, SPREAD_COLS), 0)
    col = lax.broadcasted_iota(I32, (PEER_PICKS, SPREAD_COLS), 1)
    spread = (col // PACKED_ROWS == pick).astype(BF16)
    return pl.pallas_call(
        _peer_in_kernel,
        grid=(nt,),
        in_specs=[sspec, sspec,
                  pl.BlockSpec((tm, D_MODEL), lambda i: (i, 0)),
                  tspec, tspec,
                  pl.BlockSpec((PEER_PICKS, SPREAD_COLS), lambda i: (0, 0)),
                  _table_spec()],
        out_specs=pl.BlockSpec((tm, SPREAD_COLS), lambda i: (i, 0)),
        out_shape=jax.ShapeDtypeStruct((m, SPREAD_COLS), F32),
        scratch_shapes=[pltpu.VMEM((tm, PEER_PICKS, LANES), F32), pltpu.VMEM((tm, SUBLANES, LANES), F32)],
        compiler_params=_params("arbitrary"),
        name="peer_expert_in",
    )(off.transpose(0, 2, 1), sh.transpose(0, 2, 1), x2d, sh, gate, spread, table)


def _peer_out_kernel(off_s, w_ref, x_ref, g_ref, b_ref, tab_ref, y_ref, xt_ref):
    tm = x_ref.shape[0]
    _rows_to_tiles(x_ref, xt_ref)
    chunks = SPREAD_COLS // LANES
    lane = lax.broadcasted_iota(I32, (SUBLANES, LANES), 1)
    sub = lax.broadcasted_iota(I32, (SUBLANES, LANES), 0)
    own_chunk = (lane % PACKED_ROWS) // 2 == sub

    def token(t, w_row):
        tiles = [pltpu.bitcast(tab_ref[pl.ds(pl.multiple_of(off_s[0, t, p], SUBLANES), SUBLANES), :], BF16)
                 for p in range(PEER_PICKS)]
        lhs = jnp.concatenate([jnp.where(own_chunk, jnp.broadcast_to(w, (SUBLANES, LANES)), 0.0) for w in w_row],
                              axis=1).astype(BF16)
        out = jnp.dot(lhs, jnp.concatenate(tiles, axis=0), preferred_element_type=F32)
        xt_ref[t] = DN_ALPHA * xt_ref[t] + out

    def token_group(i, carry):
        for g in range(OUT_TOKENS_PER_STEP // SUBLANES):
            t0 = pl.multiple_of(i * OUT_TOKENS_PER_STEP + g * SUBLANES, SUBLANES)
            w8 = [w_ref[pl.ds(t0, SUBLANES), k * LANES:(k + 1) * LANES] for k in range(chunks)]
            for r in range(SUBLANES):
                token(t0 + r, [w[r:r + 1, :] for w in w8])
        return carry

    lax.fori_loop(0, tm // OUT_TOKENS_PER_STEP, token_group, 0)
    _tiles_to_rows(xt_ref, y_ref)
    y_ref[...] = _layer_norm_rows(y_ref[...], g_ref[...], b_ref[...])


def _peer_out(off, w_rows, x2d, g, b, table, tm):
    nt = off.shape[0]
    m = x2d.shape[0]
    sspec = pl.BlockSpec((1, tm, PEER_PICKS), lambda i: (i, 0, 0), memory_space=pltpu.SMEM)
    vec = pl.BlockSpec((1, D_MODEL), lambda i: (0, 0))
    xspec = pl.BlockSpec((tm, D_MODEL), lambda i: (i, 0))
    return pl.pallas_call(
        _peer_out_kernel,
        grid=(nt,),
        in_specs=[sspec, pl.BlockSpec((tm, SPREAD_COLS), lambda i: (i, 0)), xspec, vec, vec, _table_spec()],
        out_specs=xspec,
        out_shape=jax.ShapeDtypeStruct((m, D_MODEL), F32),
        scratch_shapes=[pltpu.VMEM((tm, SUBLANES, LANES), F32)],
        compiler_params=_params("arbitrary"),
        name="peer_expert_out",
    )(off.transpose(0, 2, 1), w_rows, x2d, g, b, table)


def _pick_tile(n, pref):
    t = pref
    while n % t:
        t //= 2
    return t


def _layer(x, past, mem_k, mem_v, wts):
    b, t, _ = x.shape
    m = b * t
    x2d = x.reshape(m, D_MODEL)
    tm = _pick_tile(m, 256)

    (qsb, ksb, vsb, qfx, kfx, vfx, ksbb, vsbb, kfxb, vfxb, lf) = _in_projection(
        x2d, wts["w_in_main"], wts["w_in_f"], wts["b_f"], tm)
    state = tuple(a.reshape(b, t, N_HEADS, HEAD_DIM) for a in (ksb, vsb, kfx, vfx)) + (lf.reshape(b, t, N_HEADS),)

    r3 = lambda a: a.reshape(b, t, GROUP_WIDTH)
    tq = min(QUERY_BLOCK, t)
    kblk = FOX_TILE // tq
    new_kv = [r3(ksbb), r3(vsbb), r3(kfxb), r3(vfxb)]
    lf_new = lf.reshape(b, t, N_HEADS)
    if past is None:
        p, wd = 0, tq
        main_kv, diag_kv, lf_all = new_kv, new_kv, lf_new
    else:
        p = past[0].shape[1]
        wd = -(-t // KEY_BLOCK) * KEY_BLOCK
        assert t == tq
        main_kv = [c.reshape(b, p, GROUP_WIDTH) for c in past[:4]]
        diag_kv = [jnp.pad(a, ((0, 0), (0, wd - t), (0, 0))) for a in new_kv]
        lf_all = jnp.concatenate([past[4].astype(F32), jnp.pad(lf_new, ((0, 0), (0, wd - t), (0, 0)))], axis=1)

    c_all = _forget_cumsum(lf_all)
    lk = main_kv[0].shape[1]
    heads = lambda c, n, w: c.T.reshape(b, N_HEADS // 2, 2, n, w).transpose(0, 1, 3, 2, 4)
    cq = c_all[p:p + t].reshape(t, b, N_HEADS).transpose(1, 0, 2)
    ck = heads(c_all[:lk], lk // kblk, kblk)
    ckd = heads(c_all[p:p + (t // tq) * wd], t // tq, wd)

    assert t % tq == 0 and lk % kblk == 0 and kblk % KEY_BLOCK == 0 and wd % KEY_BLOCK == 0
    assert p % kblk == 0 and (past is not None or tq % kblk == 0)
    o_sb = _stick_breaking_attention(r3(qsb), main_kv[0], main_kv[1], diag_kv[0], diag_kv[1], tq, wd, p)
    o_fx = _forgetting_attention(r3(qfx), main_kv[2], main_kv[3], diag_kv[2], diag_kv[3], cq, ck, ckd, tq, wd, p)

    x1 = _mix_out(o_sb.reshape(m, GROUP_WIDTH), o_fx.reshape(m, GROUP_WIDTH), x2d,
                  wts["w_gn"], wts["w_out"], wts["ln1_g"], wts["ln1_b"], tm)
    x2 = _mem_attention(x1.reshape(b, t, D_MODEL), mem_k, mem_v, wts["w_mq"], wts["w_mo"],
                        wts["ln2_g"], wts["ln2_b"], _pick_tile(t, 256))
    x2d2 = x2.reshape(m, D_MODEL)

    tr = LANES
    off, sh, gate = _peer_route(x2d2, wts["w_pq"], wts["keys_a"], wts["keys_b"], tr)
    w_rows = _peer_in(off, sh, x2d2, gate, wts["table_u"], tr)
    y = _peer_out(off, w_rows, x2d2, wts["ln3_g"], wts["ln3_b"], wts["table_v"], tr)
    return y.reshape(b, t, D_MODEL), state


def kernel(x_prompt, x_sample, mem_prompt, cache_sb_k, cache_sb_v, cache_fox_k, cache_fox_v, cache_fox_logf,
           cache_mem_k, cache_mem_v, w_in, b_f, w_gn, w_out, ln1_g, ln1_b, w_mq, w_mk, w_mv, w_mo, ln2_g, ln2_b,
           w_pq, peer_keys_a, peer_keys_b, peer_u, peer_v, ln3_g, ln3_b):
    depth = w_in.shape[0]
    hp, hs = x_prompt, x_sample
    bp = x_prompt.shape[0]
    n_mem = mem_prompt.shape[1]
    mix_cols = 6 * GROUP_WIDTH
    outs_p = [[] for _ in range(7)]
    outs_s = [[] for _ in range(5)]
    row = lambda a: a.reshape(1, D_MODEL)
    for l in range(depth):
        wts = {
            "w_in_main": w_in[l][:, :mix_cols].astype(BF16),
            "w_in_f": jnp.pad(w_in[l][:, mix_cols:], ((0, 0), (0, LANES - N_HEADS))).astype(BF16),
            "b_f": jnp.pad(b_f[l], (0, LANES - N_HEADS)).reshape(1, LANES),
            "w_gn": row(w_gn[l]), "w_out": w_out[l].astype(BF16),
            "ln1_g": row(ln1_g[l]), "ln1_b": row(ln1_b[l]),
            "w_mq": w_mq[l].astype(BF16), "w_mo": w_mo[l].astype(BF16),
            "ln2_g": row(ln2_g[l]), "ln2_b": row(ln2_b[l]),
            "w_pq": w_pq[l].astype(BF16),
            "keys_a": peer_keys_a[l].astype(BF16), "keys_b": peer_keys_b[l].astype(BF16),
            "table_u": _pack_table(peer_u[l]), "table_v": _pack_table(peer_v[l]),
            "ln3_g": row(ln3_g[l]), "ln3_b": row(ln3_b[l]),
        }
        mem2d = mem_prompt.reshape(bp * n_mem, D_MODEL)
        mk_p, mv_p = _mem_kv(mem2d, w_mk[l].astype(BF16), w_mv[l].astype(BF16), _pick_tile(bp * n_mem, 512))
        mk_p = mk_p.reshape(bp, n_mem, D_MODEL)
        mv_p = mv_p.reshape(bp, n_mem, D_MODEL)
        hp, st_p = _layer(hp, None, mk_p, mv_p, wts)
        bs = x_sample.shape[0]
        past = (cache_sb_k[l], cache_sb_v[l], cache_fox_k[l], cache_fox_v[l], cache_fox_logf[l])
        hs, st_s = _layer(hs, past, cache_mem_k[l].reshape(bs, -1, D_MODEL), cache_mem_v[l].reshape(bs, -1, D_MODEL), wts)
        for i in range(5):
            outs_p[i].append(st_p[i])
            outs_s[i].append(st_s[i])
        outs_p[5].append(mk_p.reshape(bp, n_mem, MEM_HEADS, MEM_HEAD_DIM))
        outs_p[6].append(mv_p.reshape(bp, n_mem, MEM_HEADS, MEM_HEAD_DIM))
    stack = lambda xs: jnp.stack(xs)
    return (hp, hs) + tuple(stack(o) for o in outs_p) + tuple(stack(o) for o in outs_s)
```

```python
import functools

import jax
import jax.numpy as jnp
from jax import lax
from jax.experimental import pallas as pl
from jax.experimental.pallas import tpu as pltpu

F32 = jnp.float32
BF16 = jnp.bfloat16
I32 = jnp.int32

D_MODEL = 1024
HEAD_DIM = 64
N_HEADS = 8
GROUP_WIDTH = N_HEADS * HEAD_DIM
MEM_HEADS = 4
MEM_HEAD_DIM = D_MODEL // MEM_HEADS
PEER_HEADS = 8
PEER_NKEYS = 128
PEER_TOPK = 16
PEER_HALF = 128
PEER_PICKS = PEER_HEADS * PEER_TOPK
PEER_EXPERTS = PEER_NKEYS * PEER_NKEYS
DN_ALPHA = 2.0 ** 0.25
LN_EPS = 1e-5
GN_EPS = 1e-6

LANES = 128
SUBLANES = 8
KEY_BLOCK = 128
QUERY_BLOCK = 256
FOX_TILE = 256 * 256
ATTN_PAIRS = 4
VMEM_LIMIT = 56 * 1024 * 1024


def _params(*sem):
    return pltpu.CompilerParams(dimension_semantics=sem, vmem_limit_bytes=VMEM_LIMIT)


def _log_sigmoid(x):
    return jnp.minimum(x, 0.0) - jnp.log1p(jnp.exp(-jnp.abs(x)))


def _layer_norm_rows(r, g, b):
    mu = jnp.mean(r, axis=-1, keepdims=True)
    d = r - mu
    var = jnp.mean(d * d, axis=-1, keepdims=True)
    return d * lax.rsqrt(var + LN_EPS) * g + b


def _inproj_kernel(x_ref, w_ref, wf_ref, bf_ref,
                   qsb_ref, ksb_ref, vsb_ref, qfx_ref, kfx_ref, vfx_ref,
                   ksbb_ref, vsbb_ref, kfxb_ref, vfxb_ref, lf_ref):
    xb = x_ref[...].astype(BF16)

    def proj(j):
        return jnp.dot(xb, w_ref[:, j * GROUP_WIDTH:(j + 1) * GROUP_WIDTH], preferred_element_type=F32)

    scale = HEAD_DIM ** -0.5
    qsb_ref[...] = (proj(0) * scale).astype(BF16)
    k = proj(1)
    ksb_ref[...] = k
    ksbb_ref[...] = k.astype(BF16)
    v = proj(2)
    vsb_ref[...] = v
    vsbb_ref[...] = v.astype(BF16)
    qfx_ref[...] = (proj(3) * scale).astype(BF16)
    k = proj(4)
    kfx_ref[...] = k
    kfxb_ref[...] = k.astype(BF16)
    v = proj(5)
    vfx_ref[...] = v
    vfxb_ref[...] = v.astype(BF16)
    f = jnp.dot(xb, wf_ref[...], preferred_element_type=F32) + bf_ref[...]
    lf_ref[...] = _log_sigmoid(f)[:, :N_HEADS]


def _in_projection(x2d, w_main, w_f, b_f, tm):
    m = x2d.shape[0]
    f32o = jax.ShapeDtypeStruct((m, GROUP_WIDTH), F32)
    bf16o = jax.ShapeDtypeStruct((m, GROUP_WIDTH), BF16)
    blk = pl.BlockSpec((tm, GROUP_WIDTH), lambda i: (i, 0))
    return pl.pallas_call(
        _inproj_kernel,
        grid=(m // tm,),
        in_specs=[
            pl.BlockSpec((tm, D_MODEL), lambda i: (i, 0)),
            pl.BlockSpec((D_MODEL, 6 * GROUP_WIDTH), lambda i: (0, 0)),
            pl.BlockSpec((D_MODEL, LANES), lambda i: (0, 0)),
            pl.BlockSpec((1, LANES), lambda i: (0, 0)),
        ],
        out_specs=[blk] * 10 + [pl.BlockSpec((tm, N_HEADS), lambda i: (i, 0))],
        out_shape=[bf16o, f32o, f32o, bf16o, f32o, f32o, bf16o, bf16o, bf16o, bf16o,
                   jax.ShapeDtypeStruct((m, N_HEADS), F32)],
        compiler_params=_params("parallel"),
        name="in_projection",
    )(x2d, w_main, w_f, b_f)


def _cumsum_kernel(lf_ref, tri_ref, c_ref):
    l, cols = lf_ref.shape

    def chunk(i, carry):
        r0 = pl.multiple_of(i * KEY_BLOCK, KEY_BLOCK)
        v = lf_ref[pl.ds(r0, KEY_BLOCK), :]
        hi = v.astype(BF16)
        r1 = v - hi.astype(F32)
        mid = r1.astype(BF16)
        lo = (r1 - mid.astype(F32)).astype(BF16)
        parts = jnp.concatenate([hi, mid, lo], axis=1)
        s = jnp.dot(tri_ref[...], parts, preferred_element_type=F32)
        c = s[:, :cols] + s[:, cols:2 * cols] + s[:, 2 * cols:] + carry
        c_ref[pl.ds(r0, KEY_BLOCK), :] = c
        return c[KEY_BLOCK - 1:, :]

    lax.fori_loop(0, l // KEY_BLOCK, chunk, jnp.zeros((1, cols), F32))


def _forget_cumsum(lf):
    b, l, _ = lf.shape
    cols = b * N_HEADS
    r = lax.broadcasted_iota(I32, (KEY_BLOCK, KEY_BLOCK), 0)
    c = lax.broadcasted_iota(I32, (KEY_BLOCK, KEY_BLOCK), 1)
    tri = (c <= r).astype(BF16)
    return pl.pallas_call(
        _cumsum_kernel,
        grid=(1,),
        in_specs=[pl.BlockSpec((l, cols), lambda i: (0, 0)),
                  pl.BlockSpec((KEY_BLOCK, KEY_BLOCK), lambda i: (0, 0))],
        out_specs=pl.BlockSpec((l, cols), lambda i: (0, 0)),
        out_shape=jax.ShapeDtypeStruct((l, cols), F32),
        compiler_params=_params("arbitrary"),
        name="forget_cumsum",
    )(lf.transpose(1, 0, 2).reshape(l, cols), tri)


def _head_masks(width):
    lane = lax.broadcasted_iota(I32, (1, width), 1)
    return lane < HEAD_DIM


def _sb_kernel(q_ref, k_ref, v_ref, kd_ref, vd_ref, tri_ref, o_ref, *, tq, past):
    i = pl.program_id(2)
    q = q_ref[0]
    first = _head_masks(LANES)
    zero_q = jnp.zeros_like(q)
    qh = (jnp.where(first, q, zero_q), jnp.where(first, zero_q, q))
    q_pos0 = past + i * tq
    diag = q_pos0 // KEY_BLOCK
    n_diag = kd_ref.shape[1] // KEY_BLOCK
    tri = tri_ref[...]

    def block(j, run, acc, masked):
        if masked:
            kb = kd_ref[0, j * KEY_BLOCK:(j + 1) * KEY_BLOCK, :]
            vb = vd_ref[0, j * KEY_BLOCK:(j + 1) * KEY_BLOCK, :]
            kpos = j * KEY_BLOCK + lax.broadcasted_iota(I32, (tq, KEY_BLOCK), 1)
            qpos = lax.broadcasted_iota(I32, (tq, KEY_BLOCK), 0)
            mask = kpos < qpos
        else:
            k0 = pl.multiple_of(j * KEY_BLOCK, KEY_BLOCK)
            kb = k_ref[0, pl.ds(k0, KEY_BLOCK), :].astype(BF16)
            vb = v_ref[0, pl.ds(k0, KEY_BLOCK), :].astype(BF16)
        ws = []
        new_run = []
        for h in range(2):
            z = lax.dot_general(qh[h], kb, (((1,), (1,)), ((), ())), preferred_element_type=F32)
            sp = jnp.maximum(z, 0.0) + jnp.log(1.0 + jnp.exp(-jnp.abs(z)))
            log_beta = z - sp
            if masked:
                sp = jnp.where(mask, sp, 0.0)
            hi = sp.astype(BF16)
            lo = (sp - hi.astype(F32)).astype(BF16)
            c = jnp.dot(jnp.concatenate([hi, lo], axis=1), tri, preferred_element_type=F32)
            w = jnp.exp(log_beta + c[:, :KEY_BLOCK] + run[h])
            if masked:
                w = jnp.where(mask, w, 0.0)
            new_run.append(run[h] + c[:, KEY_BLOCK:])
            ws.append(w.astype(BF16))
        zero_v = jnp.zeros_like(vb)
        v2 = jnp.concatenate([jnp.where(first, vb, zero_v), jnp.where(first, zero_v, vb)], axis=0)
        acc = acc + jnp.dot(jnp.concatenate(ws, axis=1), v2, preferred_element_type=F32)
        return tuple(new_run), acc

    def alive(run):
        return (jnp.max(jnp.maximum(run[0], run[1])) > EXP_UNDERFLOW).astype(I32)

    zeros = jnp.zeros((tq, KEY_BLOCK), F32)
    run, acc = (zeros, zeros), jnp.zeros((tq, LANES), F32)
    for d in reversed(range(n_diag)):
        run, acc = block(d, run, acc, True)

    def cond(state):
        it, live, _, _ = state
        return (it < diag) & (live > 0)

    def body(state):
        it, _, run, acc = state
        run, acc = block(diag - 1 - it, run, acc, False)
        return it + 1, alive(run), run, acc

    _, _, _, acc = lax.while_loop(cond, body, (jnp.int32(0), alive(run), run, acc))
    o_ref[0] = acc


EXP_UNDERFLOW = -105.0


def _cumsum_rhs():
    r = lax.broadcasted_iota(I32, (2 * KEY_BLOCK, 2 * KEY_BLOCK), 0) % KEY_BLOCK
    c = lax.broadcasted_iota(I32, (2 * KEY_BLOCK, 2 * KEY_BLOCK), 1)
    return -((c >= KEY_BLOCK) | (r > c)).astype(BF16)


def _stick_breaking_attention(q, k, v, kd, vd, tq, wd, past):
    b, t, _ = q.shape
    lk = k.shape[1]
    pairs = GROUP_WIDTH // LANES
    main = pl.BlockSpec((1, lk, LANES), lambda bi, hp, i: (bi, 0, hp))
    diag = pl.BlockSpec((1, wd, LANES), lambda bi, hp, i: (bi, i, hp))
    return pl.pallas_call(
        functools.partial(_sb_kernel, tq=tq, past=past),
        grid=(b, pairs, t // tq),
        in_specs=[
            pl.BlockSpec((1, tq, LANES), lambda bi, hp, i: (bi, i, hp)),
            main, main, diag, diag,
            pl.BlockSpec((2 * KEY_BLOCK, 2 * KEY_BLOCK), lambda bi, hp, i: (0, 0)),
        ],
        out_specs=pl.BlockSpec((1, tq, LANES), lambda bi, hp, i: (bi, i, hp)),
        out_shape=jax.ShapeDtypeStruct((b, t, GROUP_WIDTH), F32),
        compiler_params=_params("parallel", "parallel", "arbitrary"),
        name="stick_breaking_attention",
    )(q, k, v, kd, vd, _cumsum_rhs())


def _fox_kernel(q_ref, k_ref, v_ref, kd_ref, vd_ref, cq_ref, ck_ref, ckd_ref, o_ref, *, tq, past):
    i = pl.program_id(2)
    hp = pl.program_id(1)
    first = _head_masks(LANES)
    kblk = ck_ref.shape[-1]
    wd = kd_ref.shape[1]
    diag = (past + i * tq) // kblk
    cq_all = cq_ref[0]
    head_lane = lax.broadcasted_iota(I32, (1, N_HEADS), 1)
    lanes = [slice(pp * LANES, (pp + 1) * LANES) for pp in range(ATTN_PAIRS)]
    qh, cq = [], []
    for pp in range(ATTN_PAIRS):
        q = q_ref[0, :, lanes[pp]]
        zero_q = jnp.zeros_like(q)
        qh.append((jnp.where(first, q, zero_q), jnp.where(first, zero_q, q)))
        head0 = 2 * (ATTN_PAIRS * hp + pp)
        cq.append([jnp.sum(jnp.where(head_lane == head0 + h, cq_all, 0.0), axis=1, keepdims=True) for h in range(2)])

    def pair_block(pp, j, carry, masked):
        ms, ls, acc = carry
        if masked:
            kb, vb, ck = kd_ref[0, :, lanes[pp]], vd_ref[0, :, lanes[pp]], ckd_ref[0, pp, 0]
            mask = lax.broadcasted_iota(I32, (tq, wd), 1) <= lax.broadcasted_iota(I32, (tq, wd), 0)
        else:
            k0 = pl.multiple_of(j * kblk, kblk)
            kb = k_ref[0, pl.ds(k0, kblk), lanes[pp]].astype(BF16)
            vb = v_ref[0, pl.ds(k0, kblk), lanes[pp]].astype(BF16)
            ck = ck_ref[0, pp, j]
        ps, new_m, new_l, scales = [], [], [], []
        for h in range(2):
            s = lax.dot_general(qh[pp][h], kb, (((1,), (1,)), ((), ())), preferred_element_type=F32)
            s = s + cq[pp][h] - ck[h:h + 1, :]
            if masked:
                s = jnp.where(mask, s, -jnp.inf)
            m = jnp.maximum(ms[h], jnp.max(s, axis=1, keepdims=True))
            p = jnp.exp(s - m)
            a = jnp.exp(ms[h] - m)
            new_m.append(m)
            new_l.append(a * ls[h] + jnp.sum(p, axis=1, keepdims=True))
            scales.append(a)
            ps.append(p.astype(BF16))
        zero_v = jnp.zeros_like(vb)
        v2 = jnp.concatenate([jnp.where(first, vb, zero_v), jnp.where(first, zero_v, vb)], axis=0)
        pv = jnp.dot(jnp.concatenate(ps, axis=1), v2, preferred_element_type=F32)
        acc = acc * jnp.where(first, scales[0], scales[1]) + pv
        return (tuple(new_m), tuple(new_l), acc)

    def block(j, carries, masked):
        return tuple(pair_block(pp, j, carries[pp], masked) for pp in range(ATTN_PAIRS))

    neg = jnp.full((tq, 1), -jnp.inf, F32)
    zero = jnp.zeros((tq, 1), F32)
    carries = block(0, (((neg, neg), (zero, zero), jnp.zeros((tq, LANES), F32)),) * ATTN_PAIRS, True)

    def body(it, carries):
        return block(diag - 1 - it, carries, False)

    carries = lax.fori_loop(0, diag, body, carries)
    for pp in range(ATTN_PAIRS):
        _, ls, acc = carries[pp]
        o_ref[0, :, lanes[pp]] = acc / jnp.where(first, ls[0], ls[1])


def _forgetting_attention(q, k, v, kd, vd, cq, ck, ckd, tq, wd, past):
    b, t, _ = q.shape
    lk = k.shape[1]
    width = ATTN_PAIRS * LANES
    groups = GROUP_WIDTH // width
    main = pl.BlockSpec((1, lk, width), lambda bi, hp, i: (bi, 0, hp))
    diag = pl.BlockSpec((1, wd, width), lambda bi, hp, i: (bi, i, hp))
    return pl.pallas_call(
        functools.partial(_fox_kernel, tq=tq, past=past),
        grid=(b, groups, t // tq),
        in_specs=[
            pl.BlockSpec((1, tq, width), lambda bi, hp, i: (bi, i, hp)),
            main, main, diag, diag,
            pl.BlockSpec((1, tq, N_HEADS), lambda bi, hp, i: (bi, i, 0)),
            pl.BlockSpec((1, ATTN_PAIRS) + ck.shape[2:], lambda bi, hp, i: (bi, hp, 0, 0, 0)),
            pl.BlockSpec((1, ATTN_PAIRS, 1, 2, wd), lambda bi, hp, i: (bi, hp, i, 0, 0)),
        ],
        out_specs=pl.BlockSpec((1, tq, width), lambda bi, hp, i: (bi, i, hp)),
        out_shape=jax.ShapeDtypeStruct((b, t, GROUP_WIDTH), F32),
        compiler_params=_params("parallel", "parallel", "arbitrary"),
        name="forgetting_attention",
    )(q, k, v, kd, vd, cq, ck, ckd)


def _mixout_kernel(osb_ref, ofx_ref, x_ref, gn_ref, w_ref, g_ref, b_ref, y_ref):
    def rms(o, g):
        return o * lax.rsqrt(jnp.mean(o * o, axis=-1, keepdims=True) + GN_EPS) * g

    gn = gn_ref[...]
    o = jnp.concatenate([rms(osb_ref[...], gn[:, :GROUP_WIDTH]), rms(ofx_ref[...], gn[:, GROUP_WIDTH:])], axis=1)
    mix = jnp.dot(o.astype(BF16), w_ref[...], preferred_element_type=F32)
    y_ref[...] = _layer_norm_rows(DN_ALPHA * x_ref[...] + mix, g_ref[...], b_ref[...])


def _mix_out(osb, ofx, x2d, w_gn, w_out, g, b, tm):
    m = x2d.shape[0]
    row = pl.BlockSpec((1, D_MODEL), lambda i: (0, 0))
    return pl.pallas_call(
        _mixout_kernel,
        grid=(m // tm,),
        in_specs=[
            pl.BlockSpec((tm, GROUP_WIDTH), lambda i: (i, 0)),
            pl.BlockSpec((tm, GROUP_WIDTH), lambda i: (i, 0)),
            pl.BlockSpec((tm, D_MODEL), lambda i: (i, 0)),
            row,
            pl.BlockSpec((D_MODEL, D_MODEL), lambda i: (0, 0)),
            row, row,
        ],
        out_specs=pl.BlockSpec((tm, D_MODEL), lambda i: (i, 0)),
        out_shape=jax.ShapeDtypeStruct((m, D_MODEL), F32),
        compiler_params=_params("parallel"),
        name="mix_out_ln1",
    )(osb, ofx, x2d, w_gn, w_out, g, b)


def _memkv_kernel(m_ref, wk_ref, wv_ref, k_ref, v_ref):
    mb = m_ref[...].astype(BF16)
    k_ref[...] = jnp.dot(mb, wk_ref[...], preferred_element_type=F32)
    v_ref[...] = jnp.dot(mb, wv_ref[...], preferred_element_type=F32)


def _mem_kv(mem2d, w_mk, w_mv, tm):
    m = mem2d.shape[0]
    wspec = pl.BlockSpec((D_MODEL, D_MODEL), lambda i: (0, 0))
    blk = pl.BlockSpec((tm, D_MODEL), lambda i: (i, 0))
    out = jax.ShapeDtypeStruct((m, D_MODEL), F32)
    return pl.pallas_call(
        _memkv_kernel,
        grid=(m // tm,),
        in_specs=[blk, wspec, wspec],
        out_specs=[blk, blk],
        out_shape=[out, out],
        compiler_params=_params("parallel"),
        name="mem_kv",
    )(mem2d, w_mk, w_mv)


def _memattn_kernel(x_ref, mk_ref, mv_ref, wq_ref, wo_ref, g_ref, b_ref, y_ref):
    x = x_ref[0]
    q = jnp.dot(x.astype(BF16), wq_ref[...], preferred_element_type=F32)
    qb = (q * (MEM_HEAD_DIM ** -0.5)).astype(BF16)
    mk = mk_ref[0].astype(BF16)
    mv = mv_ref[0].astype(BF16)
    outs = []
    for h in range(MEM_HEADS):
        sl = slice(h * MEM_HEAD_DIM, (h + 1) * MEM_HEAD_DIM)
        s = lax.dot_general(qb[:, sl], mk[:, sl], (((1,), (1,)), ((), ())), preferred_element_type=F32)
        p = jnp.exp(s - jnp.max(s, axis=1, keepdims=True))
        o = jnp.dot(p.astype(BF16), mv[:, sl], preferred_element_type=F32)
        outs.append(o / jnp.sum(p, axis=1, keepdims=True))
    o = jnp.concatenate(outs, axis=1).astype(BF16)
    att = jnp.dot(o, wo_ref[...], preferred_element_type=F32)
    y_ref[0] = _layer_norm_rows(DN_ALPHA * x + att, g_ref[...], b_ref[...])


def _mem_attention(x3d, mk, mv, w_mq, w_mo, g, b, tm):
    bsz, t, _ = x3d.shape
    n_mem = mk.shape[1]
    row = pl.BlockSpec((1, D_MODEL), lambda bi, i: (0, 0))
    wspec = pl.BlockSpec((D_MODEL, D_MODEL), lambda bi, i: (0, 0))
    return pl.pallas_call(
        _memattn_kernel,
        grid=(bsz, t // tm),
        in_specs=[
            pl.BlockSpec((1, tm, D_MODEL), lambda bi, i: (bi, i, 0)),
            pl.BlockSpec((1, n_mem, D_MODEL), lambda bi, i: (bi, 0, 0)),
            pl.BlockSpec((1, n_mem, D_MODEL), lambda bi, i: (bi, 0, 0)),
            wspec, wspec, row, row,
        ],
        out_specs=pl.BlockSpec((1, tm, D_MODEL), lambda bi, i: (bi, i, 0)),
        out_shape=jax.ShapeDtypeStruct((bsz, t, D_MODEL), F32),
        compiler_params=_params("parallel", "parallel"),
        name="mem_attention_ln2",
    )(x3d, mk, mv, w_mq, w_mo, g, b)


def _tree(op, xs):
    xs = list(xs)
    while len(xs) > 1:
        xs = [op(xs[i], xs[i + 1]) if i + 1 < len(xs) else xs[i] for i in range(0, len(xs), 2)]
    return xs[0]


def _all_sublanes(op, x):
    for shift in (4, 2, 1):
        x = op(x, pltpu.roll(x, shift, 0))
    return x


def _topk_slabs(slabs, keys, k, big):
    vals, kout = [], []
    for r in range(k):
        m = _all_sublanes(jnp.maximum, _tree(jnp.maximum, slabs))
        km = _all_sublanes(jnp.minimum, _tree(jnp.minimum, [jnp.where(s == m, kk, big) for s, kk in zip(slabs, keys)]))
        vals.append(m)
        kout.append(km)
        if r + 1 < k:
            slabs = [jnp.where(kk == km, -jnp.inf, s) for s, kk in zip(slabs, keys)]
    return vals, kout


def _pack_rows(rows, sub_io):
    out = rows[0]
    for r in range(1, SUBLANES):
        out = jnp.where(sub_io == r, rows[r], out)
    return out


HALF_EXPERTS = PEER_EXPERTS // 2
HALF_SHIFT = HALF_EXPERTS.bit_length() - 1
HIGH_MASK = -65536
ROUTE_HEADS_PER_STEP = 8


def _route_kernel(x_ref, w_ref, ka_ref, kb_ref, off_ref, sh_ref, g_ref, q_scr):
    tm = x_ref.shape[0]
    xb = x_ref[...].astype(BF16)
    for c in range(2 * PEER_HEADS):
        q_scr[c] = jnp.dot(xb, w_ref[:, c * PEER_HALF:(c + 1) * PEER_HALF],
                           preferred_element_type=F32).astype(BF16)
    assert PEER_TOPK == 2 * SUBLANES
    nt = (((1,), (1,)), ((), ()))
    sub_io = lax.broadcasted_iota(I32, (SUBLANES, tm), 0)
    sub_f = sub_io.astype(F32)
    n_slabs = PEER_NKEYS // SUBLANES
    key_slabs = [sub_f + float(SUBLANES * i) for i in range(n_slabs)]
    experts = float(PEER_EXPERTS)
    big = float(PEER_TOPK * PEER_TOPK) * experts

    def head(h):
        sa = lax.dot_general(ka_ref[h], q_scr[2 * h], nt, preferred_element_type=F32)
        sb = lax.dot_general(kb_ref[h], q_scr[2 * h + 1], nt, preferred_element_type=F32)
        slabs = lambda s: [s[SUBLANES * i:SUBLANES * (i + 1)] for i in range(n_slabs)]
        va, ia = _topk_slabs(slabs(sa), key_slabs, PEER_TOPK, float(PEER_NKEYS))
        vb, ib = _topk_slabs(slabs(sb), key_slabs, PEER_TOPK, float(PEER_NKEYS))
        va_hi, ia_hi = _pack_rows(va[SUBLANES:], sub_io), _pack_rows(ia[SUBLANES:], sub_io)
        vb_lo, ib_lo = _pack_rows(vb[:SUBLANES], sub_io), _pack_rows(ib[:SUBLANES], sub_io)
        vb_hi, ib_hi = _pack_rows(vb[SUBLANES:], sub_io), _pack_rows(ib[SUBLANES:], sub_io)
        cand = [va[0] + vb_lo, va[0] + vb_hi]
        ckey = [sub_f * experts + (ia[0] * PEER_NKEYS + ib_lo),
                (sub_f + SUBLANES) * experts + (ia[0] * PEER_NKEYS + ib_hi)]
        for i in range(1, SUBLANES):
            cand.append(va[i] + vb_lo)
            ckey.append((sub_f + float(i * PEER_TOPK)) * experts + (ia[i] * PEER_NKEYS + ib_lo))
        cand.append(va_hi + vb[0])
        ckey.append((sub_f + SUBLANES) * (PEER_TOPK * experts) + (ia_hi * PEER_NKEYS + ib[0]))
        top, tkey = _topk_slabs(cand, ckey, PEER_TOPK, big)
        r0 = pl.multiple_of(h * PEER_TOPK, SUBLANES)
        es = [jnp.exp(_pack_rows(top[SUBLANES * j:SUBLANES * (j + 1)], sub_io) - top[0]) for j in range(2)]
        denom = _all_sublanes(jnp.add, es[0] + es[1])
        for j in range(2):
            expert = _pack_rows(tkey[SUBLANES * j:SUBLANES * (j + 1)], sub_io).astype(I32) & (PEER_EXPERTS - 1)
            rows = pl.ds(pl.multiple_of(r0 + SUBLANES * j, SUBLANES), SUBLANES)
            off_ref[0, rows, :] = (expert & (HALF_EXPERTS - 1)) * SUBLANES
            sh_ref[0, rows, :] = (expert >> HALF_SHIFT) * 16
            g_ref[0, rows, :] = es[j] / denom

    def head_group(i, carry):
        for k in range(ROUTE_HEADS_PER_STEP):
            head(ROUTE_HEADS_PER_STEP * i + k)
        return carry

    lax.fori_loop(0, PEER_HEADS // ROUTE_HEADS_PER_STEP, head_group, 0)


def _peer_route(x2d, w_pq, keys_a, keys_b, tm):
    m = x2d.shape[0]
    nt = m // tm
    kspec = pl.BlockSpec((PEER_HEADS, PEER_NKEYS, PEER_HALF), lambda i: (0, 0, 0))
    ospec = pl.BlockSpec((1, PEER_PICKS, tm), lambda i: (i, 0, 0))
    return pl.pallas_call(
        _route_kernel,
        grid=(nt,),
        in_specs=[pl.BlockSpec((tm, D_MODEL), lambda i: (i, 0)),
                  pl.BlockSpec((D_MODEL, 2 * PEER_HEADS * PEER_HALF), lambda i: (0, 0)),
                  kspec, kspec],
        out_specs=[ospec, ospec, ospec],
        out_shape=[jax.ShapeDtypeStruct((nt, PEER_PICKS, tm), I32),
                   jax.ShapeDtypeStruct((nt, PEER_PICKS, tm), I32),
                   jax.ShapeDtypeStruct((nt, PEER_PICKS, tm), F32)],
        scratch_shapes=[pltpu.VMEM((2 * PEER_HEADS, tm, PEER_HALF), BF16)],
        compiler_params=_params("parallel"),
        name="peer_route",
    )(x2d, w_pq, keys_a, keys_b)


def _pack_table(t):
    bits = lax.bitcast_convert_type(t.astype(BF16), jnp.uint16).astype(jnp.uint32)
    word = (bits[:HALF_EXPERTS] << 16) | bits[HALF_EXPERTS:]
    return lax.bitcast_convert_type(word, I32).reshape(HALF_EXPERTS * SUBLANES, LANES)


def _table_spec():
    return pl.BlockSpec((HALF_EXPERTS * SUBLANES, LANES), lambda i: (0, 0), pipeline_mode=pl.Buffered(1))


def _table_row(tab_ref, off, shift):
    row = tab_ref[pl.ds(pl.multiple_of(off, SUBLANES), SUBLANES), :]
    return pltpu.bitcast((row << shift) & HIGH_MASK, F32)


def _rows_to_tiles(x_ref, tiles_ref):
    for s in range(SUBLANES):
        tiles_ref[:, s, :] = x_ref[:, s * LANES:(s + 1) * LANES]


def _tiles_to_rows(tiles_ref, y_ref):
    for s in range(SUBLANES):
        y_ref[:, s * LANES:(s + 1) * LANES] = tiles_ref[:, s, :]


BIT_REVERSED = (0, 4, 2, 6, 1, 5, 3, 7)


def _sublane_sums(prods, sub_io):
    def merge(a, b, h):
        low = (sub_io & h) == 0
        if 2 * h == SUBLANES:
            return jnp.where(low, a, b) + pltpu.roll(jnp.where(low, b, a), h, 0)
        return jnp.where(low, a, pltpu.roll(b, h, 0)) + jnp.where(low, pltpu.roll(a, SUBLANES - h, 0), b)

    p = [prods[BIT_REVERSED[k]] for k in range(SUBLANES)]
    t = [merge(p[2 * k], p[2 * k + 1], 4) for k in range(4)]
    u = [merge(t[2 * k], t[2 * k + 1], 2) for k in range(2)]
    return merge(u[0], u[1], 1)


def _peer_in_kernel(off_s, sh_s, x_ref, sh_ref, g_ref, spread_ref, tab_ref, w_ref, part_ref, xt_ref):
    tm = x_ref.shape[0]
    sub_io = lax.broadcasted_iota(I32, (SUBLANES, LANES), 0)
    tok_io = lax.broadcasted_iota(I32, (PEER_PICKS, tm), 1)
    _rows_to_tiles(x_ref, xt_ref)

    def gather(t):
        x = xt_ref[t]
        for g8 in range(PEER_PICKS // SUBLANES):
            picks = [g8 * SUBLANES + s for s in range(SUBLANES)]
            prods = [x * _table_row(tab_ref, off_s[0, t, p], sh_s[0, t, p]) for p in picks]
            part_ref[t, g8 * SUBLANES:(g8 + 1) * SUBLANES, :] = _sublane_sums(prods, sub_io)

    def fold(t, h_t):
        col = jnp.sum(part_ref[t], axis=1, keepdims=True)
        return jnp.where(tok_io == t, col, h_t)

    def step(t, h_t):
        h_t = fold(t - 1, h_t)
        gather(t)
        return h_t

    gather(0)
    h_t = lax.fori_loop(1, tm, step, jnp.zeros((PEER_PICKS, tm), F32))
    h_t = fold(tm - 1, h_t)
    gelu = 0.5 * h_t * (1.0 + lax.erf(h_t * (2.0 ** -0.5)))
    w = (gelu * g_ref[0]).astype(BF16)
    tn = (((0,), (0,)), ((), ()))
    spread = spread_ref[...]
    w_rows = lax.dot_general(w, spread, tn, preferred_element_type=F32)
    low_half = lax.dot_general((sh_ref[0] >> 4).astype(BF16), spread, tn, preferred_element_type=F32)
    odd_row = (lax.broadcasted_iota(I32, (1, PACKED_ROWS * PEER_PICKS), 1) & 1).astype(F32)
    w_ref[...] = jnp.where(low_half + odd_row == 1.0, w_rows, 0.0)


OUT_TOKENS_PER_STEP = 16
PACKED_ROWS = 2 * SUBLANES
SPREAD_COLS = PACKED_ROWS * PEER_PICKS


def _peer_in(off, sh, x2d, gate, table, tm):
    nt = off.shape[0]
    m = x2d.shape[0]
    assert tm == LANES
    tspec = pl.BlockSpec((1, PEER_PICKS, tm), lambda i: (i, 0, 0))
    sspec = pl.BlockSpec((1, tm, PEER_PICKS), lambda i: (i, 0, 0), memory_space=pltpu.SMEM)
    pick = lax.broadcasted_iota(I32, (PEER_PICKS, SPREAD_COLS), 0)
    col = lax.broadcasted_iota(I32, (PEER_PICKS, SPREAD_COLS), 1)
    spread = (col // PACKED_ROWS == pick).astype(BF16)
    return pl.pallas_call(
        _peer_in_kernel,
        grid=(nt,),
        in_specs=[sspec, sspec,
                  pl.BlockSpec((tm, D_MODEL), lambda i: (i, 0)),
                  tspec, tspec,
                  pl.BlockSpec((PEER_PICKS, SPREAD_COLS), lambda i: (0, 0)),
                  _table_spec()],
        out_specs=pl.BlockSpec((tm, SPREAD_COLS), lambda i: (i, 0)),
        out_shape=jax.ShapeDtypeStruct((m, SPREAD_COLS), F32),
        scratch_shapes=[pltpu.VMEM((tm, PEER_PICKS, LANES), F32), pltpu.VMEM((tm, SUBLANES, LANES), F32)],
        compiler_params=_params("arbitrary"),
        name="peer_expert_in",
    )(off.transpose(0, 2, 1), sh.transpose(0, 2, 1), x2d, sh, gate, spread, table)


def _peer_out_kernel(off_s, w_ref, x_ref, g_ref, b_ref, tab_ref, y_ref, xt_ref):
    tm = x_ref.shape[0]
    _rows_to_tiles(x_ref, xt_ref)
    chunks = SPREAD_COLS // LANES
    lane = lax.broadcasted_iota(I32, (SUBLANES, LANES), 1)
    sub = lax.broadcasted_iota(I32, (SUBLANES, LANES), 0)
    own_chunk = (lane % PACKED_ROWS) // 2 == sub

    def token(t, w_row):
        tiles = [pltpu.bitcast(tab_ref[pl.ds(pl.multiple_of(off_s[0, t, p], SUBLANES), SUBLANES), :], BF16)
                 for p in range(PEER_PICKS)]
        lhs = jnp.concatenate([jnp.where(own_chunk, jnp.broadcast_to(w, (SUBLANES, LANES)), 0.0) for w in w_row],
                              axis=1).astype(BF16)
        out = jnp.dot(lhs, jnp.concatenate(tiles, axis=0), preferred_element_type=F32)
        xt_ref[t] = DN_ALPHA * xt_ref[t] + out

    def token_group(i, carry):
        for g in range(OUT_TOKENS_PER_STEP // SUBLANES):
            t0 = pl.multiple_of(i * OUT_TOKENS_PER_STEP + g * SUBLANES, SUBLANES)
            w8 = [w_ref[pl.ds(t0, SUBLANES), k * LANES:(k + 1) * LANES] for k in range(chunks)]
            for r in range(SUBLANES):
                token(t0 + r, [w[r:r + 1, :] for w in w8])
        return carry

    lax.fori_loop(0, tm // OUT_TOKENS_PER_STEP, token_group, 0)
    _tiles_to_rows(xt_ref, y_ref)
    y_ref[...] = _layer_norm_rows(y_ref[...], g_ref[...], b_ref[...])


def _peer_out(off, w_rows, x2d, g, b, table, tm):
    nt = off.shape[0]
    m = x2d.shape[0]
    sspec = pl.BlockSpec((1, tm, PEER_PICKS), lambda i: (i, 0, 0), memory_space=pltpu.SMEM)
    vec = pl.BlockSpec((1, D_MODEL), lambda i: (0, 0))
    xspec = pl.BlockSpec((tm, D_MODEL), lambda i: (i, 0))
    return pl.pallas_call(
        _peer_out_kernel,
        grid=(nt,),
        in_specs=[sspec, pl.BlockSpec((tm, SPREAD_COLS), lambda i: (i, 0)), xspec, vec, vec, _table_spec()],
        out_specs=xspec,
        out_shape=jax.ShapeDtypeStruct((m, D_MODEL), F32),
        scratch_shapes=[pltpu.VMEM((tm, SUBLANES, LANES), F32)],
        compiler_params=_params("arbitrary"),
        name="peer_expert_out",
    )(off.transpose(0, 2, 1), w_rows, x2d, g, b, table)


def _pick_tile(n, pref):
    t = pref
    while n % t:
        t //= 2
    return t


def _layer(x, past, mem_k, mem_v, wts):
    b, t, _ = x.shape
    m = b * t
    x2d = x.reshape(m, D_MODEL)
    tm = _pick_tile(m, 256)

    (qsb, ksb, vsb, qfx, kfx, vfx, ksbb, vsbb, kfxb, vfxb, lf) = _in_projection(
        x2d, wts["w_in_main"], wts["w_in_f"], wts["b_f"], tm)
    state = tuple(a.reshape(b, t, N_HEADS, HEAD_DIM) for a in (ksb, vsb, kfx, vfx)) + (lf.reshape(b, t, N_HEADS),)

    r3 = lambda a: a.reshape(b, t, GROUP_WIDTH)
    tq = min(QUERY_BLOCK, t)
    kblk = FOX_TILE // tq
    new_kv = [r3(ksbb), r3(vsbb), r3(kfxb), r3(vfxb)]
    lf_new = lf.reshape(b, t, N_HEADS)
    if past is None:
        p, wd = 0, tq
        main_kv, diag_kv, lf_all = new_kv, new_kv, lf_new
    else:
        p = past[0].shape[1]
        wd = -(-t // KEY_BLOCK) * KEY_BLOCK
        assert t == tq
        main_kv = [c.reshape(b, p, GROUP_WIDTH) for c in past[:4]]
        diag_kv = [jnp.pad(a, ((0, 0), (0, wd - t), (0, 0))) for a in new_kv]
        lf_all = jnp.concatenate([past[4].astype(F32), jnp.pad(lf_new, ((0, 0), (0, wd - t), (0, 0)))], axis=1)

    c_all = _forget_cumsum(lf_all)
    lk = main_kv[0].shape[1]
    heads = lambda c, n, w: c.T.reshape(b, N_HEADS // 2, 2, n, w).transpose(0, 1, 3, 2, 4)
    cq = c_all[p:p + t].reshape(t, b, N_HEADS).transpose(1, 0, 2)
    ck = heads(c_all[:lk], lk // kblk, kblk)
    ckd = heads(c_all[p:p + (t // tq) * wd], t // tq, wd)

    assert t % tq == 0 and lk % kblk == 0 and kblk % KEY_BLOCK == 0 and wd % KEY_BLOCK == 0
    assert p % kblk == 0 and (past is not None or tq % kblk == 0)
    o_sb = _stick_breaking_attention(r3(qsb), main_kv[0], main_kv[1], diag_kv[0], diag_kv[1], tq, wd, p)
    o_fx = _forgetting_attention(r3(qfx), main_kv[2], main_kv[3], diag_kv[2], diag_kv[3], cq, ck, ckd, tq, wd, p)

    x1 = _mix_out(o_sb.reshape(m, GROUP_WIDTH), o_fx.reshape(m, GROUP_WIDTH), x2d,
                  wts["w_gn"], wts["w_out"], wts["ln1_g"], wts["ln1_b"], tm)
    x2 = _mem_attention(x1.reshape(b, t, D_MODEL), mem_k, mem_v, wts["w_mq"], wts["w_mo"],
                        wts["ln2_g"], wts["ln2_b"], _pick_tile(t, 256))
    x2d2 = x2.reshape(m, D_MODEL)

    tr = LANES
    off, sh, gate = _peer_route(x2d2, wts["w_pq"], wts["keys_a"], wts["keys_b"], tr)
    w_rows = _peer_in(off, sh, x2d2, gate, wts["table_u"], tr)
    y = _peer_out(off, w_rows, x2d2, wts["ln3_g"], wts["ln3_b"], wts["table_v"], tr)
    return y.reshape(b, t, D_MODEL), state


def kernel(x_prompt, x_sample, mem_prompt, cache_sb_k, cache_sb_v, cache_fox_k, cache_fox_v, cache_fox_logf,
           cache_mem_k, cache_mem_v, w_in, b_f, w_gn, w_out, ln1_g, ln1_b, w_mq, w_mk, w_mv, w_mo, ln2_g, ln2_b,
           w_pq, peer_keys_a, peer_keys_b, peer_u, peer_v, ln3_g, ln3_b):
    depth = w_in.shape[0]
    hp, hs = x_prompt, x_sample
    bp = x_prompt.shape[0]
    n_mem = mem_prompt.shape[1]
    mix_cols = 6 * GROUP_WIDTH
    outs_p = [[] for _ in range(7)]
    outs_s = [[] for _ in range(5)]
    row = lambda a: a.reshape(1, D_MODEL)
    for l in range(depth):
        wts = {
            "w_in_main": w_in[l][:, :mix_cols].astype(BF16),
            "w_in_f": jnp.pad(w_in[l][:, mix_cols:], ((0, 0), (0, LANES - N_HEADS))).astype(BF16),
            "b_f": jnp.pad(b_f[l], (0, LANES - N_HEADS)).reshape(1, LANES),
            "w_gn": row(w_gn[l]), "w_out": w_out[l].astype(BF16),
            "ln1_g": row(ln1_g[l]), "ln1_b": row(ln1_b[l]),
            "w_mq": w_mq[l].astype(BF16), "w_mo": w_mo[l].astype(BF16),
            "ln2_g": row(ln2_g[l]), "ln2_b": row(ln2_b[l]),
            "w_pq": w_pq[l].astype(BF16),
            "keys_a": peer_keys_a[l].astype(BF16), "keys_b": peer_keys_b[l].astype(BF16),
            "table_u": _pack_table(peer_u[l]), "table_v": _pack_table(peer_v[l]),
            "ln3_g": row(ln3_g[l]), "ln3_b": row(ln3_b[l]),
        }
        mem2d = mem_prompt.reshape(bp * n_mem, D_MODEL)
        mk_p, mv_p = _mem_kv(mem2d, w_mk[l].astype(BF16), w_mv[l].astype(BF16), _pick_tile(bp * n_mem, 512))
        mk_p = mk_p.reshape(bp, n_mem, D_MODEL)
        mv_p = mv_p.reshape(bp, n_mem, D_MODEL)
        hp, st_p = _layer(hp, None, mk_p, mv_p, wts)
        bs = x_sample.shape[0]
        past = (cache_sb_k[l], cache_sb_v[l], cache_fox_k[l], cache_fox_v[l], cache_fox_logf[l])
        hs, st_s = _layer(hs, past, cache_mem_k[l].reshape(bs, -1, D_MODEL), cache_mem_v[l].reshape(bs, -1, D_MODEL), wts)
        for i in range(5):
            outs_p[i].append(st_p[i])
            outs_s[i].append(st_s[i])
        outs_p[5].append(mk_p.reshape(bp, n_mem, MEM_HEADS, MEM_HEAD_DIM))
        outs_p[6].append(mv_p.reshape(bp, n_mem, MEM_HEADS, MEM_HEAD_DIM))
    stack = lambda xs: jnp.stack(xs)
    return (hp, hs) + tuple(stack(o) for o in outs_p) + tuple(stack(o) for o in outs_s)
```

```python
import functools

import jax
import jax.numpy as jnp
from jax import lax
from jax.experimental import pallas as pl
from jax.experimental.pallas import tpu as pltpu

F32 = jnp.float32
BF16 = jnp.bfloat16
I32 = jnp.int32

D_MODEL = 1024
HEAD_DIM = 64
N_HEADS = 8
GROUP_WIDTH = N_HEADS * HEAD_DIM
MEM_HEADS = 4
MEM_HEAD_DIM = D_MODEL // MEM_HEADS
PEER_HEADS = 8
PEER_NKEYS = 128
PEER_TOPK = 16
PEER_HALF = 128
PEER_PICKS = PEER_HEADS * PEER_TOPK
PEER_EXPERTS = PEER_NKEYS * PEER_NKEYS
DN_ALPHA = 2.0 ** 0.25
LN_EPS = 1e-5
GN_EPS = 1e-6

LANES = 128
SUBLANES = 8
KEY_BLOCK = 128
QUERY_BLOCK = 256
FOX_TILE = 256 * 256
ATTN_PAIRS = 4
SB_PAIRS = 4
VMEM_LIMIT = 56 * 1024 * 1024


def _params(*sem):
    return pltpu.CompilerParams(dimension_semantics=sem, vmem_limit_bytes=VMEM_LIMIT)


def _log_sigmoid(x):
    return jnp.minimum(x, 0.0) - jnp.log1p(jnp.exp(-jnp.abs(x)))


def _layer_norm_rows(r, g, b):
    mu = jnp.mean(r, axis=-1, keepdims=True)
    d = r - mu
    var = jnp.mean(d * d, axis=-1, keepdims=True)
    return d * lax.rsqrt(var + LN_EPS) * g + b


def _inproj_kernel(x_ref, w_ref, wf_ref, bf_ref,
                   qsb_ref, ksb_ref, vsb_ref, qfx_ref, kfx_ref, vfx_ref,
                   ksbb_ref, vsbb_ref, kfxb_ref, vfxb_ref, lf_ref):
    xb = x_ref[...].astype(BF16)

    def proj(j):
        return jnp.dot(xb, w_ref[:, j * GROUP_WIDTH:(j + 1) * GROUP_WIDTH], preferred_element_type=F32)

    scale = HEAD_DIM ** -0.5
    qsb_ref[...] = (proj(0) * scale).astype(BF16)
    k = proj(1)
    ksb_ref[...] = k
    ksbb_ref[...] = k.astype(BF16)
    v = proj(2)
    vsb_ref[...] = v
    vsbb_ref[...] = v.astype(BF16)
    qfx_ref[...] = (proj(3) * scale).astype(BF16)
    k = proj(4)
    kfx_ref[...] = k
    kfxb_ref[...] = k.astype(BF16)
    v = proj(5)
    vfx_ref[...] = v
    vfxb_ref[...] = v.astype(BF16)
    f = jnp.dot(xb, wf_ref[...], preferred_element_type=F32) + bf_ref[...]
    lf_ref[...] = _log_sigmoid(f)[:, :N_HEADS]


def _in_projection(x2d, w_main, w_f, b_f, tm):
    m = x2d.shape[0]
    f32o = jax.ShapeDtypeStruct((m, GROUP_WIDTH), F32)
    bf16o = jax.ShapeDtypeStruct((m, GROUP_WIDTH), BF16)
    blk = pl.BlockSpec((tm, GROUP_WIDTH), lambda i: (i, 0))
    return pl.pallas_call(
        _inproj_kernel,
        grid=(m // tm,),
        in_specs=[
            pl.BlockSpec((tm, D_MODEL), lambda i: (i, 0)),
            pl.BlockSpec((D_MODEL, 6 * GROUP_WIDTH), lambda i: (0, 0)),
            pl.BlockSpec((D_MODEL, LANES), lambda i: (0, 0)),
            pl.BlockSpec((1, LANES), lambda i: (0, 0)),
        ],
        out_specs=[blk] * 10 + [pl.BlockSpec((tm, N_HEADS), lambda i: (i, 0))],
        out_shape=[bf16o, f32o, f32o, bf16o, f32o, f32o, bf16o, bf16o, bf16o, bf16o,
                   jax.ShapeDtypeStruct((m, N_HEADS), F32)],
        compiler_params=_params("parallel"),
        name="in_projection",
    )(x2d, w_main, w_f, b_f)


def _cumsum_kernel(lf_ref, tri_ref, c_ref):
    l, cols = lf_ref.shape

    def chunk(i, carry):
        r0 = pl.multiple_of(i * KEY_BLOCK, KEY_BLOCK)
        v = lf_ref[pl.ds(r0, KEY_BLOCK), :]
        hi = v.astype(BF16)
        r1 = v - hi.astype(F32)
        mid = r1.astype(BF16)
        lo = (r1 - mid.astype(F32)).astype(BF16)
        parts = jnp.concatenate([hi, mid, lo], axis=1)
        s = jnp.dot(tri_ref[...], parts, preferred_element_type=F32)
        c = s[:, :cols] + s[:, cols:2 * cols] + s[:, 2 * cols:] + carry
        c_ref[pl.ds(r0, KEY_BLOCK), :] = c
        return c[KEY_BLOCK - 1:, :]

    lax.fori_loop(0, l // KEY_BLOCK, chunk, jnp.zeros((1, cols), F32))


def _forget_cumsum(lf):
    b, l, _ = lf.shape
    cols = b * N_HEADS
    r = lax.broadcasted_iota(I32, (KEY_BLOCK, KEY_BLOCK), 0)
    c = lax.broadcasted_iota(I32, (KEY_BLOCK, KEY_BLOCK), 1)
    tri = (c <= r).astype(BF16)
    return pl.pallas_call(
        _cumsum_kernel,
        grid=(1,),
        in_specs=[pl.BlockSpec((l, cols), lambda i: (0, 0)),
                  pl.BlockSpec((KEY_BLOCK, KEY_BLOCK), lambda i: (0, 0))],
        out_specs=pl.BlockSpec((l, cols), lambda i: (0, 0)),
        out_shape=jax.ShapeDtypeStruct((l, cols), F32),
        compiler_params=_params("arbitrary"),
        name="forget_cumsum",
    )(lf.transpose(1, 0, 2).reshape(l, cols), tri)


def _head_masks(width):
    lane = lax.broadcasted_iota(I32, (1, width), 1)
    return lane < HEAD_DIM


def _sb_kernel(q_ref, k_ref, v_ref, kd_ref, vd_ref, tri_ref, o_ref, *, tq, past):
    i = pl.program_id(2)
    first = _head_masks(LANES)
    lanes = [slice(pp * LANES, (pp + 1) * LANES) for pp in range(SB_PAIRS)]
    qhs = []
    for pp in range(SB_PAIRS):
        q = q_ref[0, :, lanes[pp]]
        zero_q = jnp.zeros_like(q)
        qhs.append((jnp.where(first, q, zero_q), jnp.where(first, zero_q, q)))
    q_pos0 = past + i * tq
    diag = q_pos0 // KEY_BLOCK
    n_diag = kd_ref.shape[1] // KEY_BLOCK
    tri = tri_ref[...]

    def block(j, runs, accs, masked):
        out = [pair_block(pp, j, runs[pp], accs[pp], masked) for pp in range(SB_PAIRS)]
        return tuple(o[0] for o in out), tuple(o[1] for o in out)

    def pair_block(pp, j, run, acc, masked):
        qh = qhs[pp]
        if masked:
            kb = kd_ref[0, j * KEY_BLOCK:(j + 1) * KEY_BLOCK, lanes[pp]]
            vb = vd_ref[0, j * KEY_BLOCK:(j + 1) * KEY_BLOCK, lanes[pp]]
            kpos = j * KEY_BLOCK + lax.broadcasted_iota(I32, (tq, KEY_BLOCK), 1)
            qpos = lax.broadcasted_iota(I32, (tq, KEY_BLOCK), 0)
            mask = kpos < qpos
        else:
            k0 = pl.multiple_of(j * KEY_BLOCK, KEY_BLOCK)
            kb = k_ref[0, pl.ds(k0, KEY_BLOCK), lanes[pp]].astype(BF16)
            vb = v_ref[0, pl.ds(k0, KEY_BLOCK), lanes[pp]].astype(BF16)
        ws = []
        new_run = []
        for h in range(2):
            z = lax.dot_general(qh[h], kb, (((1,), (1,)), ((), ())), preferred_element_type=F32)
            sp = jnp.maximum(z, 0.0) + jnp.log(1.0 + jnp.exp(-jnp.abs(z)))
            log_beta = z - sp
            if masked:
                sp = jnp.where(mask, sp, 0.0)
            hi = sp.astype(BF16)
            lo = (sp - hi.astype(F32)).astype(BF16)
            c = jnp.dot(jnp.concatenate([hi, lo], axis=1), tri, preferred_element_type=F32)
            w = jnp.exp(log_beta + c[:, :KEY_BLOCK] + run[h])
            if masked:
                w = jnp.where(mask, w, 0.0)
            new_run.append(run[h] + c[:, KEY_BLOCK:])
            ws.append(w.astype(BF16))
        zero_v = jnp.zeros_like(vb)
        v2 = jnp.concatenate([jnp.where(first, vb, zero_v), jnp.where(first, zero_v, vb)], axis=0)
        acc = acc + jnp.dot(jnp.concatenate(ws, axis=1), v2, preferred_element_type=F32)
        return tuple(new_run), acc

    def alive(runs):
        top = _tree(jnp.maximum, [r for run in runs for r in run])
        return (jnp.max(top) > EXP_UNDERFLOW).astype(I32)

    zeros = jnp.zeros((tq, KEY_BLOCK), F32)
    runs, accs = ((zeros, zeros),) * SB_PAIRS, (jnp.zeros((tq, LANES), F32),) * SB_PAIRS
    for d in reversed(range(n_diag)):
        runs, accs = block(d, runs, accs, True)

    def cond(state):
        it, live, _, _ = state
        return (it < diag) & (live > 0)

    def body(state):
        it, _, runs, accs = state
        runs, accs = block(diag - 1 - it, runs, accs, False)
        return it + 1, alive(runs), runs, accs

    _, _, _, accs = lax.while_loop(cond, body, (jnp.int32(0), alive(runs), runs, accs))
    for pp in range(SB_PAIRS):
        o_ref[0, :, lanes[pp]] = accs[pp]


EXP_UNDERFLOW = -105.0


def _cumsum_rhs():
    r = lax.broadcasted_iota(I32, (2 * KEY_BLOCK, 2 * KEY_BLOCK), 0) % KEY_BLOCK
    c = lax.broadcasted_iota(I32, (2 * KEY_BLOCK, 2 * KEY_BLOCK), 1)
    return -((c >= KEY_BLOCK) | (r > c)).astype(BF16)


def _stick_breaking_attention(q, k, v, kd, vd, tq, wd, past):
    b, t, _ = q.shape
    lk = k.shape[1]
    width = SB_PAIRS * LANES
    main = pl.BlockSpec((1, lk, width), lambda bi, hp, i: (bi, 0, hp))
    diag = pl.BlockSpec((1, wd, width), lambda bi, hp, i: (bi, i, hp))
    return pl.pallas_call(
        functools.partial(_sb_kernel, tq=tq, past=past),
        grid=(b, GROUP_WIDTH // width, t // tq),
        in_specs=[
            pl.BlockSpec((1, tq, width), lambda bi, hp, i: (bi, i, hp)),
            main, main, diag, diag,
            pl.BlockSpec((2 * KEY_BLOCK, 2 * KEY_BLOCK), lambda bi, hp, i: (0, 0)),
        ],
        out_specs=pl.BlockSpec((1, tq, width), lambda bi, hp, i: (bi, i, hp)),
        out_shape=jax.ShapeDtypeStruct((b, t, GROUP_WIDTH), F32),
        compiler_params=_params("parallel", "parallel", "arbitrary"),
        name="stick_breaking_attention",
    )(q, k, v, kd, vd, _cumsum_rhs())


def _fox_kernel(q_ref, k_ref, v_ref, kd_ref, vd_ref, cq_ref, ck_ref, ckd_ref, o_ref, *, tq, past):
    i = pl.program_id(2)
    hp = pl.program_id(1)
    first = _head_masks(LANES)
    kblk = ck_ref.shape[-1]
    wd = kd_ref.shape[1]
    diag = (past + i * tq) // kblk
    cq_all = cq_ref[0]
    head_lane = lax.broadcasted_iota(I32, (1, N_HEADS), 1)
    lanes = [slice(pp * LANES, (pp + 1) * LANES) for pp in range(ATTN_PAIRS)]
    qh, cq = [], []
    for pp in range(ATTN_PAIRS):
        q = q_ref[0, :, lanes[pp]]
        zero_q = jnp.zeros_like(q)
        qh.append((jnp.where(first, q, zero_q), jnp.where(first, zero_q, q)))
        head0 = 2 * (ATTN_PAIRS * hp + pp)
        cq.append([jnp.sum(jnp.where(head_lane == head0 + h, cq_all, 0.0), axis=1, keepdims=True) for h in range(2)])

    def pair_block(pp, j, carry, masked):
        ms, ls, acc = carry
        if masked:
            kb, vb, ck = kd_ref[0, :, lanes[pp]], vd_ref[0, :, lanes[pp]], ckd_ref[0, pp, 0]
            mask = lax.broadcasted_iota(I32, (tq, wd), 1) <= lax.broadcasted_iota(I32, (tq, wd), 0)
        else:
            k0 = pl.multiple_of(j * kblk, kblk)
            kb = k_ref[0, pl.ds(k0, kblk), lanes[pp]].astype(BF16)
            vb = v_ref[0, pl.ds(k0, kblk), lanes[pp]].astype(BF16)
            ck = ck_ref[0, pp, j]
        ps, new_m, new_l, scales = [], [], [], []
        for h in range(2):
            s = lax.dot_general(qh[pp][h], kb, (((1,), (1,)), ((), ())), preferred_element_type=F32)
            s = s + cq[pp][h] - ck[h:h + 1, :]
            if masked:
                s = jnp.where(mask, s, -jnp.inf)
            m = jnp.maximum(ms[h], jnp.max(s, axis=1, keepdims=True))
            p = jnp.exp(s - m)
            a = jnp.exp(ms[h] - m)
            new_m.append(m)
            new_l.append(a * ls[h] + jnp.sum(p, axis=1, keepdims=True))
            scales.append(a)
            ps.append(p.astype(BF16))
        zero_v = jnp.zeros_like(vb)
        v2 = jnp.concatenate([jnp.where(first, vb, zero_v), jnp.where(first, zero_v, vb)], axis=0)
        pv = jnp.dot(jnp.concatenate(ps, axis=1), v2, preferred_element_type=F32)
        acc = acc * jnp.where(first, scales[0], scales[1]) + pv
        return (tuple(new_m), tuple(new_l), acc)

    def block(j, carries, masked):
        return tuple(pair_block(pp, j, carries[pp], masked) for pp in range(ATTN_PAIRS))

    neg = jnp.full((tq, 1), -jnp.inf, F32)
    zero = jnp.zeros((tq, 1), F32)
    carries = block(0, (((neg, neg), (zero, zero), jnp.zeros((tq, LANES), F32)),) * ATTN_PAIRS, True)

    def body(it, carries):
        return block(diag - 1 - it, carries, False)

    carries = lax.fori_loop(0, diag, body, carries)
    for pp in range(ATTN_PAIRS):
        _, ls, acc = carries[pp]
        o_ref[0, :, lanes[pp]] = acc / jnp.where(first, ls[0], ls[1])


def _forgetting_attention(q, k, v, kd, vd, cq, ck, ckd, tq, wd, past):
    b, t, _ = q.shape
    lk = k.shape[1]
    width = ATTN_PAIRS * LANES
    groups = GROUP_WIDTH // width
    main = pl.BlockSpec((1, lk, width), lambda bi, hp, i: (bi, 0, hp))
    diag = pl.BlockSpec((1, wd, width), lambda bi, hp, i: (bi, i, hp))
    return pl.pallas_call(
        functools.partial(_fox_kernel, tq=tq, past=past),
        grid=(b, groups, t // tq),
        in_specs=[
            pl.BlockSpec((1, tq, width), lambda bi, hp, i: (bi, i, hp)),
            main, main, diag, diag,
            pl.BlockSpec((1, tq, N_HEADS), lambda bi, hp, i: (bi, i, 0)),
            pl.BlockSpec((1, ATTN_PAIRS) + ck.shape[2:], lambda bi, hp, i: (bi, hp, 0, 0, 0)),
            pl.BlockSpec((1, ATTN_PAIRS, 1, 2, wd), lambda bi, hp, i: (bi, hp, i, 0, 0)),
        ],
        out_specs=pl.BlockSpec((1, tq, width), lambda bi, hp, i: (bi, i, hp)),
        out_shape=jax.ShapeDtypeStruct((b, t, GROUP_WIDTH), F32),
        compiler_params=_params("parallel", "parallel", "arbitrary"),
        name="forgetting_attention",
    )(q, k, v, kd, vd, cq, ck, ckd)


def _mixout_kernel(osb_ref, ofx_ref, x_ref, gn_ref, w_ref, g_ref, b_ref, y_ref):
    def rms(o, g):
        return o * lax.rsqrt(jnp.mean(o * o, axis=-1, keepdims=True) + GN_EPS) * g

    gn = gn_ref[...]
    o = jnp.concatenate([rms(osb_ref[...], gn[:, :GROUP_WIDTH]), rms(ofx_ref[...], gn[:, GROUP_WIDTH:])], axis=1)
    mix = jnp.dot(o.astype(BF16), w_ref[...], preferred_element_type=F32)
    y_ref[...] = _layer_norm_rows(DN_ALPHA * x_ref[...] + mix, g_ref[...], b_ref[...])


def _mix_out(osb, ofx, x2d, w_gn, w_out, g, b, tm):
    m = x2d.shape[0]
    row = pl.BlockSpec((1, D_MODEL), lambda i: (0, 0))
    return pl.pallas_call(
        _mixout_kernel,
        grid=(m // tm,),
        in_specs=[
            pl.BlockSpec((tm, GROUP_WIDTH), lambda i: (i, 0)),
            pl.BlockSpec((tm, GROUP_WIDTH), lambda i: (i, 0)),
            pl.BlockSpec((tm, D_MODEL), lambda i: (i, 0)),
            row,
            pl.BlockSpec((D_MODEL, D_MODEL), lambda i: (0, 0)),
            row, row,
        ],
        out_specs=pl.BlockSpec((tm, D_MODEL), lambda i: (i, 0)),
        out_shape=jax.ShapeDtypeStruct((m, D_MODEL), F32),
        compiler_params=_params("parallel"),
        name="mix_out_ln1",
    )(osb, ofx, x2d, w_gn, w_out, g, b)


def _memkv_kernel(m_ref, wk_ref, wv_ref, k_ref, v_ref):
    mb = m_ref[...].astype(BF16)
    k_ref[...] = jnp.dot(mb, wk_ref[...], preferred_element_type=F32)
    v_ref[...] = jnp.dot(mb, wv_ref[...], preferred_element_type=F32)


def _mem_kv(mem2d, w_mk, w_mv, tm):
    m = mem2d.shape[0]
    wspec = pl.BlockSpec((D_MODEL, D_MODEL), lambda i: (0, 0))
    blk = pl.BlockSpec((tm, D_MODEL), lambda i: (i, 0))
    out = jax.ShapeDtypeStruct((m, D_MODEL), F32)
    return pl.pallas_call(
        _memkv_kernel,
        grid=(m // tm,),
        in_specs=[blk, wspec, wspec],
        out_specs=[blk, blk],
        out_shape=[out, out],
        compiler_params=_params("parallel"),
        name="mem_kv",
    )(mem2d, w_mk, w_mv)


def _memattn_kernel(x_ref, mk_ref, mv_ref, wq_ref, wo_ref, g_ref, b_ref, y_ref):
    x = x_ref[0]
    q = jnp.dot(x.astype(BF16), wq_ref[...], preferred_element_type=F32)
    qb = (q * (MEM_HEAD_DIM ** -0.5)).astype(BF16)
    mk = mk_ref[0].astype(BF16)
    mv = mv_ref[0].astype(BF16)
    outs = []
    for h in range(MEM_HEADS):
        sl = slice(h * MEM_HEAD_DIM, (h + 1) * MEM_HEAD_DIM)
        s = lax.dot_general(qb[:, sl], mk[:, sl], (((1,), (1,)), ((), ())), preferred_element_type=F32)
        p = jnp.exp(s - jnp.max(s, axis=1, keepdims=True))
        o = jnp.dot(p.astype(BF16), mv[:, sl], preferred_element_type=F32)
        outs.append(o / jnp.sum(p, axis=1, keepdims=True))
    o = jnp.concatenate(outs, axis=1).astype(BF16)
    att = jnp.dot(o, wo_ref[...], preferred_element_type=F32)
    y_ref[0] = _layer_norm_rows(DN_ALPHA * x + att, g_ref[...], b_ref[...])


def _mem_attention(x3d, mk, mv, w_mq, w_mo, g, b, tm):
    bsz, t, _ = x3d.shape
    n_mem = mk.shape[1]
    row = pl.BlockSpec((1, D_MODEL), lambda bi, i: (0, 0))
    wspec = pl.BlockSpec((D_MODEL, D_MODEL), lambda bi, i: (0, 0))
    return pl.pallas_call(
        _memattn_kernel,
        grid=(bsz, t // tm),
        in_specs=[
            pl.BlockSpec((1, tm, D_MODEL), lambda bi, i: (bi, i, 0)),
            pl.BlockSpec((1, n_mem, D_MODEL), lambda bi, i: (bi, 0, 0)),
            pl.BlockSpec((1, n_mem, D_MODEL), lambda bi, i: (bi, 0, 0)),
            wspec, wspec, row, row,
        ],
        out_specs=pl.BlockSpec((1, tm, D_MODEL), lambda bi, i: (bi, i, 0)),
        out_shape=jax.ShapeDtypeStruct((bsz, t, D_MODEL), F32),
        compiler_params=_params("parallel", "parallel"),
        name="mem_attention_ln2",
    )(x3d, mk, mv, w_mq, w_mo, g, b)


def _tree(op, xs):
    xs = list(xs)
    while len(xs) > 1:
        xs = [op(xs[i], xs[i + 1]) if i + 1 < len(xs) else xs[i] for i in range(0, len(xs), 2)]
    return xs[0]


def _all_sublanes(op, x):
    for shift in (4, 2, 1):
        x = op(x, pltpu.roll(x, shift, 0))
    return x


def _topk_slabs(slabs, keys, k, big):
    vals, kout = [], []
    for r in range(k):
        m = _all_sublanes(jnp.maximum, _tree(jnp.maximum, slabs))
        km = _all_sublanes(jnp.minimum, _tree(jnp.minimum, [jnp.where(s == m, kk, big) for s, kk in zip(slabs, keys)]))
        vals.append(m)
        kout.append(km)
        if r + 1 < k:
            slabs = [jnp.where(kk == km, -jnp.inf, s) for s, kk in zip(slabs, keys)]
    return vals, kout


def _pack_rows(rows, sub_io):
    out = rows[0]
    for r in range(1, SUBLANES):
        out = jnp.where(sub_io == r, rows[r], out)
    return out


HALF_EXPERTS = PEER_EXPERTS // 2
HALF_SHIFT = HALF_EXPERTS.bit_length() - 1
HIGH_MASK = -65536
ROUTE_HEADS_PER_STEP = 8


def _route_kernel(x_ref, w_ref, ka_ref, kb_ref, off_ref, sh_ref, g_ref, q_scr):
    tm = x_ref.shape[0]
    xb = x_ref[...].astype(BF16)
    for c in range(2 * PEER_HEADS):
        q_scr[c] = jnp.dot(xb, w_ref[:, c * PEER_HALF:(c + 1) * PEER_HALF],
                           preferred_element_type=F32).astype(BF16)
    assert PEER_TOPK == 2 * SUBLANES
    nt = (((1,), (1,)), ((), ()))
    sub_io = lax.broadcasted_iota(I32, (SUBLANES, tm), 0)
    sub_f = sub_io.astype(F32)
    n_slabs = PEER_NKEYS // SUBLANES
    key_slabs = [sub_f + float(SUBLANES * i) for i in range(n_slabs)]
    experts = float(PEER_EXPERTS)
    big = float(PEER_TOPK * PEER_TOPK) * experts

    def head(h):
        sa = lax.dot_general(ka_ref[h], q_scr[2 * h], nt, preferred_element_type=F32)
        sb = lax.dot_general(kb_ref[h], q_scr[2 * h + 1], nt, preferred_element_type=F32)
        slabs = lambda s: [s[SUBLANES * i:SUBLANES * (i + 1)] for i in range(n_slabs)]
        va, ia = _topk_slabs(slabs(sa), key_slabs, PEER_TOPK, float(PEER_NKEYS))
        vb, ib = _topk_slabs(slabs(sb), key_slabs, PEER_TOPK, float(PEER_NKEYS))
        va_hi, ia_hi = _pack_rows(va[SUBLANES:], sub_io), _pack_rows(ia[SUBLANES:], sub_io)
        vb_lo, ib_lo = _pack_rows(vb[:SUBLANES], sub_io), _pack_rows(ib[:SUBLANES], sub_io)
        vb_hi, ib_hi = _pack_rows(vb[SUBLANES:], sub_io), _pack_rows(ib[SUBLANES:], sub_io)
        cand = [va[0] + vb_lo, va[0] + vb_hi]
        ckey = [sub_f * experts + (ia[0] * PEER_NKEYS + ib_lo),
                (sub_f + SUBLANES) * experts + (ia[0] * PEER_NKEYS + ib_hi)]
        for i in range(1, SUBLANES):
            cand.append(va[i] + vb_lo)
            ckey.append((sub_f + float(i * PEER_TOPK)) * experts + (ia[i] * PEER_NKEYS + ib_lo))
        cand.append(va_hi + vb[0])
        ckey.append((sub_f + SUBLANES) * (PEER_TOPK * experts) + (ia_hi * PEER_NKEYS + ib[0]))
        top, tkey = _topk_slabs(cand, ckey, PEER_TOPK, big)
        r0 = pl.multiple_of(h * PEER_TOPK, SUBLANES)
        es = [jnp.exp(_pack_rows(top[SUBLANES * j:SUBLANES * (j + 1)], sub_io) - top[0]) for j in range(2)]
        denom = _all_sublanes(jnp.add, es[0] + es[1])
        for j in range(2):
            expert = _pack_rows(tkey[SUBLANES * j:SUBLANES * (j + 1)], sub_io).astype(I32) & (PEER_EXPERTS - 1)
            rows = pl.ds(pl.multiple_of(r0 + SUBLANES * j, SUBLANES), SUBLANES)
            off_ref[0, rows, :] = (expert & (HALF_EXPERTS - 1)) * SUBLANES
            sh_ref[0, rows, :] = (expert >> HALF_SHIFT) * 16
            g_ref[0, rows, :] = es[j] / denom

    def head_group(i, carry):
        for k in range(ROUTE_HEADS_PER_STEP):
            head(ROUTE_HEADS_PER_STEP * i + k)
        return carry

    lax.fori_loop(0, PEER_HEADS // ROUTE_HEADS_PER_STEP, head_group, 0)


def _peer_route(x2d, w_pq, keys_a, keys_b, tm):
    m = x2d.shape[0]
    nt = m // tm
    kspec = pl.BlockSpec((PEER_HEADS, PEER_NKEYS, PEER_HALF), lambda i: (0, 0, 0))
    ospec = pl.BlockSpec((1, PEER_PICKS, tm), lambda i: (i, 0, 0))
    return pl.pallas_call(
        _route_kernel,
        grid=(nt,),
        in_specs=[pl.BlockSpec((tm, D_MODEL), lambda i: (i, 0)),
                  pl.BlockSpec((D_MODEL, 2 * PEER_HEADS * PEER_HALF), lambda i: (0, 0)),
                  kspec, kspec],
        out_specs=[ospec, ospec, ospec],
        out_shape=[jax.ShapeDtypeStruct((nt, PEER_PICKS, tm), I32),
                   jax.ShapeDtypeStruct((nt, PEER_PICKS, tm), I32),
                   jax.ShapeDtypeStruct((nt, PEER_PICKS, tm), F32)],
        scratch_shapes=[pltpu.VMEM((2 * PEER_HEADS, tm, PEER_HALF), BF16)],
        compiler_params=_params("parallel"),
        name="peer_route",
    )(x2d, w_pq, keys_a, keys_b)


def _pack_table(t):
    bits = lax.bitcast_convert_type(t.astype(BF16), jnp.uint16).astype(jnp.uint32)
    word = (bits[:HALF_EXPERTS] << 16) | bits[HALF_EXPERTS:]
    return lax.bitcast_convert_type(word, I32).reshape(HALF_EXPERTS * SUBLANES, LANES)


def _table_spec():
    return pl.BlockSpec((HALF_EXPERTS * SUBLANES, LANES), lambda i: (0, 0), pipeline_mode=pl.Buffered(1))


def _table_row(tab_ref, off, shift):
    row = tab_ref[pl.ds(pl.multiple_of(off, SUBLANES), SUBLANES), :]
    return pltpu.bitcast((row << shift) & HIGH_MASK, F32)


def _rows_to_tiles(x_ref, tiles_ref):
    for s in range(SUBLANES):
        tiles_ref[:, s, :] = x_ref[:, s * LANES:(s + 1) * LANES]


def _tiles_to_rows(tiles_ref, y_ref):
    for s in range(SUBLANES):
        y_ref[:, s * LANES:(s + 1) * LANES] = tiles_ref[:, s, :]


BIT_REVERSED = (0, 4, 2, 6, 1, 5, 3, 7)


def _sublane_sums(prods, sub_io):
    def merge(a, b, h):
        low = (sub_io & h) == 0
        if 2 * h == SUBLANES:
            return jnp.where(low, a, b) + pltpu.roll(jnp.where(low, b, a), h, 0)
        return jnp.where(low, a, pltpu.roll(b, h, 0)) + jnp.where(low, pltpu.roll(a, SUBLANES - h, 0), b)

    p = [prods[BIT_REVERSED[k]] for k in range(SUBLANES)]
    t = [merge(p[2 * k], p[2 * k + 1], 4) for k in range(4)]
    u = [merge(t[2 * k], t[2 * k + 1], 2) for k in range(2)]
    return merge(u[0], u[1], 1)


def _peer_in_kernel(off_s, sh_s, x_ref, sh_ref, g_ref, spread_ref, tab_ref, w_ref, part_ref, xt_ref):
    tm = x_ref.shape[0]
    sub_io = lax.broadcasted_iota(I32, (SUBLANES, LANES), 0)
    tok_io = lax.broadcasted_iota(I32, (PEER_PICKS, tm), 1)
    _rows_to_tiles(x_ref, xt_ref)

    def gather(t):
        x = xt_ref[t]
        for g8 in range(PEER_PICKS // SUBLANES):
            picks = [g8 * SUBLANES + s for s in range(SUBLANES)]
            prods = [x * _table_row(tab_ref, off_s[0, t, p], sh_s[0, t, p]) for p in picks]
            part_ref[t, g8 * SUBLANES:(g8 + 1) * SUBLANES, :] = _sublane_sums(prods, sub_io)

    def fold(t, h_t):
        col = jnp.sum(part_ref[t], axis=1, keepdims=True)
        return jnp.where(tok_io == t, col, h_t)

    def step(t, h_t):
        h_t = fold(t - 1, h_t)
        gather(t)
        return h_t

    gather(0)
    h_t = lax.fori_loop(1, tm, step, jnp.zeros((PEER_PICKS, tm), F32))
    h_t = fold(tm - 1, h_t)
    gelu = 0.5 * h_t * (1.0 + lax.erf(h_t * (2.0 ** -0.5)))
    w = (gelu * g_ref[0]).astype(BF16)
    tn = (((0,), (0,)), ((), ()))
    spread = spread_ref[...]
    w_rows = lax.dot_general(w, spread, tn, preferred_element_type=F32)
    low_half = lax.dot_general((sh_ref[0] >> 4).astype(BF16), spread, tn, preferred_element_type=F32)
    odd_row = (lax.broadcasted_iota(I32, (1, PACKED_ROWS * PEER_PICKS), 1) & 1).astype(F32)
    w_ref[...] = jnp.where(low_half + odd_row == 1.0, w_rows, 0.0)


OUT_TOKENS_PER_STEP = 16
PACKED_ROWS = 2 * SUBLANES
SPREAD_COLS = PACKED_ROWS * PEER_PICKS


def _peer_in(off, sh, x2d, gate, table, tm):
    nt = off.shape[0]
    m = x2d.shape[0]
    assert tm == LANES
    tspec = pl.BlockSpec((1, PEER_PICKS, tm), lambda i: (i, 0, 0))
    sspec = pl.BlockSpec((1, tm, PEER_PICKS), lambda i: (i, 0, 0), memory_space=pltpu.SMEM)
    pick = lax.broadcasted_iota(I32, (PEER_PICKS, SPREAD_COLS), 0)
    col = lax.broadcasted_iota(I32, (PEER_PICKS, SPREAD_COLS), 1)
    spread = (col // PACKED_ROWS == pick).astype(BF16)
    return pl.pallas_call(
        _peer_in_kernel,
        grid=(nt,),
        in_specs=[sspec, sspec,
                  pl.BlockSpec((tm, D_MODEL), lambda i: (i, 0)),
                  tspec, tspec,
                  pl.BlockSpec((PEER_PICKS, SPREAD_COLS), lambda i: (0, 0)),
                  _table_spec()],
        out_specs=pl.BlockSpec((tm, SPREAD_COLS), lambda i: (i, 0)),
        out_shape=jax.ShapeDtypeStruct((m, SPREAD_COLS), F32),
        scratch_shapes=[pltpu.VMEM((tm, PEER_PICKS, LANES), F32), pltpu.VMEM((tm, SUBLANES, LANES), F32)],
        compiler_params=_params("arbitrary"),
        name="peer_expert_in",
    )(off.transpose(0, 2, 1), sh.transpose(0, 2, 1), x2d, sh, gate, spread, table)


def _peer_out_kernel(off_s, w_ref, x_ref, g_ref, b_ref, tab_ref, y_ref, xt_ref):
    tm = x_ref.shape[0]
    _rows_to_tiles(x_ref, xt_ref)
    chunks = SPREAD_COLS // LANES
    lane = lax.broadcasted_iota(I32, (SUBLANES, LANES), 1)
    sub = lax.broadcasted_iota(I32, (SUBLANES, LANES), 0)
    own_chunk = (lane % PACKED_ROWS) // 2 == sub

    def token(t, w_row):
        tiles = [pltpu.bitcast(tab_ref[pl.ds(pl.multiple_of(off_s[0, t, p], SUBLANES), SUBLANES), :], BF16)
                 for p in range(PEER_PICKS)]
        lhs = jnp.concatenate([jnp.where(own_chunk, jnp.broadcast_to(w, (SUBLANES, LANES)), 0.0) for w in w_row],
                              axis=1).astype(BF16)
        out = jnp.dot(lhs, jnp.concatenate(tiles, axis=0), preferred_element_type=F32)
        xt_ref[t] = DN_ALPHA * xt_ref[t] + out

    def token_group(i, carry):
        for g in range(OUT_TOKENS_PER_STEP // SUBLANES):
            t0 = pl.multiple_of(i * OUT_TOKENS_PER_STEP + g * SUBLANES, SUBLANES)
            w8 = [w_ref[pl.ds(t0, SUBLANES), k * LANES:(k + 1) * LANES] for k in range(chunks)]
            for r in range(SUBLANES):
                token(t0 + r, [w[r:r + 1, :] for w in w8])
        return carry

    lax.fori_loop(0, tm // OUT_TOKENS_PER_STEP, token_group, 0)
    _tiles_to_rows(xt_ref, y_ref)
    y_ref[...] = _layer_norm_rows(y_ref[...], g_ref[...], b_ref[...])


def _peer_out(off, w_rows, x2d, g, b, table, tm):
    nt = off.shape[0]
    m = x2d.shape[0]
    sspec = pl.BlockSpec((1, tm, PEER_PICKS), lambda i: (i, 0, 0), memory_space=pltpu.SMEM)
    vec = pl.BlockSpec((1, D_MODEL), lambda i: (0, 0))
    xspec = pl.BlockSpec((tm, D_MODEL), lambda i: (i, 0))
    return pl.pallas_call(
        _peer_out_kernel,
        grid=(nt,),
        in_specs=[sspec, pl.BlockSpec((tm, SPREAD_COLS), lambda i: (i, 0)), xspec, vec, vec, _table_spec()],
        out_specs=xspec,
        out_shape=jax.ShapeDtypeStruct((m, D_MODEL), F32),
        scratch_shapes=[pltpu.VMEM((tm, SUBLANES, LANES), F32)],
        compiler_params=_params("arbitrary"),
        name="peer_expert_out",
    )(off.transpose(0, 2, 1), w_rows, x2d, g, b, table)


def _pick_tile(n, pref):
    t = pref
    while n % t:
        t //= 2
    return t


def _layer(x, past, mem_k, mem_v, wts):
    b, t, _ = x.shape
    m = b * t
    x2d = x.reshape(m, D_MODEL)
    tm = _pick_tile(m, 256)

    (qsb, ksb, vsb, qfx, kfx, vfx, ksbb, vsbb, kfxb, vfxb, lf) = _in_projection(
        x2d, wts["w_in_main"], wts["w_in_f"], wts["b_f"], tm)
    state = tuple(a.reshape(b, t, N_HEADS, HEAD_DIM) for a in (ksb, vsb, kfx, vfx)) + (lf.reshape(b, t, N_HEADS),)

    r3 = lambda a: a.reshape(b, t, GROUP_WIDTH)
    tq = min(QUERY_BLOCK, t)
    kblk = FOX_TILE // tq
    new_kv = [r3(ksbb), r3(vsbb), r3(kfxb), r3(vfxb)]
    lf_new = lf.reshape(b, t, N_HEADS)
    if past is None:
        p, wd = 0, tq
        main_kv, diag_kv, lf_all = new_kv, new_kv, lf_new
    else:
        p = past[0].shape[1]
        wd = -(-t // KEY_BLOCK) * KEY_BLOCK
        assert t == tq
        main_kv = [c.reshape(b, p, GROUP_WIDTH) for c in past[:4]]
        diag_kv = [jnp.pad(a, ((0, 0), (0, wd - t), (0, 0))) for a in new_kv]
        lf_all = jnp.concatenate([past[4].astype(F32), jnp.pad(lf_new, ((0, 0), (0, wd - t), (0, 0)))], axis=1)

    c_all = _forget_cumsum(lf_all)
    lk = main_kv[0].shape[1]
    heads = lambda c, n, w: c.T.reshape(b, N_HEADS // 2, 2, n, w).transpose(0, 1, 3, 2, 4)
    cq = c_all[p:p + t].reshape(t, b, N_HEADS).transpose(1, 0, 2)
    ck = heads(c_all[:lk], lk // kblk, kblk)
    ckd = heads(c_all[p:p + (t // tq) * wd], t // tq, wd)

    assert t % tq == 0 and lk % kblk == 0 and kblk % KEY_BLOCK == 0 and wd % KEY_BLOCK == 0
    assert p % kblk == 0 and (past is not None or tq % kblk == 0)
    o_sb = _stick_breaking_attention(r3(qsb), main_kv[0], main_kv[1], diag_kv[0], diag_kv[1], tq, wd, p)
    o_fx = _forgetting_attention(r3(qfx), main_kv[2], main_kv[3], diag_kv[2], diag_kv[3], cq, ck, ckd, tq, wd, p)

    x1 = _mix_out(o_sb.reshape(m, GROUP_WIDTH), o_fx.reshape(m, GROUP_WIDTH), x2d,
                  wts["w_gn"], wts["w_out"], wts["ln1_g"], wts["ln1_b"], tm)
    x2 = _mem_attention(x1.reshape(b, t, D_MODEL), mem_k, mem_v, wts["w_mq"], wts["w_mo"],
                        wts["ln2_g"], wts["ln2_b"], _pick_tile(t, 256))
    x2d2 = x2.reshape(m, D_MODEL)

    tr = LANES
    off, sh, gate = _peer_route(x2d2, wts["w_pq"], wts["keys_a"], wts["keys_b"], tr)
    w_rows = _peer_in(off, sh, x2d2, gate, wts["table_u"], tr)
    y = _peer_out(off, w_rows, x2d2, wts["ln3_g"], wts["ln3_b"], wts["table_v"], tr)
    return y.reshape(b, t, D_MODEL), state


def kernel(x_prompt, x_sample, mem_prompt, cache_sb_k, cache_sb_v, cache_fox_k, cache_fox_v, cache_fox_logf,
           cache_mem_k, cache_mem_v, w_in, b_f, w_gn, w_out, ln1_g, ln1_b, w_mq, w_mk, w_mv, w_mo, ln2_g, ln2_b,
           w_pq, peer_keys_a, peer_keys_b, peer_u, peer_v, ln3_g, ln3_b):
    depth = w_in.shape[0]
    hp, hs = x_prompt, x_sample
    bp = x_prompt.shape[0]
    n_mem = mem_prompt.shape[1]
    mix_cols = 6 * GROUP_WIDTH
    outs_p = [[] for _ in range(7)]
    outs_s = [[] for _ in range(5)]
    row = lambda a: a.reshape(1, D_MODEL)
    for l in range(depth):
        wts = {
            "w_in_main": w_in[l][:, :mix_cols].astype(BF16),
            "w_in_f": jnp.pad(w_in[l][:, mix_cols:], ((0, 0), (0, LANES - N_HEADS))).astype(BF16),
            "b_f": jnp.pad(b_f[l], (0, LANES - N_HEADS)).reshape(1, LANES),
            "w_gn": row(w_gn[l]), "w_out": w_out[l].astype(BF16),
            "ln1_g": row(ln1_g[l]), "ln1_b": row(ln1_b[l]),
            "w_mq": w_mq[l].astype(BF16), "w_mo": w_mo[l].astype(BF16),
            "ln2_g": row(ln2_g[l]), "ln2_b": row(ln2_b[l]),
            "w_pq": w_pq[l].astype(BF16),
            "keys_a": peer_keys_a[l].astype(BF16), "keys_b": peer_keys_b[l].astype(BF16),
            "table_u": _pack_table(peer_u[l]), "table_v": _pack_table(peer_v[l]),
            "ln3_g": row(ln3_g[l]), "ln3_b": row(ln3_b[l]),
        }
        mem2d = mem_prompt.reshape(bp * n_mem, D_MODEL)
        mk_p, mv_p = _mem_kv(mem2d, w_mk[l].astype(BF16), w_mv[l].astype(BF16), _pick_tile(bp * n_mem, 512))
        mk_p = mk_p.reshape(bp, n_mem, D_MODEL)
        mv_p = mv_p.reshape(bp, n_mem, D_MODEL)
        hp, st_p = _layer(hp, None, mk_p, mv_p, wts)
        bs = x_sample.shape[0]
        past = (cache_sb_k[l], cache_sb_v[l], cache_fox_k[l], cache_fox_v[l], cache_fox_logf[l])
        hs, st_s = _layer(hs, past, cache_mem_k[l].reshape(bs, -1, D_MODEL), cache_mem_v[l].reshape(bs, -1, D_MODEL), wts)
        for i in range(5):
            outs_p[i].append(st_p[i])
            outs_s[i].append(st_s[i])
        outs_p[5].append(mk_p.reshape(bp, n_mem, MEM_HEADS, MEM_HEAD_DIM))
        outs_p[6].append(mv_p.reshape(bp, n_mem, MEM_HEADS, MEM_HEAD_DIM))
    stack = lambda xs: jnp.stack(xs)
    return (hp, hs) + tuple(stack(o) for o in outs_p) + tuple(stack(o) for o in outs_s)
```

```python
import functools

import jax
import jax.numpy as jnp
from jax import lax
from jax.experimental import pallas as pl
from jax.experimental.pallas import tpu as pltpu

F32 = jnp.float32
BF16 = jnp.bfloat16
I32 = jnp.int32

D_MODEL = 1024
HEAD_DIM = 64
N_HEADS = 8
GROUP_WIDTH = N_HEADS * HEAD_DIM
MEM_HEADS = 4
MEM_HEAD_DIM = D_MODEL // MEM_HEADS
PEER_HEADS = 8
PEER_NKEYS = 128
PEER_TOPK = 16
PEER_HALF = 128
PEER_PICKS = PEER_HEADS * PEER_TOPK
PEER_EXPERTS = PEER_NKEYS * PEER_NKEYS
DN_ALPHA = 2.0 ** 0.25
LN_EPS = 1e-5
GN_EPS = 1e-6

LANES = 128
SUBLANES = 8
KEY_BLOCK = 128
QUERY_BLOCK = 256
FOX_TILE = 256 * 256
ATTN_PAIRS = 4
SB_PAIRS = 4
VMEM_LIMIT = 56 * 1024 * 1024


def _params(*sem):
    return pltpu.CompilerParams(dimension_semantics=sem, vmem_limit_bytes=VMEM_LIMIT)


def _log_sigmoid(x):
    return jnp.minimum(x, 0.0) - jnp.log1p(jnp.exp(-jnp.abs(x)))


def _layer_norm_rows(r, g, b):
    mu = jnp.mean(r, axis=-1, keepdims=True)
    d = r - mu
    var = jnp.mean(d * d, axis=-1, keepdims=True)
    return d * lax.rsqrt(var + LN_EPS) * g + b


def _inproj_kernel(x_ref, w_ref, wf_ref, bf_ref,
                   qsb_ref, ksb_ref, vsb_ref, qfx_ref, kfx_ref, vfx_ref,
                   ksbb_ref, vsbb_ref, kfxb_ref, vfxb_ref, lf_ref):
    xb = x_ref[...].astype(BF16)

    def proj(j):
        return jnp.dot(xb, w_ref[:, j * GROUP_WIDTH:(j + 1) * GROUP_WIDTH], preferred_element_type=F32)

    scale = HEAD_DIM ** -0.5
    qsb_ref[...] = (proj(0) * scale).astype(BF16)
    k = proj(1)
    ksb_ref[...] = k
    ksbb_ref[...] = k.astype(BF16)
    v = proj(2)
    vsb_ref[...] = v
    vsbb_ref[...] = v.astype(BF16)
    qfx_ref[...] = (proj(3) * scale).astype(BF16)
    k = proj(4)
    kfx_ref[...] = k
    kfxb_ref[...] = k.astype(BF16)
    v = proj(5)
    vfx_ref[...] = v
    vfxb_ref[...] = v.astype(BF16)
    f = jnp.dot(xb, wf_ref[...], preferred_element_type=F32) + bf_ref[...]
    lf_ref[...] = _log_sigmoid(f)[:, :N_HEADS]


def _in_projection(x2d, w_main, w_f, b_f, tm):
    m = x2d.shape[0]
    f32o = jax.ShapeDtypeStruct((m, GROUP_WIDTH), F32)
    bf16o = jax.ShapeDtypeStruct((m, GROUP_WIDTH), BF16)
    blk = pl.BlockSpec((tm, GROUP_WIDTH), lambda i: (i, 0))
    return pl.pallas_call(
        _inproj_kernel,
        grid=(m // tm,),
        in_specs=[
            pl.BlockSpec((tm, D_MODEL), lambda i: (i, 0)),
            pl.BlockSpec((D_MODEL, 6 * GROUP_WIDTH), lambda i: (0, 0)),
            pl.BlockSpec((D_MODEL, LANES), lambda i: (0, 0)),
            pl.BlockSpec((1, LANES), lambda i: (0, 0)),
        ],
        out_specs=[blk] * 10 + [pl.BlockSpec((tm, N_HEADS), lambda i: (i, 0))],
        out_shape=[bf16o, f32o, f32o, bf16o, f32o, f32o, bf16o, bf16o, bf16o, bf16o,
                   jax.ShapeDtypeStruct((m, N_HEADS), F32)],
        compiler_params=_params("parallel"),
        name="in_projection",
    )(x2d, w_main, w_f, b_f)


def _cumsum_kernel(lf_ref, tri_ref, c_ref):
    l, cols = lf_ref.shape

    def chunk(i, carry):
        r0 = pl.multiple_of(i * KEY_BLOCK, KEY_BLOCK)
        v = lf_ref[pl.ds(r0, KEY_BLOCK), :]
        hi = v.astype(BF16)
        r1 = v - hi.astype(F32)
        mid = r1.astype(BF16)
        lo = (r1 - mid.astype(F32)).astype(BF16)
        parts = jnp.concatenate([hi, mid, lo], axis=1)
        s = jnp.dot(tri_ref[...], parts, preferred_element_type=F32)
        c = s[:, :cols] + s[:, cols:2 * cols] + s[:, 2 * cols:] + carry
        c_ref[pl.ds(r0, KEY_BLOCK), :] = c
        return c[KEY_BLOCK - 1:, :]

    lax.fori_loop(0, l // KEY_BLOCK, chunk, jnp.zeros((1, cols), F32))


def _forget_cumsum(lf):
    b, l, _ = lf.shape
    cols = b * N_HEADS
    r = lax.broadcasted_iota(I32, (KEY_BLOCK, KEY_BLOCK), 0)
    c = lax.broadcasted_iota(I32, (KEY_BLOCK, KEY_BLOCK), 1)
    tri = (c <= r).astype(BF16)
    return pl.pallas_call(
        _cumsum_kernel,
        grid=(1,),
        in_specs=[pl.BlockSpec((l, cols), lambda i: (0, 0)),
                  pl.BlockSpec((KEY_BLOCK, KEY_BLOCK), lambda i: (0, 0))],
        out_specs=pl.BlockSpec((l, cols), lambda i: (0, 0)),
        out_shape=jax.ShapeDtypeStruct((l, cols), F32),
        compiler_params=_params("arbitrary"),
        name="forget_cumsum",
    )(lf.transpose(1, 0, 2).reshape(l, cols), tri)


def _head_masks(width):
    lane = lax.broadcasted_iota(I32, (1, width), 1)
    return lane < HEAD_DIM


def _sb_kernel(q_ref, k_ref, v_ref, kd_ref, vd_ref, tri_ref, o_ref, *, tq, past):
    i = pl.program_id(2)
    first = _head_masks(LANES)
    lanes = [slice(pp * LANES, (pp + 1) * LANES) for pp in range(SB_PAIRS)]
    qhs = []
    for pp in range(SB_PAIRS):
        q = q_ref[0, :, lanes[pp]]
        zero_q = jnp.zeros_like(q)
        qhs.append((jnp.where(first, q, zero_q), jnp.where(first, zero_q, q)))
    q_pos0 = past + i * tq
    diag = q_pos0 // KEY_BLOCK
    n_diag = kd_ref.shape[1] // KEY_BLOCK
    tri = tri_ref[...]

    def block(j, runs, accs, masked):
        out = [pair_block(pp, j, runs[pp], accs[pp], masked) for pp in range(SB_PAIRS)]
        return tuple(o[0] for o in out), tuple(o[1] for o in out)

    def pair_block(pp, j, run, acc, masked):
        qh = qhs[pp]
        if masked:
            kb = kd_ref[0, j * KEY_BLOCK:(j + 1) * KEY_BLOCK, lanes[pp]]
            vb = vd_ref[0, j * KEY_BLOCK:(j + 1) * KEY_BLOCK, lanes[pp]]
            kpos = j * KEY_BLOCK + lax.broadcasted_iota(I32, (tq, KEY_BLOCK), 1)
            qpos = lax.broadcasted_iota(I32, (tq, KEY_BLOCK), 0)
            mask = kpos < qpos
        else:
            k0 = pl.multiple_of(j * KEY_BLOCK, KEY_BLOCK)
            kb = k_ref[0, pl.ds(k0, KEY_BLOCK), lanes[pp]].astype(BF16)
            vb = v_ref[0, pl.ds(k0, KEY_BLOCK), lanes[pp]].astype(BF16)
        ws = []
        new_run = []
        for h in range(2):
            z = lax.dot_general(qh[h], kb, (((1,), (1,)), ((), ())), preferred_element_type=F32)
            sp = jnp.maximum(z, 0.0) + jnp.log(1.0 + jnp.exp(-jnp.abs(z)))
            log_beta = z - sp
            if masked:
                sp = jnp.where(mask, sp, 0.0)
            hi = sp.astype(BF16)
            lo = (sp - hi.astype(F32)).astype(BF16)
            c = jnp.dot(jnp.concatenate([hi, lo], axis=1), tri, preferred_element_type=F32)
            w = jnp.exp(log_beta + c[:, :KEY_BLOCK] + run[h])
            if masked:
                w = jnp.where(mask, w, 0.0)
            new_run.append(run[h] + c[:, KEY_BLOCK:])
            ws.append(w.astype(BF16))
        zero_v = jnp.zeros_like(vb)
        v2 = jnp.concatenate([jnp.where(first, vb, zero_v), jnp.where(first, zero_v, vb)], axis=0)
        acc = acc + jnp.dot(jnp.concatenate(ws, axis=1), v2, preferred_element_type=F32)
        return tuple(new_run), acc

    def alive(runs):
        top = _tree(jnp.maximum, [r for run in runs for r in run])
        return (jnp.max(top) > EXP_UNDERFLOW).astype(I32)

    zeros = jnp.zeros((tq, KEY_BLOCK), F32)
    runs, accs = ((zeros, zeros),) * SB_PAIRS, (jnp.zeros((tq, LANES), F32),) * SB_PAIRS
    for d in reversed(range(n_diag)):
        runs, accs = block(d, runs, accs, True)

    def cond(state):
        it, live, _, _ = state
        return (it < diag) & (live > 0)

    def body(state):
        it, _, runs, accs = state
        runs, accs = block(diag - 1 - it, runs, accs, False)
        return it + 1, alive(runs), runs, accs

    _, _, _, accs = lax.while_loop(cond, body, (jnp.int32(0), alive(runs), runs, accs))
    for pp in range(SB_PAIRS):
        o_ref[0, :, lanes[pp]] = accs[pp]


EXP_UNDERFLOW = -105.0


def _cumsum_rhs():
    r = lax.broadcasted_iota(I32, (2 * KEY_BLOCK, 2 * KEY_BLOCK), 0) % KEY_BLOCK
    c = lax.broadcasted_iota(I32, (2 * KEY_BLOCK, 2 * KEY_BLOCK), 1)
    return -((c >= KEY_BLOCK) | (r > c)).astype(BF16)


def _stick_breaking_attention(q, k, v, kd, vd, tq, wd, past):
    b, t, _ = q.shape
    lk = k.shape[1]
    width = SB_PAIRS * LANES
    main = pl.BlockSpec((1, lk, width), lambda bi, hp, i: (bi, 0, hp))
    diag = pl.BlockSpec((1, wd, width), lambda bi, hp, i: (bi, i, hp))
    return pl.pallas_call(
        functools.partial(_sb_kernel, tq=tq, past=past),
        grid=(b, GROUP_WIDTH // width, t // tq),
        in_specs=[
            pl.BlockSpec((1, tq, width), lambda bi, hp, i: (bi, i, hp)),
            main, main, diag, diag,
            pl.BlockSpec((2 * KEY_BLOCK, 2 * KEY_BLOCK), lambda bi, hp, i: (0, 0)),
        ],
        out_specs=pl.BlockSpec((1, tq, width), lambda bi, hp, i: (bi, i, hp)),
        out_shape=jax.ShapeDtypeStruct((b, t, GROUP_WIDTH), F32),
        compiler_params=_params("parallel", "parallel", "arbitrary"),
        name="stick_breaking_attention",
    )(q, k, v, kd, vd, _cumsum_rhs())


def _fox_kernel(q_ref, k_ref, v_ref, kd_ref, vd_ref, cq_ref, ck_ref, ckd_ref, o_ref, *, tq, past):
    i = pl.program_id(2)
    hp = pl.program_id(1)
    first = _head_masks(LANES)
    kblk = ck_ref.shape[-1]
    wd = kd_ref.shape[1]
    diag = (past + i * tq) // kblk
    cq_all = cq_ref[0]
    head_lane = lax.broadcasted_iota(I32, (1, N_HEADS), 1)
    lanes = [slice(pp * LANES, (pp + 1) * LANES) for pp in range(ATTN_PAIRS)]
    qh, cq = [], []
    for pp in range(ATTN_PAIRS):
        q = q_ref[0, :, lanes[pp]]
        zero_q = jnp.zeros_like(q)
        qh.append((jnp.where(first, q, zero_q), jnp.where(first, zero_q, q)))
        head0 = 2 * (ATTN_PAIRS * hp + pp)
        cq.append([jnp.sum(jnp.where(head_lane == head0 + h, cq_all, 0.0), axis=1, keepdims=True) for h in range(2)])

    def pair_block(pp, j, carry, masked):
        ms, ls, acc = carry
        if masked:
            kb, vb, ck = kd_ref[0, :, lanes[pp]], vd_ref[0, :, lanes[pp]], ckd_ref[0, pp, 0]
            mask = lax.broadcasted_iota(I32, (tq, wd), 1) <= lax.broadcasted_iota(I32, (tq, wd), 0)
        else:
            k0 = pl.multiple_of(j * kblk, kblk)
            kb = k_ref[0, pl.ds(k0, kblk), lanes[pp]].astype(BF16)
            vb = v_ref[0, pl.ds(k0, kblk), lanes[pp]].astype(BF16)
            ck = ck_ref[0, pp, j]
        ps, new_m, new_l, scales = [], [], [], []
        for h in range(2):
            s = lax.dot_general(qh[pp][h], kb, (((1,), (1,)), ((), ())), preferred_element_type=F32)
            s = s + cq[pp][h] - ck[h:h + 1, :]
            if masked:
                s = jnp.where(mask, s, -jnp.inf)
            m = jnp.maximum(ms[h], jnp.max(s, axis=1, keepdims=True))
            p = jnp.exp(s - m)
            a = jnp.exp(ms[h] - m)
            new_m.append(m)
            new_l.append(a * ls[h] + jnp.sum(p, axis=1, keepdims=True))
            scales.append(a)
            ps.append(p.astype(BF16))
        zero_v = jnp.zeros_like(vb)
        v2 = jnp.concatenate([jnp.where(first, vb, zero_v), jnp.where(first, zero_v, vb)], axis=0)
        pv = jnp.dot(jnp.concatenate(ps, axis=1), v2, preferred_element_type=F32)
        acc = acc * jnp.where(first, scales[0], scales[1]) + pv
        return (tuple(new_m), tuple(new_l), acc)

    def block(j, carries, masked):
        return tuple(pair_block(pp, j, carries[pp], masked) for pp in range(ATTN_PAIRS))

    neg = jnp.full((tq, 1), -jnp.inf, F32)
    zero = jnp.zeros((tq, 1), F32)
    carries = block(0, (((neg, neg), (zero, zero), jnp.zeros((tq, LANES), F32)),) * ATTN_PAIRS, True)

    def body(it, carries):
        return block(diag - 1 - it, carries, False)

    carries = lax.fori_loop(0, diag, body, carries)
    for pp in range(ATTN_PAIRS):
        _, ls, acc = carries[pp]
        o_ref[0, :, lanes[pp]] = acc / jnp.where(first, ls[0], ls[1])


def _forgetting_attention(q, k, v, kd, vd, cq, ck, ckd, tq, wd, past):
    b, t, _ = q.shape
    lk = k.shape[1]
    width = ATTN_PAIRS * LANES
    groups = GROUP_WIDTH // width
    main = pl.BlockSpec((1, lk, width), lambda bi, hp, i: (bi, 0, hp))
    diag = pl.BlockSpec((1, wd, width), lambda bi, hp, i: (bi, i, hp))
    return pl.pallas_call(
        functools.partial(_fox_kernel, tq=tq, past=past),
        grid=(b, groups, t // tq),
        in_specs=[
            pl.BlockSpec((1, tq, width), lambda bi, hp, i: (bi, i, hp)),
            main, main, diag, diag,
            pl.BlockSpec((1, tq, N_HEADS), lambda bi, hp, i: (bi, i, 0)),
            pl.BlockSpec((1, ATTN_PAIRS) + ck.shape[2:], lambda bi, hp, i: (bi, hp, 0, 0, 0)),
            pl.BlockSpec((1, ATTN_PAIRS, 1, 2, wd), lambda bi, hp, i: (bi, hp, i, 0, 0)),
        ],
        out_specs=pl.BlockSpec((1, tq, width), lambda bi, hp, i: (bi, i, hp)),
        out_shape=jax.ShapeDtypeStruct((b, t, GROUP_WIDTH), F32),
        compiler_params=_params("parallel", "parallel", "arbitrary"),
        name="forgetting_attention",
    )(q, k, v, kd, vd, cq, ck, ckd)


def _mixout_kernel(osb_ref, ofx_ref, x_ref, gn_ref, w_ref, g_ref, b_ref, y_ref):
    def rms(o, g):
        return o * lax.rsqrt(jnp.mean(o * o, axis=-1, keepdims=True) + GN_EPS) * g

    gn = gn_ref[...]
    o = jnp.concatenate([rms(osb_ref[...], gn[:, :GROUP_WIDTH]), rms(ofx_ref[...], gn[:, GROUP_WIDTH:])], axis=1)
    mix = jnp.dot(o.astype(BF16), w_ref[...], preferred_element_type=F32)
    y_ref[...] = _layer_norm_rows(DN_ALPHA * x_ref[...] + mix, g_ref[...], b_ref[...])


def _mix_out(osb, ofx, x2d, w_gn, w_out, g, b, tm):
    m = x2d.shape[0]
    row = pl.BlockSpec((1, D_MODEL), lambda i: (0, 0))
    return pl.pallas_call(
        _mixout_kernel,
        grid=(m // tm,),
        in_specs=[
            pl.BlockSpec((tm, GROUP_WIDTH), lambda i: (i, 0)),
            pl.BlockSpec((tm, GROUP_WIDTH), lambda i: (i, 0)),
            pl.BlockSpec((tm, D_MODEL), lambda i: (i, 0)),
            row,
            pl.BlockSpec((D_MODEL, D_MODEL), lambda i: (0, 0)),
            row, row,
        ],
        out_specs=pl.BlockSpec((tm, D_MODEL), lambda i: (i, 0)),
        out_shape=jax.ShapeDtypeStruct((m, D_MODEL), F32),
        compiler_params=_params("parallel"),
        name="mix_out_ln1",
    )(osb, ofx, x2d, w_gn, w_out, g, b)


def _memkv_kernel(m_ref, wk_ref, wv_ref, k_ref, v_ref):
    mb = m_ref[...].astype(BF16)
    k_ref[...] = jnp.dot(mb, wk_ref[...], preferred_element_type=F32)
    v_ref[...] = jnp.dot(mb, wv_ref[...], preferred_element_type=F32)


def _mem_kv(mem2d, w_mk, w_mv, tm):
    m = mem2d.shape[0]
    wspec = pl.BlockSpec((D_MODEL, D_MODEL), lambda i: (0, 0))
    blk = pl.BlockSpec((tm, D_MODEL), lambda i: (i, 0))
    out = jax.ShapeDtypeStruct((m, D_MODEL), F32)
    return pl.pallas_call(
        _memkv_kernel,
        grid=(m // tm,),
        in_specs=[blk, wspec, wspec],
        out_specs=[blk, blk],
        out_shape=[out, out],
        compiler_params=_params("parallel"),
        name="mem_kv",
    )(mem2d, w_mk, w_mv)


def _memattn_kernel(x_ref, mk_ref, mv_ref, wq_ref, wo_ref, g_ref, b_ref, y_ref):
    x = x_ref[0]
    q = jnp.dot(x.astype(BF16), wq_ref[...], preferred_element_type=F32)
    qb = (q * (MEM_HEAD_DIM ** -0.5)).astype(BF16)
    mk = mk_ref[0].astype(BF16)
    mv = mv_ref[0].astype(BF16)
    outs = []
    for h in range(MEM_HEADS):
        sl = slice(h * MEM_HEAD_DIM, (h + 1) * MEM_HEAD_DIM)
        s = lax.dot_general(qb[:, sl], mk[:, sl], (((1,), (1,)), ((), ())), preferred_element_type=F32)
        p = jnp.exp(s - jnp.max(s, axis=1, keepdims=True))
        o = jnp.dot(p.astype(BF16), mv[:, sl], preferred_element_type=F32)
        outs.append(o / jnp.sum(p, axis=1, keepdims=True))
    o = jnp.concatenate(outs, axis=1).astype(BF16)
    att = jnp.dot(o, wo_ref[...], preferred_element_type=F32)
    y_ref[0] = _layer_norm_rows(DN_ALPHA * x + att, g_ref[...], b_ref[...])


def _mem_attention(x3d, mk, mv, w_mq, w_mo, g, b, tm):
    bsz, t, _ = x3d.shape
    n_mem = mk.shape[1]
    row = pl.BlockSpec((1, D_MODEL), lambda bi, i: (0, 0))
    wspec = pl.BlockSpec((D_MODEL, D_MODEL), lambda bi, i: (0, 0))
    return pl.pallas_call(
        _memattn_kernel,
        grid=(bsz, t // tm),
        in_specs=[
            pl.BlockSpec((1, tm, D_MODEL), lambda bi, i: (bi, i, 0)),
            pl.BlockSpec((1, n_mem, D_MODEL), lambda bi, i: (bi, 0, 0)),
            pl.BlockSpec((1, n_mem, D_MODEL), lambda bi, i: (bi, 0, 0)),
            wspec, wspec, row, row,
        ],
        out_specs=pl.BlockSpec((1, tm, D_MODEL), lambda bi, i: (bi, i, 0)),
        out_shape=jax.ShapeDtypeStruct((bsz, t, D_MODEL), F32),
        compiler_params=_params("parallel", "parallel"),
        name="mem_attention_ln2",
    )(x3d, mk, mv, w_mq, w_mo, g, b)


def _tree(op, xs):
    xs = list(xs)
    while len(xs) > 1:
        xs = [op(xs[i], xs[i + 1]) if i + 1 < len(xs) else xs[i] for i in range(0, len(xs), 2)]
    return xs[0]


def _all_sublanes(op, x):
    for shift in (4, 2, 1):
        x = op(x, pltpu.roll(x, shift, 0))
    return x


def _topk_slabs(slabs, keys, k, big):
    vals, kout = [], []
    for r in range(k):
        m = _all_sublanes(jnp.maximum, _tree(jnp.maximum, slabs))
        km = _all_sublanes(jnp.minimum, _tree(jnp.minimum, [jnp.where(s == m, kk, big) for s, kk in zip(slabs, keys)]))
        vals.append(m)
        kout.append(km)
        if r + 1 < k:
            slabs = [jnp.where(kk == km, -jnp.inf, s) for s, kk in zip(slabs, keys)]
    return vals, kout


def _pack_rows(rows, sub_io):
    out = rows[0]
    for r in range(1, SUBLANES):
        out = jnp.where(sub_io == r, rows[r], out)
    return out


HALF_EXPERTS = PEER_EXPERTS // 2
HALF_SHIFT = HALF_EXPERTS.bit_length() - 1
HIGH_MASK = -65536
ROUTE_HEADS_PER_STEP = 8


def _route_kernel(x_ref, w_ref, ka_ref, kb_ref, off_ref, sh_ref, g_ref, q_scr):
    tm = x_ref.shape[0]
    xb = x_ref[...].astype(BF16)
    for c in range(2 * PEER_HEADS):
        q_scr[c] = jnp.dot(xb, w_ref[:, c * PEER_HALF:(c + 1) * PEER_HALF],
                           preferred_element_type=F32).astype(BF16)
    assert PEER_TOPK == 2 * SUBLANES
    nt = (((1,), (1,)), ((), ()))
    sub_io = lax.broadcasted_iota(I32, (SUBLANES, tm), 0)
    sub_f = sub_io.astype(F32)
    n_slabs = PEER_NKEYS // SUBLANES
    key_slabs = [sub_f + float(SUBLANES * i) for i in range(n_slabs)]
    experts = float(PEER_EXPERTS)
    big = float(PEER_TOPK * PEER_TOPK) * experts

    def head(h):
        sa = lax.dot_general(ka_ref[h], q_scr[2 * h], nt, preferred_element_type=F32)
        sb = lax.dot_general(kb_ref[h], q_scr[2 * h + 1], nt, preferred_element_type=F32)
        slabs = lambda s: [s[SUBLANES * i:SUBLANES * (i + 1)] for i in range(n_slabs)]
        va, ia = _topk_slabs(slabs(sa), key_slabs, PEER_TOPK, float(PEER_NKEYS))
        vb, ib = _topk_slabs(slabs(sb), key_slabs, PEER_TOPK, float(PEER_NKEYS))
        va_hi, ia_hi = _pack_rows(va[SUBLANES:], sub_io), _pack_rows(ia[SUBLANES:], sub_io)
        vb_lo, ib_lo = _pack_rows(vb[:SUBLANES], sub_io), _pack_rows(ib[:SUBLANES], sub_io)
        vb_hi, ib_hi = _pack_rows(vb[SUBLANES:], sub_io), _pack_rows(ib[SUBLANES:], sub_io)
        cand = [va[0] + vb_lo, va[0] + vb_hi]
        ckey = [sub_f * experts + (ia[0] * PEER_NKEYS + ib_lo),
                (sub_f + SUBLANES) * experts + (ia[0] * PEER_NKEYS + ib_hi)]
        for i in range(1, SUBLANES):
            cand.append(va[i] + vb_lo)
            ckey.append((sub_f + float(i * PEER_TOPK)) * experts + (ia[i] * PEER_NKEYS + ib_lo))
        cand.append(va_hi + vb[0])
        ckey.append((sub_f + SUBLANES) * (PEER_TOPK * experts) + (ia_hi * PEER_NKEYS + ib[0]))
        top, tkey = _topk_slabs(cand, ckey, PEER_TOPK, big)
        r0 = pl.multiple_of(h * PEER_TOPK, SUBLANES)
        es = [jnp.exp(_pack_rows(top[SUBLANES * j:SUBLANES * (j + 1)], sub_io) - top[0]) for j in range(2)]
        denom = _all_sublanes(jnp.add, es[0] + es[1])
        for j in range(2):
            expert = _pack_rows(tkey[SUBLANES * j:SUBLANES * (j + 1)], sub_io).astype(I32) & (PEER_EXPERTS - 1)
            rows = pl.ds(pl.multiple_of(r0 + SUBLANES * j, SUBLANES), SUBLANES)
            off_ref[0, rows, :] = (expert & (HALF_EXPERTS - 1)) * SUBLANES
            sh_ref[0, rows, :] = (expert >> HALF_SHIFT) * 16
            g_ref[0, rows, :] = es[j] / denom

    def head_group(i, carry):
        for k in range(ROUTE_HEADS_PER_STEP):
            head(ROUTE_HEADS_PER_STEP * i + k)
        return carry

    lax.fori_loop(0, PEER_HEADS // ROUTE_HEADS_PER_STEP, head_group, 0)


def _peer_route(x2d, w_pq, keys_a, keys_b, tm):
    m = x2d.shape[0]
    nt = m // tm
    kspec = pl.BlockSpec((PEER_HEADS, PEER_NKEYS, PEER_HALF), lambda i: (0, 0, 0))
    ospec = pl.BlockSpec((1, PEER_PICKS, tm), lambda i: (i, 0, 0))
    return pl.pallas_call(
        _route_kernel,
        grid=(nt,),
        in_specs=[pl.BlockSpec((tm, D_MODEL), lambda i: (i, 0)),
                  pl.BlockSpec((D_MODEL, 2 * PEER_HEADS * PEER_HALF), lambda i: (0, 0)),
                  kspec, kspec],
        out_specs=[ospec, ospec, ospec],
        out_shape=[jax.ShapeDtypeStruct((nt, PEER_PICKS, tm), I32),
                   jax.ShapeDtypeStruct((nt, PEER_PICKS, tm), I32),
                   jax.ShapeDtypeStruct((nt, PEER_PICKS, tm), F32)],
        scratch_shapes=[pltpu.VMEM((2 * PEER_HEADS, tm, PEER_HALF), BF16)],
        compiler_params=_params("parallel"),
        name="peer_route",
    )(x2d, w_pq, keys_a, keys_b)


def _pack_table(t):
    bits = lax.bitcast_convert_type(t.astype(BF16), jnp.uint16).astype(jnp.uint32)
    word = (bits[:HALF_EXPERTS] << 16) | bits[HALF_EXPERTS:]
    return lax.bitcast_convert_type(word, I32).reshape(HALF_EXPERTS * SUBLANES, LANES)


def _table_spec():
    return pl.BlockSpec((HALF_EXPERTS * SUBLANES, LANES), lambda i: (0, 0), pipeline_mode=pl.Buffered(1))


def _table_row(tab_ref, off, shift):
    row = tab_ref[pl.ds(pl.multiple_of(off, SUBLANES), SUBLANES), :]
    return pltpu.bitcast((row << shift) & HIGH_MASK, F32)


def _rows_to_tiles(x_ref, tiles_ref):
    for s in range(SUBLANES):
        tiles_ref[:, s, :] = x_ref[:, s * LANES:(s + 1) * LANES]


def _tiles_to_rows(tiles_ref, y_ref):
    for s in range(SUBLANES):
        y_ref[:, s * LANES:(s + 1) * LANES] = tiles_ref[:, s, :]


BIT_REVERSED = (0, 4, 2, 6, 1, 5, 3, 7)


def _sublane_sums(prods, sub_io):
    def merge(a, b, h):
        low = (sub_io & h) == 0
        if 2 * h == SUBLANES:
            return jnp.where(low, a, b) + pltpu.roll(jnp.where(low, b, a), h, 0)
        return jnp.where(low, a, pltpu.roll(b, h, 0)) + jnp.where(low, pltpu.roll(a, SUBLANES - h, 0), b)

    p = [prods[BIT_REVERSED[k]] for k in range(SUBLANES)]
    t = [merge(p[2 * k], p[2 * k + 1], 4) for k in range(4)]
    u = [merge(t[2 * k], t[2 * k + 1], 2) for k in range(2)]
    return merge(u[0], u[1], 1)


def _peer_in_kernel(off_s, sh_s, x_ref, sh_ref, g_ref, spread_ref, tab_ref, w_ref, part_ref, xt_ref):
    tm = x_ref.shape[0]
    sub_io = lax.broadcasted_iota(I32, (SUBLANES, LANES), 0)
    tok_io = lax.broadcasted_iota(I32, (PEER_PICKS, tm), 1)
    _rows_to_tiles(x_ref, xt_ref)

    def gather(t):
        x = xt_ref[t]
        for g8 in range(PEER_PICKS // SUBLANES):
            picks = [g8 * SUBLANES + s for s in range(SUBLANES)]
            prods = [x * _table_row(tab_ref, off_s[0, t, p], sh_s[0, t, p]) for p in picks]
            part_ref[t, g8 * SUBLANES:(g8 + 1) * SUBLANES, :] = _sublane_sums(prods, sub_io)

    def fold(t, h_t):
        col = jnp.sum(part_ref[t], axis=1, keepdims=True)
        return jnp.where(tok_io == t, col, h_t)

    def step(t, h_t):
        h_t = fold(t - 1, h_t)
        gather(t)
        return h_t

    gather(0)
    h_t = lax.fori_loop(1, tm, step, jnp.zeros((PEER_PICKS, tm), F32))
    h_t = fold(tm - 1, h_t)
    gelu = 0.5 * h_t * (1.0 + lax.erf(h_t * (2.0 ** -0.5)))
    w = (gelu * g_ref[0]).astype(BF16)
    tn = (((0,), (0,)), ((), ()))
    spread = spread_ref[...]
    w_rows = lax.dot_general(w, spread, tn, preferred_element_type=F32)
    low_half = lax.dot_general((sh_ref[0] >> 4).astype(BF16), spread, tn, preferred_element_type=F32)
    odd_row = (lax.broadcasted_iota(I32, (1, PACKED_ROWS * PEER_PICKS), 1) & 1).astype(F32)
    w_ref[...] = jnp.where(low_half + odd_row == 1.0, w_rows, 0.0)


OUT_TOKENS_PER_STEP = 32
PACKED_ROWS = 2 * SUBLANES
SPREAD_COLS = PACKED_ROWS * PEER_PICKS


def _peer_in(off, sh, x2d, gate, table, tm):
    nt = off.shape[0]
    m = x2d.shape[0]
    assert tm == LANES
    tspec = pl.BlockSpec((1, PEER_PICKS, tm), lambda i: (i, 0, 0))
    sspec = pl.BlockSpec((1, tm, PEER_PICKS), lambda i: (i, 0, 0), memory_space=pltpu.SMEM)
    pick = lax.broadcasted_iota(I32, (PEER_PICKS, SPREAD_COLS), 0)
    col = lax.broadcasted_iota(I32, (PEER_PICKS, SPREAD_COLS), 1)
    spread = (col // PACKED_ROWS == pick).astype(BF16)
    return pl.pallas_call(
        _peer_in_kernel,
        grid=(nt,),
        in_specs=[sspec, sspec,
                  pl.BlockSpec((tm, D_MODEL), lambda i: (i, 0)),
                  tspec, tspec,
                  pl.BlockSpec((PEER_PICKS, SPREAD_COLS), lambda i: (0, 0)),
                  _table_spec()],
        out_specs=pl.BlockSpec((tm, SPREAD_COLS), lambda i: (i, 0)),
        out_shape=jax.ShapeDtypeStruct((m, SPREAD_COLS), F32),
        scratch_shapes=[pltpu.VMEM((tm, PEER_PICKS, LANES), F32), pltpu.VMEM((tm, SUBLANES, LANES), F32)],
        compiler_params=_params("arbitrary"),
        name="peer_expert_in",
    )(off.transpose(0, 2, 1), sh.transpose(0, 2, 1), x2d, sh, gate, spread, table)


def _peer_out_kernel(off_s, w_ref, x_ref, g_ref, b_ref, tab_ref, y_ref, xt_ref):
    tm = x_ref.shape[0]
    _rows_to_tiles(x_ref, xt_ref)
    chunks = SPREAD_COLS // LANES
    lane = lax.broadcasted_iota(I32, (SUBLANES, LANES), 1)
    sub = lax.broadcasted_iota(I32, (SUBLANES, LANES), 0)
    own_chunk = (lane % PACKED_ROWS) // 2 == sub

    def token(t, w_row):
        tiles = [pltpu.bitcast(tab_ref[pl.ds(pl.multiple_of(off_s[0, t, p], SUBLANES), SUBLANES), :], BF16)
                 for p in range(PEER_PICKS)]
        lhs = jnp.concatenate([jnp.where(own_chunk, jnp.broadcast_to(w, (SUBLANES, LANES)), 0.0) for w in w_row],
                              axis=1).astype(BF16)
        out = jnp.dot(lhs, jnp.concatenate(tiles, axis=0), preferred_element_type=F32)
        xt_ref[t] = DN_ALPHA * xt_ref[t] + out

    def token_group(i, carry):
        for g in range(OUT_TOKENS_PER_STEP // SUBLANES):
            t0 = pl.multiple_of(i * OUT_TOKENS_PER_STEP + g * SUBLANES, SUBLANES)
            w8 = [w_ref[pl.ds(t0, SUBLANES), k * LANES:(k + 1) * LANES] for k in range(chunks)]
            for r in range(SUBLANES):
                token(t0 + r, [w[r:r + 1, :] for w in w8])
        return carry

    lax.fori_loop(0, tm // OUT_TOKENS_PER_STEP, token_group, 0)
    _tiles_to_rows(xt_ref, y_ref)
    y_ref[...] = _layer_norm_rows(y_ref[...], g_ref[...], b_ref[...])


def _peer_out(off, w_rows, x2d, g, b, table, tm):
    nt = off.shape[0]
    m = x2d.shape[0]
    sspec = pl.BlockSpec((1, tm, PEER_PICKS), lambda i: (i, 0, 0), memory_space=pltpu.SMEM)
    vec = pl.BlockSpec((1, D_MODEL), lambda i: (0, 0))
    xspec = pl.BlockSpec((tm, D_MODEL), lambda i: (i, 0))
    return pl.pallas_call(
        _peer_out_kernel,
        grid=(nt,),
        in_specs=[sspec, pl.BlockSpec((tm, SPREAD_COLS), lambda i: (i, 0)), xspec, vec, vec, _table_spec()],
        out_specs=xspec,
        out_shape=jax.ShapeDtypeStruct((m, D_MODEL), F32),
        scratch_shapes=[pltpu.VMEM((tm, SUBLANES, LANES), F32)],
        compiler_params=_params("arbitrary"),
        name="peer_expert_out",
    )(off.transpose(0, 2, 1), w_rows, x2d, g, b, table)


def _pick_tile(n, pref):
    t = pref
    while n % t:
        t //= 2
    return t


def _layer(x, past, mem_k, mem_v, wts):
    b, t, _ = x.shape
    m = b * t
    x2d = x.reshape(m, D_MODEL)
    tm = _pick_tile(m, 256)

    (qsb, ksb, vsb, qfx, kfx, vfx, ksbb, vsbb, kfxb, vfxb, lf) = _in_projection(
        x2d, wts["w_in_main"], wts["w_in_f"], wts["b_f"], tm)
    state = tuple(a.reshape(b, t, N_HEADS, HEAD_DIM) for a in (ksb, vsb, kfx, vfx)) + (lf.reshape(b, t, N_HEADS),)

    r3 = lambda a: a.reshape(b, t, GROUP_WIDTH)
    tq = min(QUERY_BLOCK, t)
    kblk = FOX_TILE // tq
    new_kv = [r3(ksbb), r3(vsbb), r3(kfxb), r3(vfxb)]
    lf_new = lf.reshape(b, t, N_HEADS)
    if past is None:
        p, wd = 0, tq
        main_kv, diag_kv, lf_all = new_kv, new_kv, lf_new
    else:
        p = past[0].shape[1]
        wd = -(-t // KEY_BLOCK) * KEY_BLOCK
        assert t == tq
        main_kv = [c.reshape(b, p, GROUP_WIDTH) for c in past[:4]]
        diag_kv = [jnp.pad(a, ((0, 0), (0, wd - t), (0, 0))) for a in new_kv]
        lf_all = jnp.concatenate([past[4].astype(F32), jnp.pad(lf_new, ((0, 0), (0, wd - t), (0, 0)))], axis=1)

    c_all = _forget_cumsum(lf_all)
    lk = main_kv[0].shape[1]
    heads = lambda c, n, w: c.T.reshape(b, N_HEADS // 2, 2, n, w).transpose(0, 1, 3, 2, 4)
    cq = c_all[p:p + t].reshape(t, b, N_HEADS).transpose(1, 0, 2)
    ck = heads(c_all[:lk], lk // kblk, kblk)
    ckd = heads(c_all[p:p + (t // tq) * wd], t // tq, wd)

    assert t % tq == 0 and lk % kblk == 0 and kblk % KEY_BLOCK == 0 and wd % KEY_BLOCK == 0
    assert p % kblk == 0 and (past is not None or tq % kblk == 0)
    o_sb = _stick_breaking_attention(r3(qsb), main_kv[0], main_kv[1], diag_kv[0], diag_kv[1], tq, wd, p)
    o_fx = _forgetting_attention(r3(qfx), main_kv[2], main_kv[3], diag_kv[2], diag_kv[3], cq, ck, ckd, tq, wd, p)

    x1 = _mix_out(o_sb.reshape(m, GROUP_WIDTH), o_fx.reshape(m, GROUP_WIDTH), x2d,
                  wts["w_gn"], wts["w_out"], wts["ln1_g"], wts["ln1_b"], tm)
    x2 = _mem_attention(x1.reshape(b, t, D_MODEL), mem_k, mem_v, wts["w_mq"], wts["w_mo"],
                        wts["ln2_g"], wts["ln2_b"], _pick_tile(t, 256))
    x2d2 = x2.reshape(m, D_MODEL)

    tr = LANES
    off, sh, gate = _peer_route(x2d2, wts["w_pq"], wts["keys_a"], wts["keys_b"], tr)
    w_rows = _peer_in(off, sh, x2d2, gate, wts["table_u"], tr)
    y = _peer_out(off, w_rows, x2d2, wts["ln3_g"], wts["ln3_b"], wts["table_v"], tr)
    return y.reshape(b, t, D_MODEL), state


def kernel(x_prompt, x_sample, mem_prompt, cache_sb_k, cache_sb_v, cache_fox_k, cache_fox_v, cache_fox_logf,
           cache_mem_k, cache_mem_v, w_in, b_f, w_gn, w_out, ln1_g, ln1_b, w_mq, w_mk, w_mv, w_mo, ln2_g, ln2_b,
           w_pq, peer_keys_a, peer_keys_b, peer_u, peer_v, ln3_g, ln3_b):
    depth = w_in.shape[0]
    hp, hs = x_prompt, x_sample
    bp = x_prompt.shape[0]
    n_mem = mem_prompt.shape[1]
    mix_cols = 6 * GROUP_WIDTH
    outs_p = [[] for _ in range(7)]
    outs_s = [[] for _ in range(5)]
    row = lambda a: a.reshape(1, D_MODEL)
    for l in range(depth):
        wts = {
            "w_in_main": w_in[l][:, :mix_cols].astype(BF16),
            "w_in_f": jnp.pad(w_in[l][:, mix_cols:], ((0, 0), (0, LANES - N_HEADS))).astype(BF16),
            "b_f": jnp.pad(b_f[l], (0, LANES - N_HEADS)).reshape(1, LANES),
            "w_gn": row(w_gn[l]), "w_out": w_out[l].astype(BF16),
            "ln1_g": row(ln1_g[l]), "ln1_b": row(ln1_b[l]),
            "w_mq": w_mq[l].astype(BF16), "w_mo": w_mo[l].astype(BF16),
            "ln2_g": row(ln2_g[l]), "ln2_b": row(ln2_b[l]),
            "w_pq": w_pq[l].astype(BF16),
            "keys_a": peer_keys_a[l].astype(BF16), "keys_b": peer_keys_b[l].astype(BF16),
            "table_u": _pack_table(peer_u[l]), "table_v": _pack_table(peer_v[l]),
            "ln3_g": row(ln3_g[l]), "ln3_b": row(ln3_b[l]),
        }
        mem2d = mem_prompt.reshape(bp * n_mem, D_MODEL)
        mk_p, mv_p = _mem_kv(mem2d, w_mk[l].astype(BF16), w_mv[l].astype(BF16), _pick_tile(bp * n_mem, 512))
        mk_p = mk_p.reshape(bp, n_mem, D_MODEL)
        mv_p = mv_p.reshape(bp, n_mem, D_MODEL)
        hp, st_p = _layer(hp, None, mk_p, mv_p, wts)
        bs = x_sample.shape[0]
        past = (cache_sb_k[l], cache_sb_v[l], cache_fox_k[l], cache_fox_v[l], cache_fox_logf[l])
        hs, st_s = _layer(hs, past, cache_mem_k[l].reshape(bs, -1, D_MODEL), cache_mem_v[l].reshape(bs, -1, D_MODEL), wts)
        for i in range(5):
            outs_p[i].append(st_p[i])
            outs_s[i].append(st_s[i])
        outs_p[5].append(mk_p.reshape(bp, n_mem, MEM_HEADS, MEM_HEAD_DIM))
        outs_p[6].append(mv_p.reshape(bp, n_mem, MEM_HEADS, MEM_HEAD_DIM))
    stack = lambda xs: jnp.stack(xs)
    return (hp, hs) + tuple(stack(o) for o in outs_p) + tuple(stack(o) for o in outs_s)
```
